```python
import math
import jax
import jax.numpy as jnp
from jax import lax
import numpy as np


D_MODEL = 4096
BATCH = 4
SEQ = 2048
DEPTH = 2

GRID_W = 64
CTX_LEN = 256
NORM_EPS = 1e-6
ROPE_BASE = 10000.0
Q_BLOCK = 128

N_BRANCH = 4
BRANCH_W = D_MODEL // 4
D_FF = 4 * D_MODEL

RWKV_HEAD = 64
RWKV_HEADS = BRANCH_W // RWKV_HEAD
RWKV_DECAY_RANK = 64
RWKV_AAA_RANK = 64
RWKV_GATE_RANK = 128
RWKV_SHIFT = 3
RWKV_LN_EPS = 64e-5

MLA_HEADS = 8
MLA_NOPE = 128
MLA_ROPE = 64
MLA_V = BRANCH_W // MLA_HEADS
MLA_Q_RANK = 768
MLA_KV_RANK = 256

SSD_HEAD = 64
SSD_HEADS = BRANCH_W // SSD_HEAD
SSD_GROUPS = 4
SSD_HPG = SSD_HEADS // SSD_GROUPS
SSD_STATE = 128
SSD_CONV = 3
SSD_CHUNK = 128

DIFF_HEADS = 8
DIFF_HEAD = BRANCH_W // (2 * DIFF_HEADS)

RWKV_IN = 3 * BRANCH_W + RWKV_DECAY_RANK + RWKV_AAA_RANK + RWKV_GATE_RANK
MLA_IN = MLA_Q_RANK + MLA_KV_RANK + MLA_ROPE
SSD_IN = 2 * BRANCH_W + 2 * SSD_GROUPS * SSD_STATE + SSD_HEADS
DIFF_IN = 3 * BRANCH_W
MIX_IN_SIZES = (RWKV_IN, MLA_IN, SSD_IN, DIFF_IN)
MIX_IN = RWKV_IN + MLA_IN + SSD_IN + DIFF_IN

kernel_name = 'hybrid_gated_mixer_diffusion_block'

F32 = jnp.float32


def rms_norm(x, g, eps=NORM_EPS):
    xf = x.astype(F32)
    y = xf * lax.rsqrt(jnp.mean(xf * xf, axis=-1, keepdims=True) + eps)
    return (y * g.astype(F32)).astype(x.dtype)


def modulate(x, g, shift, scale):
    return rms_norm(x, g) * (1.0 + scale) + shift


def split_cols(x, sizes):
    outs, o = [], 0
    for s in sizes:
        outs.append(x[..., o:o + s])
        o += s
    return outs


def centred_dwconv(x, w):
    k = w.shape[0]
    return lax.conv_general_dilated(
        x, w[:, None, :].astype(x.dtype), window_strides=(1,), padding=[(k // 2, k // 2)],
        dimension_numbers=('NWC', 'WIO', 'NWC'), feature_group_count=x.shape[-1])


def axial_rope(n_tok, rot_dim):
    rows = n_tok // GRID_W
    t = jnp.arange(rows * GRID_W)
    row = (t // GRID_W).astype(F32)
    col = (t % GRID_W).astype(F32)
    axis_dim = rot_dim // 2
    inv = 1.0 / (ROPE_BASE ** (jnp.arange(0, axis_dim, 2, dtype=F32) / axis_dim))
    ar = row[:, None] * inv[None]
    ac = col[:, None] * inv[None]
    ang = jnp.concatenate([ar, ar, ac, ac], axis=-1)
    return jnp.cos(ang), jnp.sin(ang)


def apply_axial_rope(x, cos, sin):
    xf = x.astype(F32)
    a = x.shape[-1] // 2
    q = a // 2
    xr, xc = xf[..., :a], xf[..., a:]
    rot = jnp.concatenate([-xr[..., q:], xr[..., :q], -xc[..., q:], xc[..., :q]], axis=-1)
    return (xf * cos + rot * sin).astype(x.dtype)


def attn_probs(q, k, scale):
    s = jnp.einsum('bqhd,bkhd->bhqk', q.astype(F32), k.astype(F32)) * scale
    return jax.nn.softmax(s, axis=-1)


def over_query_blocks(fn, q):
    b, n = q.shape[:2]
    nb = n // Q_BLOCK
    qb = jnp.moveaxis(q.reshape(b, nb, Q_BLOCK, *q.shape[2:]), 1, 0)
    out = lax.map(fn, qb)
    return jnp.moveaxis(out, 0, 1).reshape(b, n, *out.shape[3:])


def softmax_reader(k, v, scale):
    kf, vf = k.astype(F32), v.astype(F32)

    def read(qb):
        return jnp.einsum('bhqk,bkhd->bqhd', attn_probs(qb, kf, scale), vf)
    return read


def rwkv_prep(p, shift_w, w0, w_up, a0, a_up, g_up, k_k, k_a):
    b, n = p.shape[:2]
    rkv, wd, ad, gd = split_cols(p, (3 * BRANCH_W, RWKV_DECAY_RANK, RWKV_AAA_RANK, RWKV_GATE_RANK))
    r, k, v = split_cols(centred_dwconv(rkv, shift_w).astype(F32), (BRANCH_W,) * 3)
    wd, ad, gd = wd.astype(F32), ad.astype(F32), gd.astype(F32)

    def heads(t):
        return t.reshape(b, n, RWKV_HEADS, RWKV_HEAD)
    kk = heads(k * k_k.astype(F32))
    kk = kk * lax.rsqrt(jnp.sum(kk * kk, axis=-1, keepdims=True) + 1e-12)
    g = jax.nn.sigmoid(gd) @ g_up.astype(F32)
    dirs = []
    for d in range(2):
        w_log = -jax.nn.softplus(-(w0[d].astype(F32) + jnp.tanh(wd) @ w_up[d].astype(F32))) - 0.5
        decay = jnp.exp(-jnp.exp(w_log))
        a = jax.nn.sigmoid(a0[d].astype(F32) + ad @ a_up[d].astype(F32))
        kt = k * (1.0 + (a - 1.0) * k_a.astype(F32))
        dirs.append((heads(decay), heads(kt), heads(a)))
    return heads(r), heads(v), kk, g, dirs


def rwkv_scan(s0, r, w, k, v, kk, a, reverse, emit):
    def step(s, inp):
        r_t, w_t, k_t, v_t, kk_t, a_t = inp
        s_kk = jnp.einsum('bhvk,bhk->bhv', s, kk_t)
        s = (s * w_t[:, :, None, :] - s_kk[..., None] * (kk_t * a_t)[:, :, None, :]
             + v_t[..., None] * k_t[:, :, None, :])
        return s, (jnp.einsum('bhvk,bhk->bhv', s, r_t) if emit else None)
    xs = tuple(jnp.moveaxis(t, 1, 0) for t in (r, w, k, v, kk, a))
    s, ys = lax.scan(step, s0, xs, reverse=reverse)
    return s, (jnp.moveaxis(ys, 0, 1) if emit else None)


def rwkv_readout(y, r, v, g, kts, r_k, ln_g, ln_b):
    b, n = y.shape[:2]
    mu = jnp.mean(y, axis=-1, keepdims=True)
    var = jnp.mean(jnp.square(y - mu), axis=-1, keepdims=True)
    y = ((y - mu) * lax.rsqrt(var + RWKV_LN_EPS)).reshape(b, n, BRANCH_W) * ln_g.astype(F32) + ln_b.astype(F32)
    rk = r_k.astype(F32)
    bonus = sum(jnp.sum(r * kt * rk, axis=-1, keepdims=True) * v for kt in kts)
    return (y + bonus.reshape(b, n, BRANCH_W)) * g


def rwkv_mixer(p_ctx, p_lat, shift_w, w0, w_up, a0, a_up, g_up, k_k, k_a, r_k, ln_g, ln_b, ctx_out):
    args = (shift_w, w0, w_up, a0, a_up, g_up, k_k, k_a)
    rc, vc, kkc, gc, dirs_c = rwkv_prep(p_ctx, *args)
    rl, vl, kkl, gl, dirs_l = rwkv_prep(p_lat, *args)
    s0 = jnp.zeros((p_lat.shape[0], RWKV_HEADS, RWKV_HEAD, RWKV_HEAD), F32)
    y_c, y_l = 0.0, 0.0
    for d, reverse in enumerate((False, True)):
        wc, ktc, ac = dirs_c[d]
        wl, ktl, al = dirs_l[d]
        s_ctx, yc = rwkv_scan(s0, rc, wc, ktc, vc, kkc, ac, reverse, ctx_out)
        _, yl = rwkv_scan(s_ctx, rl, wl, ktl, vl, kkl, al, reverse, True)
        y_l = y_l + yl
        if ctx_out:
            y_c = y_c + yc
    out_l = rwkv_readout(y_l, rl, vl, gl, [dl[1] for dl in dirs_l], r_k, ln_g, ln_b).astype(p_lat.dtype)
    out_c = None
    if ctx_out:
        out_c = rwkv_readout(y_c, rc, vc, gc, [dc[1] for dc in dirs_c], r_k, ln_g, ln_b).astype(p_ctx.dtype)
    return out_c, out_l


def mla_queries(cq, q_norm, w_uq, rope):
    b, n = cq.shape[:2]
    q = (rms_norm(cq, q_norm) @ w_uq).reshape(b, n, MLA_HEADS, MLA_NOPE + MLA_ROPE)
    if rope is not None:
        cos, sin = rope
        q = jnp.concatenate([q[..., :MLA_NOPE], apply_axial_rope(q[..., MLA_NOPE:], cos[:, None], sin[:, None])], axis=-1)
    return q


def mla_keys_values(ckv, kr, kv_norm, w_ukv, rope):
    b, n = ckv.shape[:2]
    kv = (rms_norm(ckv, kv_norm) @ w_ukv).reshape(b, n, MLA_HEADS, MLA_NOPE + MLA_V)
    if rope is not None:
        kr = apply_axial_rope(kr, rope[0], rope[1])
    kr = jnp.broadcast_to(kr[:, :, None, :], (b, n, MLA_HEADS, MLA_ROPE))
    return jnp.concatenate([kv[..., :MLA_NOPE], kr], axis=-1), kv[..., MLA_NOPE:]


def mla_mixer(p_ctx, p_lat, q_norm, w_uq, kv_norm, w_ukv, rope, ctx_out):
    sizes = (MLA_Q_RANK, MLA_KV_RANK, MLA_ROPE)
    cq_c, ckv_c, kr_c = split_cols(p_ctx, sizes)
    cq_l, ckv_l, kr_l = split_cols(p_lat, sizes)
    k_c, v_c = mla_keys_values(ckv_c, kr_c, kv_norm, w_ukv, None)
    k_l, v_l = mla_keys_values(ckv_l, kr_l, kv_norm, w_ukv, rope)
    q_l = mla_queries(cq_l, q_norm, w_uq, rope)
    scale = (MLA_NOPE + MLA_ROPE) ** -0.5
    b, n = p_lat.shape[:2]
    reader = softmax_reader(jnp.concatenate([k_c, k_l], axis=1), jnp.concatenate([v_c, v_l], axis=1), scale)
    y_l = over_query_blocks(reader, q_l).reshape(b, n, BRANCH_W).astype(p_lat.dtype)
    y_c = None
    if ctx_out:
        q_c = mla_queries(cq_c, q_norm, w_uq, None)
        y_c = over_query_blocks(softmax_reader(k_c, v_c, scale), q_c).reshape(b, p_ctx.shape[1], BRANCH_W).astype(p_ctx.dtype)
    return y_c, y_l


def ssd_prep(p, conv_w, conv_b):
    b, n = p.shape[:2]
    gn = SSD_GROUPS * SSD_STATE
    z, xbc, dt = split_cols(p, (BRANCH_W, BRANCH_W + 2 * gn, SSD_HEADS))
    xbc = jax.nn.silu((centred_dwconv(xbc, conv_w) + conv_b).astype(F32))
    xs, bm, cm = split_cols(xbc, (BRANCH_W, gn, gn))
    return (z.astype(F32), xs.reshape(b, n, SSD_HEADS, SSD_HEAD),
            bm.reshape(b, n, SSD_GROUPS, SSD_STATE), cm.reshape(b, n, SSD_GROUPS, SSD_STATE), dt.astype(F32))


def ssd_chunked(x, dt, A, bm, cm, state0, emit):
    b, n = x.shape[:2]
    nc = n // SSD_CHUNK

    def chunks(t):
        return jnp.moveaxis(t.reshape(b, nc, SSD_CHUNK, *t.shape[2:]), 1, 0)
    dtg = dt.reshape(b, n, SSD_GROUPS, SSD_HPG)
    xdt = x.reshape(b, n, SSD_GROUPS, SSD_HPG, SSD_HEAD) * dtg[..., None]
    log_a = dtg * A.reshape(SSD_GROUPS, SSD_HPG)
    lower = jnp.tril(jnp.ones((SSD_CHUNK, SSD_CHUNK), dtype=bool))[None, :, :, None, None]

    def step(state, inp):
        xdt_c, la_c, b_c, c_c = inp
        acs = jnp.cumsum(la_c, axis=1)
        total = acs[:, -1]
        to_end = jnp.exp(total[:, None] - acs)
        new_state = (state * jnp.exp(total)[..., None, None]
                     + jnp.einsum('bsgn,bsgr,bsgrp->bgrpn', b_c, to_end, xdt_c))
        if not emit:
            return new_state, None
        seg = acs[:, :, None] - acs[:, None, :]
        decay = jnp.exp(jnp.where(lower, seg, -jnp.inf))
        cb = jnp.einsum('blgn,bsgn->blsg', c_c, b_c)
        y = jnp.einsum('blsg,blsgr,bsgrp->blgrp', cb, decay, xdt_c)
        y = y + jnp.einsum('blgn,bgrpn->blgrp', c_c, state) * jnp.exp(acs)[..., None]
        return new_state, y
    state, ys = lax.scan(step, state0, (chunks(xdt), chunks(log_a), chunks(bm), chunks(cm)))
    if not emit:
        return state, None
    return state, jnp.moveaxis(ys, 0, 1).reshape(b, n, SSD_HEADS, SSD_HEAD)


def maybe_flip(t, rev):
    return jnp.flip(t, axis=1) if rev else t


def ssd_mixer(p_ctx, p_lat, conv_w, conv_b, dt_bias, a_log, d_skip, norm_g, ctx_out):
    zc, xc, bc, cc, dtc = ssd_prep(p_ctx, conv_w, conv_b)
    zl, xl, bl, cl, dtl = ssd_prep(p_lat, conv_w, conv_b)
    b = p_lat.shape[0]
    s0 = jnp.zeros((b, SSD_GROUPS, SSD_HPG, SSD_HEAD, SSD_STATE), F32)
    y_c, y_l = 0.0, 0.0
    for d in range(2):
        rev = d == 1
        A = -jnp.exp(a_log[d].astype(F32))
        bias = dt_bias[d].astype(F32)
        s_ctx, yc = ssd_chunked(maybe_flip(xc, rev), maybe_flip(jax.nn.softplus(dtc + bias), rev), A,
                                maybe_flip(bc, rev), maybe_flip(cc, rev), s0, ctx_out)
        _, yl = ssd_chunked(maybe_flip(xl, rev), maybe_flip(jax.nn.softplus(dtl + bias), rev), A,
                            maybe_flip(bl, rev), maybe_flip(cl, rev), s_ctx, True)
        y_l = y_l + maybe_flip(yl, rev)
        if ctx_out:
            y_c = y_c + maybe_flip(yc, rev)
    dsk = d_skip.astype(F32)[:, None]

    def readout(y, x, z):
        yy = (y + dsk * x).reshape(b, x.shape[1], BRANCH_W)
        return rms_norm(yy * jax.nn.silu(z), norm_g)
    out_l = readout(y_l, xl, zl).astype(p_lat.dtype)
    out_c = readout(y_c, xc, zc).astype(p_ctx.dtype) if ctx_out else None
    return out_c, out_l


def diff_split(p):
    b, n = p.shape[:2]
    q, k, v = split_cols(p, (BRANCH_W,) * 3)
    shp = (b, n, DIFF_HEADS, 2, DIFF_HEAD)
    return q.reshape(shp), k.reshape(shp), v.reshape(b, n, DIFF_HEADS, 2 * DIFF_HEAD)


def diff_reader(k, v, lam, lam_init, subln, scale):
    k1, k2 = k[..., 0, :].astype(F32), k[..., 1, :].astype(F32)
    vf = v.astype(F32)

    def read(qb):
        p = attn_probs(qb[..., 0, :], k1, scale) - lam * attn_probs(qb[..., 1, :], k2, scale)
        o = jnp.einsum('bhqk,bkhd->bqhd', p, vf)
        return rms_norm(o, subln) * (1.0 - lam_init)
    return read


def diff_mixer(p_ctx, p_lat, lq1, lk1, lq2, lk2, subln, rope, lam_init, ctx_out):
    b, n = p_lat.shape[:2]
    cos, sin = rope[0][:, None, None, :], rope[1][:, None, None, :]
    q_c, k_c, v_c = diff_split(p_ctx)
    q_l, k_l, v_l = diff_split(p_lat)
    q_l = apply_axial_rope(q_l, cos, sin)
    k_l = apply_axial_rope(k_l, cos, sin)
    lam = (jnp.exp(jnp.sum(lq1.astype(F32) * lk1.astype(F32)))
           - jnp.exp(jnp.sum(lq2.astype(F32) * lk2.astype(F32))) + lam_init)
    scale = DIFF_HEAD ** -0.5
    reader = diff_reader(jnp.concatenate([k_c, k_l], axis=1), jnp.concatenate([v_c, v_l], axis=1),
                         lam, lam_init, subln, scale)
    y_l = over_query_blocks(reader, q_l).reshape(b, n, BRANCH_W).astype(p_lat.dtype)
    y_c = None
    if ctx_out:
        y_c = over_query_blocks(diff_reader(k_c, v_c, lam, lam_init, subln, scale), q_c)
        y_c = y_c.reshape(b, p_ctx.shape[1], BRANCH_W).astype(p_ctx.dtype)
    return y_c, y_l


def merge_branches(h, ys, w_branch, w_gate, w_out):
    b, n = h.shape[:2]
    gates = jax.nn.sigmoid(h @ w_gate).reshape(b, n, N_BRANCH, D_MODEL)
    merged = sum(gates[:, :, i] * (ys[i] @ w_branch[i]) for i in range(N_BRANCH))
    return merged @ w_out


def sq_relu_mlp(h, w1, w2):
    return jnp.square(jax.nn.relu(h @ w1)) @ w2


def setup_inputs(seed: int = 0) -> dict:
    key = jax.random.key(seed)
    keys = iter(jax.random.split(key, 64))
    L = DEPTH

    def nrm(shape, scale):
        return jax.random.normal(next(keys), shape, F32) * scale

    def gain(shape):
        return 1.0 + nrm(shape, 0.02)

    def unif(shape, lo, hi):
        return jax.random.uniform(next(keys), shape, F32, lo, hi)

    dt0 = jnp.exp(unif((L, 2, SSD_HEADS), math.log(1e-3), math.log(1e-1)))
    return {
        'x': nrm((BATCH, SEQ, D_MODEL), 1.0),
        'c': nrm((BATCH, D_MODEL), 1.0),
        'ctx': nrm((BATCH, CTX_LEN, D_MODEL), 1.0),
        'c_ctx': nrm((D_MODEL,), 1.0),
        'ada_w': nrm((L, D_MODEL, 6 * D_MODEL), D_MODEL ** -0.5),
        'ada_b': nrm((L, 6 * D_MODEL), 0.02),
        'norm_mix_pre': gain((L, D_MODEL)),
        'norm_mix_post': gain((L, D_MODEL)),
        'norm_ffn_pre': gain((L, D_MODEL)),
        'norm_ffn_post': gain((L, D_MODEL)),
        'w_in': nrm((L, D_MODEL, MIX_IN), D_MODEL ** -0.5),
        'rwkv_shift': nrm((L, RWKV_SHIFT, 3 * BRANCH_W), 0.1).at[:, RWKV_SHIFT // 2].add(1.0),
        'rwkv_w0': unif((L, 2, BRANCH_W), -4.0, 1.0),
        'rwkv_w_up': nrm((L, 2, RWKV_DECAY_RANK, BRANCH_W), RWKV_DECAY_RANK ** -0.5),
        'rwkv_a0': nrm((L, 2, BRANCH_W), 0.5),
        'rwkv_a_up': nrm((L, 2, RWKV_AAA_RANK, BRANCH_W), RWKV_AAA_RANK ** -0.5),
        'rwkv_g_up': nrm((L, RWKV_GATE_RANK, BRANCH_W), RWKV_GATE_RANK ** -0.5),
        'rwkv_k_k': 0.85 + nrm((L, BRANCH_W), 0.02),
        'rwkv_k_a': gain((L, BRANCH_W)),
        'rwkv_r_k': nrm((L, RWKV_HEADS, RWKV_HEAD), 0.1),
        'rwkv_ln_g': gain((L, BRANCH_W)),
        'rwkv_ln_b': nrm((L, BRANCH_W), 0.02),
        'mla_q_norm': gain((L, MLA_Q_RANK)),
        'mla_w_uq': nrm((L, MLA_Q_RANK, MLA_HEADS * (MLA_NOPE + MLA_ROPE)), MLA_Q_RANK ** -0.5),
        'mla_kv_norm': gain((L, MLA_KV_RANK)),
        'mla_w_ukv': nrm((L, MLA_KV_RANK, MLA_HEADS * (MLA_NOPE + MLA_V)), MLA_KV_RANK ** -0.5),
        'ssd_conv_w': nrm((L, SSD_CONV, BRANCH_W + 2 * SSD_GROUPS * SSD_STATE), SSD_CONV ** -0.5),
        'ssd_conv_b': nrm((L, BRANCH_W + 2 * SSD_GROUPS * SSD_STATE), 0.02),
        'ssd_dt_bias': dt0 + jnp.log(-jnp.expm1(-dt0)),
        'ssd_a_log': jnp.log(unif((L, 2, SSD_HEADS), 1.0, 16.0)),
        'ssd_d': gain((L, SSD_HEADS)),
        'ssd_norm': gain((L, BRANCH_W)),
        'diff_lq1': nrm((L, DIFF_HEAD), 0.1),
        'diff_lk1': nrm((L, DIFF_HEAD), 0.1),
        'diff_lq2': nrm((L, DIFF_HEAD), 0.1),
        'diff_lk2': nrm((L, DIFF_HEAD), 0.1),
        'diff_subln': gain((L, 2 * DIFF_HEAD)),
        'w_branch': nrm((L, N_BRANCH, BRANCH_W, D_MODEL), BRANCH_W ** -0.5),
        'w_gate': nrm((L, D_MODEL, N_BRANCH * D_MODEL), D_MODEL ** -0.5),
        'w_out': nrm((L, D_MODEL, D_MODEL), D_MODEL ** -0.5),
        'w_ff1': nrm((L, D_MODEL, D_FF), D_MODEL ** -0.5),
        'w_ff2': nrm((L, D_FF, D_MODEL), D_FF ** -0.5),
    }


def reference(x, c, ctx, c_ctx, ada_w, ada_b, norm_mix_pre, norm_mix_post, norm_ffn_pre, norm_ffn_post,
              w_in, rwkv_shift, rwkv_w0, rwkv_w_up, rwkv_a0, rwkv_a_up, rwkv_g_up, rwkv_k_k, rwkv_k_a,
              rwkv_r_k, rwkv_ln_g, rwkv_ln_b, mla_q_norm, mla_w_uq, mla_kv_norm, mla_w_ukv,
              ssd_conv_w, ssd_conv_b, ssd_dt_bias, ssd_a_log, ssd_d, ssd_norm,
              diff_lq1, diff_lk1, diff_lq2, diff_lk2, diff_subln, w_branch, w_gate, w_out, w_ff1, w_ff2):
    n_lat = x.shape[1]
    rope_mla = axial_rope(n_lat, MLA_ROPE)
    rope_diff = axial_rope(n_lat, DIFF_HEAD)
    s_lat = jax.nn.silu(c)
    s_ctx = jax.nn.silu(c_ctx)
    for l in range(DEPTH):
        ctx_out = l < DEPTH - 1
        mod_l = jnp.split((s_lat @ ada_w[l] + ada_b[l])[:, None, :], 6, axis=-1)
        mod_c = jnp.split((s_ctx @ ada_w[l] + ada_b[l])[None, None, :], 6, axis=-1)

        h_l = modulate(x, norm_mix_pre[l], mod_l[0], mod_l[1])
        h_c = modulate(ctx, norm_mix_pre[l], mod_c[0], mod_c[1])
        pa_l, pb_l, pc_l, pd_l = split_cols(h_l @ w_in[l], MIX_IN_SIZES)
        pa_c, pb_c, pc_c, pd_c = split_cols(h_c @ w_in[l], MIX_IN_SIZES)
        ya = rwkv_mixer(pa_c, pa_l, rwkv_shift[l], rwkv_w0[l], rwkv_w_up[l], rwkv_a0[l], rwkv_a_up[l],
                        rwkv_g_up[l], rwkv_k_k[l], rwkv_k_a[l], rwkv_r_k[l], rwkv_ln_g[l], rwkv_ln_b[l], ctx_out)
        yb = mla_mixer(pb_c, pb_l, mla_q_norm[l], mla_w_uq[l], mla_kv_norm[l], mla_w_ukv[l], rope_mla, ctx_out)
        yc = ssd_mixer(pc_c, pc_l, ssd_conv_w[l], ssd_conv_b[l], ssd_dt_bias[l], ssd_a_log[l], ssd_d[l],
                       ssd_norm[l], ctx_out)
        lam_init = 0.8 - 0.6 * math.exp(-0.3 * l)
        yd = diff_mixer(pd_c, pd_l, diff_lq1[l], diff_lk1[l], diff_lq2[l], diff_lk2[l], diff_subln[l],
                        rope_diff, lam_init, ctx_out)
        mix_l = merge_branches(h_l, [ya[1], yb[1], yc[1], yd[1]], w_branch[l], w_gate[l], w_out[l])
        x = x + mod_l[2] * rms_norm(mix_l, norm_mix_post[l])
        if ctx_out:
            mix_c = merge_branches(h_c, [ya[0], yb[0], yc[0], yd[0]], w_branch[l], w_gate[l], w_out[l])
            ctx = ctx + mod_c[2] * rms_norm(mix_c, norm_mix_post[l])

        f_l = sq_relu_mlp(modulate(x, norm_ffn_pre[l], mod_l[3], mod_l[4]), w_ff1[l], w_ff2[l])
        x = x + mod_l[5] * rms_norm(f_l, norm_ffn_post[l])
        if ctx_out:
            f_c = sq_relu_mlp(modulate(ctx, norm_ffn_pre[l], mod_c[3], mod_c[4]), w_ff1[l], w_ff2[l])
            ctx = ctx + mod_c[5] * rms_norm(f_c, norm_ffn_post[l])
    return x
```

```python
import functools
import math

import jax
import jax.numpy as jnp
from jax import lax
from jax.experimental import pallas as pl
from jax.experimental.pallas import tpu as pltpu

F32 = jnp.float32
BF16 = jnp.bfloat16
HIGHEST = lax.Precision.HIGHEST

NORM_EPS = 1e-6
ROPE_BASE = 10000.0
GRID_W = 64
N_BRANCH = 4

RWKV_HEAD = 64
RWKV_DECAY_RANK = 64
RWKV_AAA_RANK = 64
RWKV_GATE_RANK = 128
RWKV_LN_EPS = 64e-5
RWKV_CHUNK = 64

MLA_HEADS = 8
MLA_NOPE = 128
MLA_ROPE = 64
MLA_Q_RANK = 768
MLA_KV_RANK = 256

SSD_HEAD = 64
SSD_GROUPS = 4
SSD_STATE = 128
SSD_CHUNK = 128

DIFF_HEADS = 8
DIFF_HEAD = 64

VMEM_LIMIT_BYTES = 56 * 1024 * 1024


def _params(*sem):
    return pltpu.CompilerParams(dimension_semantics=sem, vmem_limit_bytes=VMEM_LIMIT_BYTES)


def _dot(a, b):
    return jnp.dot(a, b, preferred_element_type=F32)


def _dot_nt(a, b):
    return lax.dot_general(a, b, (((1,), (1,)), ((), ())), preferred_element_type=F32)


def _dot_hi(a, b):
    return jnp.dot(a, b, preferred_element_type=F32, precision=HIGHEST)


def _apply_act(y, act):
    if act is None:
        return y
    if act == "sigmoid":
        return jax.nn.sigmoid(y)
    if act == "relu2":
        return jnp.square(jnp.maximum(y, 0.0))
    raise ValueError(act)


def _mm_kernel(x_ref, w_ref, o_ref, wbf_ref, *acc, nk, act, exact):
    k = pl.program_id(1)
    i = pl.program_id(2)

    @pl.when(i == 0)
    def _():
        wbf_ref[...] = w_ref[...].astype(wbf_ref.dtype)

    if exact:
        y = _dot_hi(x_ref[...].astype(F32), wbf_ref[...])
    else:
        y = _dot(x_ref[...].astype(BF16), wbf_ref[...])
    if nk == 1:
        o_ref[...] = _apply_act(y, act).astype(o_ref.dtype)
        return
    (acc_ref,) = acc
    tm = x_ref.shape[0]
    rows = pl.ds(pl.multiple_of(i * tm, tm), tm)

    @pl.when(k == 0)
    def _():
        acc_ref[rows, :] = y

    @pl.when(jnp.logical_and(k > 0, k < nk - 1))
    def _():
        acc_ref[rows, :] += y

    @pl.when(k == nk - 1)
    def _():
        o_ref[...] = _apply_act(acc_ref[rows, :] + y, act).astype(o_ref.dtype)


def matmul(x, w, *, layer=None, rows=None, tm=1024, tn=512, tk=4096, act=None, out_dtype=F32,
           exact=False):
    m = x.shape[0] if rows is None else rows
    kdim = x.shape[1]
    n = w.shape[-1]
    assert w.shape[-2] == kdim
    tm = math.gcd(tm, m)
    tk = min(tk, kdim)
    tn = min(tn, n)
    assert m % tm == 0 and kdim % tk == 0
    ni, nk, nj = m // tm, kdim // tk, pl.cdiv(n, tn)
    if w.ndim == 3:
        w_spec = pl.BlockSpec((None, tk, tn), lambda j, k, i: (layer, k, j))
    else:
        w_spec = pl.BlockSpec((tk, tn), lambda j, k, i: (k, j))
    if nk == 1:
        o_map = lambda j, k, i: (i, j)
    else:
        o_map = lambda j, k, i: (jnp.where(k == nk - 1, i, 0), j)
    scratch = [pltpu.VMEM((tk, tn), F32 if exact else BF16)]
    if nk > 1:
        scratch.append(pltpu.VMEM((m, tn), F32))
    return pl.pallas_call(
        functools.partial(_mm_kernel, nk=nk, act=act, exact=exact),
        grid=(nj, nk, ni),
        in_specs=[pl.BlockSpec((tm, tk), lambda j, k, i: (i, k)), w_spec],
        out_specs=pl.BlockSpec((tm, tn), o_map),
        out_shape=jax.ShapeDtypeStruct((m, n), out_dtype),
        scratch_shapes=scratch,
        compiler_params=_params("arbitrary", "arbitrary", "arbitrary"),
        name="matmul_ws",
    )(x, w)


def _merge_kernel(g0, g1, g2, g3, y0, y1, y2, y3, w_ref, o_ref, wbf_ref):
    @pl.when(pl.program_id(1) == 0)
    def _():
        wbf_ref[...] = w_ref[...].astype(BF16)

    acc = None
    for b, (g, y) in enumerate(((g0, y0), (g1, y1), (g2, y2), (g3, y3))):
        t = g[...].astype(F32) * _dot(y[...].astype(BF16), wbf_ref[b])
        acc = t if acc is None else acc + t
    o_ref[...] = acc.astype(o_ref.dtype)


def merge_branches(gates, ys, w_branch, layer, *, rows, tm=1024, tn=512):
    d = w_branch.shape[-1]
    bw = w_branch.shape[-2]
    tm = math.gcd(tm, rows)
    assert rows % tm == 0 and d % tn == 0
    nj = d // tn
    g_specs = [pl.BlockSpec((tm, tn), functools.partial(lambda j, i, b: (i, b * nj + j), b=b))
               for b in range(N_BRANCH)]
    y_specs = [pl.BlockSpec((tm, bw), lambda j, i: (i, 0)) for _ in range(N_BRANCH)]
    return pl.pallas_call(
        _merge_kernel,
        grid=(nj, rows // tm),
        in_specs=g_specs + y_specs + [pl.BlockSpec((None, N_BRANCH, bw, tn), lambda j, i: (layer, 0, 0, j))],
        out_specs=pl.BlockSpec((tm, tn), lambda j, i: (i, j)),
        out_shape=jax.ShapeDtypeStruct((rows, d), BF16),
        scratch_shapes=[pltpu.VMEM((N_BRANCH, bw, tn), BF16)],
        compiler_params=_params("arbitrary", "arbitrary"),
        name="merge_branches",
    )(gates, gates, gates, gates, *ys, w_branch)


def _rms(x, eps=NORM_EPS):
    return x * lax.rsqrt(jnp.mean(x * x, axis=-1, keepdims=True) + eps)


def _norm_mod_kernel(x_ref, g_ref, shift_ref, scale_ref, o_ref):
    x = x_ref[...].astype(F32)
    y = _rms(x) * g_ref[...]
    o_ref[...] = (y * (1.0 + scale_ref[...]) + shift_ref[...]).astype(o_ref.dtype)


def norm_modulate(x, g, mod, shift_idx, scale_idx, *, rows, rows_per_group, tm=256):
    d = x.shape[1]
    tm = min(tm, rows_per_group)
    assert rows % tm == 0 and rows_per_group % tm == 0
    n_groups = mod.shape[0]

    def grp(i):
        return jnp.minimum((i * tm) // rows_per_group, n_groups - 1)
    return pl.pallas_call(
        _norm_mod_kernel,
        grid=(rows // tm,),
        in_specs=[pl.BlockSpec((tm, d), lambda i: (i, 0)),
                  pl.BlockSpec((1, d), lambda i: (0, 0)),
                  pl.BlockSpec((None, 1, d), lambda i: (grp(i), 0, shift_idx)),
                  pl.BlockSpec((None, 1, d), lambda i: (grp(i), 0, scale_idx))],
        out_specs=pl.BlockSpec((tm, d), lambda i: (i, 0)),
        out_shape=jax.ShapeDtypeStruct((rows, d), BF16),
        compiler_params=_params("arbitrary"),
        name="norm_modulate",
    )(x, g.reshape(1, d), mod, mod)


def _norm_res_kernel(x_ref, y_ref, g_ref, m_ref, o_ref):
    y = _rms(y_ref[...].astype(F32)) * g_ref[...]
    o_ref[...] = x_ref[...] + m_ref[...] * y


def norm_residual(x, y, g, mod, gate_idx, *, rows, rows_per_group, tm=256):
    d = x.shape[1]
    tm = min(tm, rows_per_group)
    assert rows % tm == 0 and rows_per_group % tm == 0
    n_groups = mod.shape[0]

    def grp(i):
        return jnp.minimum((i * tm) // rows_per_group, n_groups - 1)
    return pl.pallas_call(
        _norm_res_kernel,
        grid=(rows // tm,),
        in_specs=[pl.BlockSpec((tm, d), lambda i: (i, 0)),
                  pl.BlockSpec((tm, d), lambda i: (i, 0)),
                  pl.BlockSpec((1, d), lambda i: (0, 0)),
                  pl.BlockSpec((None, 1, d), lambda i: (grp(i), 0, gate_idx))],
        out_specs=pl.BlockSpec((tm, d), lambda i: (i, 0)),
        out_shape=jax.ShapeDtypeStruct((rows, d), F32),
        compiler_params=_params("arbitrary"),
        name="norm_residual",
    )(x, y, g.reshape(1, d), mod)


def _rwkv_kernel(r_ref, lw_ref, k_ref, v_ref, kk_ref, b_ref, lwt_ref, kt_ref, bt_ref, y_ref, s_ref,
                 *, heads, chunk):
    @pl.when(pl.program_id(2) == 0)
    def _():
        s_ref[...] = jnp.zeros_like(s_ref)

    row = lax.broadcasted_iota(jnp.int32, (chunk, chunk), 0)
    col = lax.broadcasted_iota(jnp.int32, (chunk, chunk), 1)
    incl = row >= col
    strict = row > col
    tril_ones = incl.astype(F32)
    triu_ones = (row <= col).astype(F32)
    n_levels = chunk.bit_length() - 1
    for h in range(heads):
        r, lw, k, v, kk, b = (ref[h] for ref in (r_ref, lw_ref, k_ref, v_ref, kk_ref, b_ref))
        lam = _dot_hi(tril_ones, lw)
        g_inv = jnp.exp(-lam)
        a_t = -kk * jnp.exp(lam - lw)
        b_t = (b * g_inv).astype(BF16)
        k_t = (k * g_inv).astype(BF16)
        r_t = r * jnp.exp(lam)
        ar = jnp.concatenate([a_t, r_t], axis=0).astype(BF16)
        p_b = _dot_nt(ar, b_t)
        p_k = _dot_nt(ar, k_t)
        l_ab = jnp.where(strict, p_b[:chunk], 0.0)
        l_ak = jnp.where(strict, p_k[:chunk], 0.0)
        m_rb = jnp.where(incl, p_b[chunk:], 0.0)
        m_rk = jnp.where(incl, p_k[chunk:], 0.0)
        vb = v.astype(BF16)
        x_w = a_t
        x_u = _dot(l_ak.astype(BF16), vb)
        lp = l_ab
        for lvl in range(n_levels):
            lpb = lp.astype(BF16)
            x_w = x_w + _dot(lpb, x_w.astype(BF16))
            x_u = x_u + _dot(lpb, x_u.astype(BF16))
            if lvl + 1 < n_levels:
                lp = _dot(lpb, lpb)
        s0 = s_ref[h]
        s0b = s0.astype(BF16)
        u = _dot(x_w.astype(BF16), s0b) + x_u
        ub = u.astype(BF16)
        y = _dot(r_t.astype(BF16), s0b) + _dot(m_rb.astype(BF16), ub) + _dot(m_rk.astype(BF16), vb)
        y_ref[h] = y
        lam_t = _dot_hi(lwt_ref[h, 0], triu_ones)
        lam_end = lam_t[:, chunk - 1:chunk]
        to_end = jnp.exp(lam_end - lam_t)
        b_c = (bt_ref[h, 0] * to_end).astype(BF16)
        k_c = (kt_ref[h, 0] * to_end).astype(BF16)
        s_ref[h] = jnp.exp(lam_end) * s0 + _dot(b_c, ub) + _dot(k_c, vb)


def rwkv_scan(r, lw, k, v, kk, b, *, heads_per_step=8, chunk=RWKV_CHUNK):
    g, h, n, dh = r.shape
    assert n % chunk == 0 and h % heads_per_step == 0
    nc = n // chunk

    def chunk_t(t):
        return jnp.swapaxes(t.reshape(g, h, nc, chunk, dh), -1, -2)
    hb = heads_per_step
    spec = pl.BlockSpec((None, hb, chunk, dh), lambda gi, hi, c: (gi, hi, c, 0))
    spec_t = pl.BlockSpec((None, hb, 1, dh, chunk), lambda gi, hi, c: (gi, hi, c, 0, 0))
    return pl.pallas_call(
        functools.partial(_rwkv_kernel, heads=hb, chunk=chunk),
        grid=(g, h // hb, nc),
        in_specs=[spec] * 6 + [spec_t] * 3,
        out_specs=spec,
        out_shape=jax.ShapeDtypeStruct((g, h, n, dh), F32),
        scratch_shapes=[pltpu.VMEM((hb, dh, dh), F32)],
        compiler_params=_params("arbitrary", "arbitrary", "arbitrary"),
        name="rwkv_scan",
    )(r, lw, k, v, kk, b, chunk_t(lw), chunk_t(k), chunk_t(b))


def _ssd_kernel(x_ref, b_ref, c_ref, bt_ref, la_ref, lat_ref, dt_ref, y_ref, s_ref, *, heads, groups, chunk):
    @pl.when(pl.program_id(1) == 0)
    def _():
        s_ref[...] = jnp.zeros_like(s_ref)

    row = lax.broadcasted_iota(jnp.int32, (chunk, chunk), 0)
    col = lax.broadcasted_iota(jnp.int32, (chunk, chunk), 1)
    lower = row >= col
    acs_c_all = _dot_hi(lower.astype(F32), la_ref[...])
    acs_r_all = _dot_hi(lat_ref[...], (row <= col).astype(F32))
    dt_all = dt_ref[...]
    hpg = heads // groups
    for gi in range(groups):
        cg = c_ref[gi].astype(BF16)
        cb = _dot_nt(cg, b_ref[gi].astype(BF16))
        btg = bt_ref[gi, 0].astype(BF16)
        for ri in range(hpg):
            h = gi * hpg + ri
            acs_c = acs_c_all[:, h:h + 1]
            acs_r = acs_r_all[h:h + 1, :]
            total = acs_r[:, chunk - 1:chunk]
            decay = jnp.exp(jnp.where(lower, acs_c - acs_r, -1e30))
            xdt = x_ref[h] * dt_all[:, h:h + 1]
            st = s_ref[h]
            y = _dot((cb * decay).astype(BF16), xdt.astype(BF16))
            y = y + _dot(cg, st.astype(BF16)) * jnp.exp(acs_c)
            y_ref[h] = y
            to_end = jnp.exp(total - acs_c)
            s_ref[h] = st * jnp.exp(total) + _dot(btg, (xdt * to_end).astype(BF16))


def ssd_scan(x, bm, cm, la, dt, *, chunk=SSD_CHUNK):
    g, h, n, p = x.shape
    groups, ns = bm.shape[1], bm.shape[3]
    assert n % chunk == 0
    nc = n // chunk
    bt = jnp.swapaxes(bm.reshape(g, groups, nc, chunk, ns), -1, -2)
    lat = jnp.swapaxes(la, 1, 2)
    return pl.pallas_call(
        functools.partial(_ssd_kernel, heads=h, groups=groups, chunk=chunk),
        grid=(g, nc),
        in_specs=[pl.BlockSpec((None, h, chunk, p), lambda gi, c: (gi, 0, c, 0)),
                  pl.BlockSpec((None, groups, chunk, ns), lambda gi, c: (gi, 0, c, 0)),
                  pl.BlockSpec((None, groups, chunk, ns), lambda gi, c: (gi, 0, c, 0)),
                  pl.BlockSpec((None, groups, 1, ns, chunk), lambda gi, c: (gi, 0, c, 0, 0)),
                  pl.BlockSpec((None, chunk, h), lambda gi, c: (gi, c, 0)),
                  pl.BlockSpec((None, h, chunk), lambda gi, c: (gi, 0, c)),
                  pl.BlockSpec((None, chunk, h), lambda gi, c: (gi, c, 0))],
        out_specs=pl.BlockSpec((None, h, chunk, p), lambda gi, c: (gi, 0, c, 0)),
        out_shape=jax.ShapeDtypeStruct((g, h, n, p), F32),
        scratch_shapes=[pltpu.VMEM((h, ns, p), F32)],
        compiler_params=_params("arbitrary", "arbitrary"),
        name="ssd_scan",
    )(x, bm, cm, bt, la, lat, dt)


def _softmax_pv(scores, values):
    m = None
    for s in scores:
        ms = jnp.max(s, axis=-1, keepdims=True)
        m = ms if m is None else jnp.maximum(m, ms)
    l, o = None, None
    for s, v in zip(scores, values):
        p = jnp.exp(s - m)
        ls = jnp.sum(p, axis=-1, keepdims=True)
        os_ = _dot(p.astype(BF16), v)
        l = ls if l is None else l + ls
        o = os_ if o is None else o + os_
    return o / l


def _attn_kernel(lam_ref, q_ref, g_ref, *refs, n_seg, scale, diff, out_scale):
    seg_refs = refs[:2 * n_seg]
    o_ref = refs[2 * n_seg]
    q = q_ref[...].astype(F32) * scale
    values = [seg_refs[2 * si + 1][...] for si in range(n_seg)]
    if not diff:
        qb = q.astype(BF16)
        scores = [_dot_nt(qb, seg_refs[2 * si][...]) for si in range(n_seg)]
        o_ref[...] = _softmax_pv(scores, values).astype(o_ref.dtype)
        return
    first = lax.broadcasted_iota(jnp.int32, q.shape, 1) < DIFF_HEAD
    q1 = jnp.where(first, q, 0.0).astype(BF16)
    q2 = jnp.where(first, 0.0, q).astype(BF16)
    o1 = _softmax_pv([_dot_nt(q1, seg_refs[2 * si][...]) for si in range(n_seg)], values)
    o2 = _softmax_pv([_dot_nt(q2, seg_refs[2 * si][...]) for si in range(n_seg)], values)
    o = o1 - lam_ref[0] * o2
    o_ref[...] = (_rms(o) * g_ref[...] * out_scale).astype(o_ref.dtype)


def attention(q, k, v, *, heads, batch, q_rows, q_first, segments, scale, tq=256,
              diff_lam=None, diff_gain=None, out_scale=1.0):
    dqk = q.shape[1] // heads
    dv = v.shape[1] // heads
    tq = min(tq, q_rows)
    nq = q_rows // tq
    diff = diff_lam is not None
    lam = (diff_lam if diff else jnp.zeros((), F32)).reshape(1).astype(F32)
    gain = (diff_gain if diff else jnp.ones((dv,), F32)).reshape(1, dv).astype(F32)
    seg_specs = []
    for rows, first in segments:
        for w in (dqk, dv):
            seg_specs.append(pl.BlockSpec((rows, w), functools.partial(lambda b, h, i, f: (f + b, h), f=first)))
    return pl.pallas_call(
        functools.partial(_attn_kernel, n_seg=len(segments), scale=scale, diff=diff, out_scale=out_scale),
        grid=(batch, heads, nq),
        in_specs=[pl.BlockSpec(memory_space=pltpu.SMEM),
                  pl.BlockSpec((tq, dqk), lambda b, h, i: ((q_first + b) * nq + i, h)),
                  pl.BlockSpec((1, dv), lambda b, h, i: (0, 0))] + seg_specs,
        out_specs=pl.BlockSpec((tq, dv), lambda b, h, i: (b * nq + i, h)),
        out_shape=jax.ShapeDtypeStruct((batch * q_rows, heads * dv), BF16),
        compiler_params=_params("arbitrary", "arbitrary", "arbitrary"),
        name="diff_attention" if diff else "mla_attention",
    )(lam, q, gain, *([k, v] * len(segments)))


class _Rows:
    def __init__(self, batch, n_lat, n_ctx):
        self.batch, self.n_lat, self.n_ctx = batch, n_lat, n_ctx
        self.m_lat = batch * n_lat
        self.m = batch * (n_lat + n_ctx)

    def split(self, t):
        c = t.shape[-1]
        return (t[:self.m_lat].reshape(self.batch, self.n_lat, c),
                t[self.m_lat:].reshape(self.batch, self.n_ctx, c))

    def join(self, lat, ctx):
        c = lat.shape[-1]
        return jnp.concatenate([lat.reshape(self.m_lat, c), ctx.reshape(self.batch * self.n_ctx, c)], axis=0)

    def per_seq(self, fn, t):
        lat, ctx = self.split(t)
        return self.join(fn(lat), fn(ctx))

    def sequences(self, t_fwd, t_rev, heads):
        lf, cf = self.split(t_fwd)
        lr, cr = self.split(t_rev)
        fwd = jnp.concatenate([cf, lf], axis=1)
        rev = jnp.concatenate([jnp.flip(cr, axis=1), jnp.flip(lr, axis=1)], axis=1)
        s = jnp.concatenate([fwd, rev], axis=0)
        s = s.reshape(2 * self.batch, self.n_ctx + self.n_lat, heads, -1)
        return jnp.swapaxes(s, 1, 2)

    def sequences_flat(self, t_fwd, t_rev):
        lf, cf = self.split(t_fwd)
        lr, cr = self.split(t_rev)
        fwd = jnp.concatenate([cf, lf], axis=1)
        rev = jnp.concatenate([jnp.flip(cr, axis=1), jnp.flip(lr, axis=1)], axis=1)
        return jnp.concatenate([fwd, rev], axis=0)

    def merge_dirs(self, y):
        b = self.batch
        y = jnp.swapaxes(y, 1, 2).reshape(2 * b, self.n_ctx + self.n_lat, -1)
        yf, yr = y[:b], y[b:]
        ctx = yf[:, :self.n_ctx] + jnp.flip(yr[:, :self.n_ctx], axis=1)
        lat = yf[:, self.n_ctx:] + jnp.flip(yr[:, self.n_ctx:], axis=1)
        return self.join(lat, ctx)


def _conv3(x, w):
    z = jnp.zeros_like(x[:, :1])
    prev = jnp.concatenate([z, x[:, :-1]], axis=1)
    nxt = jnp.concatenate([x[:, 1:], z], axis=1)
    return prev * w[0] + x * w[1] + nxt * w[2]


def _axial_rope_tables(n_tok, rot_dim):
    t = jnp.arange(n_tok)
    row = (t // GRID_W).astype(F32)
    col = (t % GRID_W).astype(F32)
    axis_dim = rot_dim // 2
    inv = 1.0 / (ROPE_BASE ** (jnp.arange(0, axis_dim, 2, dtype=F32) / axis_dim))
    ar = row[:, None] * inv[None]
    ac = col[:, None] * inv[None]
    ang = jnp.concatenate([ar, ar, ac, ac], axis=-1)
    return jnp.cos(ang), jnp.sin(ang)


def _apply_rope(x, cos, sin):
    a = x.shape[-1] // 2
    q = a // 2
    xr, xc = x[..., :a], x[..., a:]
    rot = jnp.concatenate([-xr[..., q:], xr[..., :q], -xc[..., q:], xc[..., :q]], axis=-1)
    return x * cos + rot * sin


def kernel(x, c, ctx, c_ctx, ada_w, ada_b, norm_mix_pre, norm_mix_post, norm_ffn_pre, norm_ffn_post, w_in, rwkv_shift, rwkv_w0, rwkv_w_up, rwkv_a0, rwkv_a_up, rwkv_g_up, rwkv_k_k, rwkv_k_a, rwkv_r_k, rwkv_ln_g, rwkv_ln_b, mla_q_norm, mla_w_uq, mla_kv_norm, mla_w_ukv, ssd_conv_w, ssd_conv_b, ssd_dt_bias, ssd_a_log, ssd_d, ssd_norm, diff_lq1, diff_lk1, diff_lq2, diff_lk2, diff_subln, w_branch, w_gate, w_out, w_ff1, w_ff2):
    batch, n_lat, d = x.shape
    n_ctx = ctx.shape[1]
    depth = ada_w.shape[0]
    bw = d // N_BRANCH
    rows = _Rows(batch, n_lat, n_ctx)
    m_lat, m_all = rows.m_lat, rows.m
    rwkv_heads = bw // RWKV_HEAD
    ssd_heads = bw // SSD_HEAD
    gn = SSD_GROUPS * SSD_STATE
    rwkv_in = 3 * bw + RWKV_DECAY_RANK + RWKV_AAA_RANK + RWKV_GATE_RANK
    mla_in = MLA_Q_RANK + MLA_KV_RANK + MLA_ROPE
    ssd_in = 2 * bw + 2 * gn + ssd_heads
    diff_in = 3 * bw
    o_mla, o_ssd, o_diff = rwkv_in, rwkv_in + mla_in, rwkv_in + mla_in + ssd_in
    assert w_in.shape[-1] == o_diff + diff_in
    assert n_lat % 128 == 0 and n_ctx % 128 == 0

    cos_m, sin_m = _axial_rope_tables(n_lat, MLA_ROPE)
    cos_d, sin_d = _axial_rope_tables(n_lat, DIFF_HEAD)

    xs = jnp.concatenate([x.reshape(m_lat, d), ctx.reshape(batch * n_ctx, d)], axis=0)
    cvec = jnp.zeros((8, d), F32).at[:batch].set(jax.nn.silu(c)).at[batch].set(jax.nn.silu(c_ctx))
    groups = batch + 1

    for l in range(depth):
        ctx_out = l < depth - 1
        m_out = m_all if ctx_out else m_lat
        mod = (matmul(cvec, ada_w, layer=l, tn=512) + ada_b[l])[:groups].reshape(groups, 1, 6 * d)

        h = norm_modulate(xs, norm_mix_pre[l], mod, 0, 1, rows=m_all, rows_per_group=n_lat)
        p = matmul(h, w_in, layer=l)
        pa, pb, pc, pd = p[:, :o_mla], p[:, o_mla:o_ssd], p[:, o_ssd:o_diff], p[:, o_diff:]

        rkv = rows.per_seq(lambda t: _conv3(t, rwkv_shift[l]), pa[:, :3 * bw])
        r, k, v = rkv[:, :bw], rkv[:, bw:2 * bw], rkv[:, 2 * bw:]
        wd = pa[:, 3 * bw:3 * bw + RWKV_DECAY_RANK]
        ad = pa[:, 3 * bw + RWKV_DECAY_RANK:3 * bw + RWKV_DECAY_RANK + RWKV_AAA_RANK]
        gd = pa[:, 3 * bw + RWKV_DECAY_RANK + RWKV_AAA_RANK:]
        kk = (k * rwkv_k_k[l]).reshape(m_all, rwkv_heads, RWKV_HEAD)
        kk = (kk * lax.rsqrt(jnp.sum(kk * kk, axis=-1, keepdims=True) + 1e-12)).reshape(m_all, bw)
        g_gate = matmul(jax.nn.sigmoid(gd), rwkv_g_up, layer=l, exact=True)
        w_pre = matmul(jnp.tanh(wd), jnp.concatenate([rwkv_w_up[l, 0], rwkv_w_up[l, 1]], axis=1), exact=True)
        a_pre = matmul(ad, jnp.concatenate([rwkv_a_up[l, 0], rwkv_a_up[l, 1]], axis=1), exact=True)
        lws, kts, bs = [], [], []
        for dr in range(2):
            w_log = -jax.nn.softplus(-(rwkv_w0[l, dr] + w_pre[:, dr * bw:(dr + 1) * bw])) - 0.5
            lws.append(-jnp.exp(w_log))
            a = jax.nn.sigmoid(rwkv_a0[l, dr] + a_pre[:, dr * bw:(dr + 1) * bw])
            kts.append(k * (1.0 + (a - 1.0) * rwkv_k_a[l]))
            bs.append(kk * a)
        y_rwkv = rwkv_scan(rows.sequences(r, r, rwkv_heads), rows.sequences(lws[0], lws[1], rwkv_heads),
                           rows.sequences(kts[0], kts[1], rwkv_heads), rows.sequences(v, v, rwkv_heads),
                           rows.sequences(kk, kk, rwkv_heads), rows.sequences(bs[0], bs[1], rwkv_heads))
        y = rows.merge_dirs(y_rwkv).reshape(m_all, rwkv_heads, RWKV_HEAD)
        mu = jnp.mean(y, axis=-1, keepdims=True)
        var = jnp.mean(jnp.square(y - mu), axis=-1, keepdims=True)
        y = ((y - mu) * lax.rsqrt(var + RWKV_LN_EPS)).reshape(m_all, bw) * rwkv_ln_g[l] + rwkv_ln_b[l]
        rh = r.reshape(m_all, rwkv_heads, RWKV_HEAD)
        bonus = sum(jnp.sum(rh * kt.reshape(m_all, rwkv_heads, RWKV_HEAD) * rwkv_r_k[l], axis=-1, keepdims=True)
                    for kt in kts) * v.reshape(m_all, rwkv_heads, RWKV_HEAD)
        ya = ((y + bonus.reshape(m_all, bw)) * g_gate).astype(BF16)

        cq, ckv, kr = pb[:, :MLA_Q_RANK], pb[:, MLA_Q_RANK:MLA_Q_RANK + MLA_KV_RANK], pb[:, MLA_Q_RANK + MLA_KV_RANK:]
        w_uq = mla_w_uq[l].reshape(MLA_Q_RANK, MLA_HEADS, MLA_NOPE + MLA_ROPE)
        w_uq = jnp.concatenate([w_uq[:, :, :MLA_NOPE].reshape(MLA_Q_RANK, -1),
                                w_uq[:, :, MLA_NOPE:].reshape(MLA_Q_RANK, -1)], axis=1)
        w_ukv = mla_w_ukv[l].reshape(MLA_KV_RANK, MLA_HEADS, 2 * MLA_NOPE)
        w_ukv = jnp.concatenate([w_ukv[:, :, :MLA_NOPE].reshape(MLA_KV_RANK, -1),
                                 w_ukv[:, :, MLA_NOPE:].reshape(MLA_KV_RANK, -1)], axis=1)
        qfull = matmul(_rms(cq) * mla_q_norm[l], w_uq, tn=512)
        kvfull = matmul(_rms(ckv) * mla_kv_norm[l], w_ukv, tn=512)
        nope_w = MLA_HEADS * MLA_NOPE
        qn, qr = qfull[:, :nope_w], qfull[:, nope_w:]
        kn, vv = kvfull[:, :nope_w], kvfull[:, nope_w:]
        qr_l, qr_c = rows.split(qr)
        qr_l = _apply_rope(qr_l.reshape(batch, n_lat, MLA_HEADS, MLA_ROPE), cos_m[:, None], sin_m[:, None])
        qr = rows.join(qr_l.reshape(batch, n_lat, -1), qr_c)
        kr_l, kr_c = rows.split(kr)
        kr = rows.join(_apply_rope(kr_l, cos_m, sin_m), kr_c)
        pad = jnp.zeros((m_all, MLA_HEADS, MLA_NOPE - MLA_ROPE), BF16)
        q_m = jnp.concatenate([qn.astype(BF16).reshape(m_all, MLA_HEADS, MLA_NOPE),
                               qr.astype(BF16).reshape(m_all, MLA_HEADS, MLA_ROPE), pad], axis=-1)
        k_m = jnp.concatenate([kn.astype(BF16).reshape(m_all, MLA_HEADS, MLA_NOPE),
                               jnp.broadcast_to(kr.astype(BF16)[:, None, :], (m_all, MLA_HEADS, MLA_ROPE)), pad],
                              axis=-1)
        q_m, k_m, vv = q_m.reshape(m_all, -1), k_m.reshape(m_all, -1), vv.astype(BF16)
        ctx_blk = m_lat // n_ctx
        segs_lat = [(n_lat, 0), (n_ctx, ctx_blk)]
        mla_scale = (MLA_NOPE + MLA_ROPE) ** -0.5
        yb = attention(q_m, k_m, vv, heads=MLA_HEADS, batch=batch, q_rows=n_lat, q_first=0, segments=segs_lat,
                       scale=mla_scale)
        if ctx_out:
            yb_c = attention(q_m, k_m, vv, heads=MLA_HEADS, batch=batch, q_rows=n_ctx, q_first=ctx_blk,
                             segments=[(n_ctx, ctx_blk)], scale=mla_scale)
            yb = jnp.concatenate([yb, yb_c], axis=0)

        z = pc[:, :bw]
        xbc = rows.per_seq(lambda t: _conv3(t, ssd_conv_w[l]) + ssd_conv_b[l], pc[:, bw:2 * bw + 2 * gn])
        xbc = jax.nn.silu(xbc)
        x_ssd, bm, cm = xbc[:, :bw], xbc[:, bw:bw + gn], xbc[:, bw + gn:]
        dt_raw = pc[:, 2 * bw + 2 * gn:]
        dts = [jax.nn.softplus(dt_raw + ssd_dt_bias[l, dr]) for dr in range(2)]
        las = [dts[dr] * (-jnp.exp(ssd_a_log[l, dr])) for dr in range(2)]
        y_ssd = ssd_scan(rows.sequences(x_ssd, x_ssd, ssd_heads), rows.sequences(bm, bm, SSD_GROUPS),
                         rows.sequences(cm, cm, SSD_GROUPS), rows.sequences_flat(las[0], las[1]),
                         rows.sequences_flat(dts[0], dts[1]))
        y = rows.merge_dirs(y_ssd)
        y = y + (x_ssd.reshape(m_all, ssd_heads, SSD_HEAD) * ssd_d[l][:, None]).reshape(m_all, bw)
        yc = (_rms(y * jax.nn.silu(z)) * ssd_norm[l]).astype(BF16)

        q_d, k_d, v_d = pd[:, :bw], pd[:, bw:2 * bw], pd[:, 2 * bw:]

        def rope_d(t):
            tl, tc = rows.split(t)
            tl = _apply_rope(tl.reshape(batch, n_lat, 2 * DIFF_HEADS, DIFF_HEAD), cos_d[:, None], sin_d[:, None])
            return rows.join(tl.reshape(batch, n_lat, bw), tc)
        q_d, k_d, v_d = rope_d(q_d).astype(BF16), rope_d(k_d).astype(BF16), v_d.astype(BF16)
        lam_init = 0.8 - 0.6 * math.exp(-0.3 * l)
        lam = (jnp.exp(jnp.sum(diff_lq1[l] * diff_lk1[l])) - jnp.exp(jnp.sum(diff_lq2[l] * diff_lk2[l])) + lam_init)
        diff_kw = dict(heads=DIFF_HEADS, batch=batch, scale=DIFF_HEAD ** -0.5, diff_lam=lam,
                       diff_gain=diff_subln[l], out_scale=1.0 - lam_init)
        yd = attention(q_d, k_d, v_d, q_rows=n_lat, q_first=0, segments=segs_lat, **diff_kw)
        if ctx_out:
            yd_c = attention(q_d, k_d, v_d, q_rows=n_ctx, q_first=ctx_blk, segments=[(n_ctx, ctx_blk)], **diff_kw)
            yd = jnp.concatenate([yd, yd_c], axis=0)

        gates = matmul(h, w_gate, layer=l, rows=m_out, act="sigmoid", out_dtype=BF16)
        merged = merge_branches(gates, [ya, yb, yc, yd], w_branch, l, rows=m_out)
        mix = matmul(merged, w_out, layer=l)
        xs = norm_residual(xs, mix, norm_mix_post[l], mod, 2, rows=m_out, rows_per_group=n_lat)

        hf = norm_modulate(xs, norm_ffn_pre[l], mod, 3, 4, rows=m_out, rows_per_group=n_lat)
        f1 = matmul(hf, w_ff1, layer=l, act="relu2", out_dtype=BF16)
        f2 = matmul(f1, w_ff2, layer=l, tm=512)
        xs = norm_residual(xs, f2, norm_ffn_post[l], mod, 5, rows=m_out, rows_per_group=n_lat)

    return xs[:m_lat].reshape(batch, n_lat, d)
```

```python
import functools
import math

import jax
import jax.numpy as jnp
from jax import lax
from jax.experimental import pallas as pl
from jax.experimental.pallas import tpu as pltpu

F32 = jnp.float32
BF16 = jnp.bfloat16
HIGHEST = lax.Precision.HIGHEST

NORM_EPS = 1e-6
ROPE_BASE = 10000.0
GRID_W = 64
N_BRANCH = 4
LANES = 128
SUBLANES = 8

RWKV_HEAD = 64
RWKV_DECAY_RANK = 64
RWKV_AAA_RANK = 64
RWKV_GATE_RANK = 128
RWKV_LN_EPS = 64e-5
RWKV_CHUNK = 64

MLA_HEADS = 8
MLA_NOPE = 128
MLA_ROPE = 64
MLA_Q_RANK = 768
MLA_KV_RANK = 256

SSD_HEAD = 64
SSD_GROUPS = 4
SSD_STATE = 128
SSD_CHUNK = 128

DIFF_HEADS = 8
DIFF_HEAD = 64

PREP_ROWS = 256

VMEM_LIMIT_BYTES = 56 * 1024 * 1024


def _params(*sem):
    return pltpu.CompilerParams(dimension_semantics=sem, vmem_limit_bytes=VMEM_LIMIT_BYTES)


def _dot(a, b):
    return jnp.dot(a, b, preferred_element_type=F32)


def _dot_nt(a, b):
    return lax.dot_general(a, b, (((1,), (1,)), ((), ())), preferred_element_type=F32)


def _dot_tn(a, b):
    return lax.dot_general(a, b, (((0,), (0,)), ((), ())), preferred_element_type=F32)


def _dot_hi(a, b):
    return jnp.dot(a, b, preferred_element_type=F32, precision=HIGHEST)


def _split_bf16(x, terms):
    parts = []
    for _ in range(terms):
        p = x.astype(BF16)
        parts.append(p)
        x = x - p.astype(F32)
    return parts


def _dot_split_lhs(x, e, terms=2):
    return sum(_dot(p, e) for p in _split_bf16(x, terms))


def _dot_split_rhs(e, x, terms=3):
    return sum(_dot(e, p) for p in _split_bf16(x, terms))


def _full(shape):
    nd = len(shape)
    return pl.BlockSpec(shape, lambda *_: (0,) * nd)


def _apply_act(y, act):
    if act is None:
        return y
    if act == "sigmoid":
        return jax.nn.sigmoid(y)
    if act == "relu2":
        return jnp.square(jnp.maximum(y, 0.0))
    raise ValueError(act)


def _mm_kernel(x_ref, w_ref, o_ref, wbf_ref, *acc, nk, act):
    k = pl.program_id(1)
    i = pl.program_id(2)

    @pl.when(i == 0)
    def _():
        wbf_ref[...] = w_ref[...].astype(BF16)

    y = _dot(x_ref[...].astype(BF16), wbf_ref[...])
    if nk == 1:
        o_ref[...] = _apply_act(y, act).astype(o_ref.dtype)
        return
    (acc_ref,) = acc
    tm = x_ref.shape[0]
    rows = pl.ds(pl.multiple_of(i * tm, tm), tm)

    @pl.when(k == 0)
    def _():
        acc_ref[rows, :] = y

    @pl.when(jnp.logical_and(k > 0, k < nk - 1))
    def _():
        acc_ref[rows, :] += y

    @pl.when(k == nk - 1)
    def _():
        o_ref[...] = _apply_act(acc_ref[rows, :] + y, act).astype(o_ref.dtype)


def matmul(x, w, *, layer=None, rows=None, tm=1024, tn=512, tk=4096, act=None, out_dtype=F32):
    m = x.shape[0] if rows is None else rows
    kdim = x.shape[1]
    n = w.shape[-1]
    assert w.shape[-2] == kdim
    tm = math.gcd(tm, m)
    tk = min(tk, kdim)
    tn = min(tn, n)
    assert m % tm == 0 and kdim % tk == 0
    ni, nk, nj = m // tm, kdim // tk, pl.cdiv(n, tn)
    if w.ndim == 3:
        w_spec = pl.BlockSpec((None, tk, tn), lambda j, k, i: (layer, k, j))
    else:
        w_spec = pl.BlockSpec((tk, tn), lambda j, k, i: (k, j))
    if nk == 1:
        o_map = lambda j, k, i: (i, j)
    else:
        o_map = lambda j, k, i: (jnp.where(k == nk - 1, i, 0), j)
    scratch = [pltpu.VMEM((tk, tn), BF16)]
    if nk > 1:
        scratch.append(pltpu.VMEM((m, tn), F32))
    return pl.pallas_call(
        functools.partial(_mm_kernel, nk=nk, act=act),
        grid=(nj, nk, ni),
        in_specs=[pl.BlockSpec((tm, tk), lambda j, k, i: (i, k)), w_spec],
        out_specs=pl.BlockSpec((tm, tn), o_map),
        out_shape=jax.ShapeDtypeStruct((m, n), out_dtype),
        scratch_shapes=scratch,
        compiler_params=_params("arbitrary", "arbitrary", "arbitrary"),
        name="matmul_ws",
    )(x, w)


def _merge_kernel(g0, g1, g2, g3, y0, y1, y2, y3, w_ref, o_ref, wbf_ref):
    @pl.when(pl.program_id(1) == 0)
    def _():
        wbf_ref[...] = w_ref[...].astype(BF16)

    acc = None
    for b, (g, y) in enumerate(((g0, y0), (g1, y1), (g2, y2), (g3, y3))):
        t = g[...].astype(F32) * _dot(y[...].astype(BF16), wbf_ref[b])
        acc = t if acc is None else acc + t
    o_ref[...] = acc.astype(o_ref.dtype)


def merge_branches(gates, ys, w_branch, layer, *, rows, tm=1024, tn=512):
    d = w_branch.shape[-1]
    bw = w_branch.shape[-2]
    tm = math.gcd(tm, rows)
    assert rows % tm == 0 and d % tn == 0
    nj = d // tn
    g_specs = [pl.BlockSpec((tm, tn), functools.partial(lambda j, i, b: (i, b * nj + j), b=b))
               for b in range(N_BRANCH)]
    y_specs = [pl.BlockSpec((tm, bw), lambda j, i: (i, 0)) for _ in range(N_BRANCH)]
    return pl.pallas_call(
        _merge_kernel,
        grid=(nj, rows // tm),
        in_specs=g_specs + y_specs + [pl.BlockSpec((None, N_BRANCH, bw, tn), lambda j, i: (layer, 0, 0, j))],
        out_specs=pl.BlockSpec((tm, tn), lambda j, i: (i, j)),
        out_shape=jax.ShapeDtypeStruct((rows, d), BF16),
        scratch_shapes=[pltpu.VMEM((N_BRANCH, bw, tn), BF16)],
        compiler_params=_params("arbitrary", "arbitrary"),
        name="merge_branches",
    )(gates, gates, gates, gates, *ys, w_branch)


def _rms(x, eps=NORM_EPS):
    return x * lax.rsqrt(jnp.mean(x * x, axis=-1, keepdims=True) + eps)


def _norm_mod_kernel(x_ref, g_ref, shift_ref, scale_ref, o_ref):
    x = x_ref[...].astype(F32)
    y = _rms(x) * g_ref[...]
    o_ref[...] = (y * (1.0 + scale_ref[...]) + shift_ref[...]).astype(o_ref.dtype)


def _group_of_rows(tm, rows_per_group, n_groups):
    return lambda i: jnp.minimum((i * tm) // rows_per_group, n_groups - 1)


def norm_modulate(x, g, mod, shift_idx, scale_idx, *, rows, rows_per_group, tm=PREP_ROWS):
    d = x.shape[1]
    tm = min(tm, rows_per_group)
    assert rows % tm == 0 and rows_per_group % tm == 0
    grp = _group_of_rows(tm, rows_per_group, mod.shape[0])
    return pl.pallas_call(
        _norm_mod_kernel,
        grid=(rows // tm,),
        in_specs=[pl.BlockSpec((tm, d), lambda i: (i, 0)),
                  pl.BlockSpec((1, d), lambda i: (0, 0)),
                  pl.BlockSpec((None, 1, d), lambda i: (grp(i), 0, shift_idx)),
                  pl.BlockSpec((None, 1, d), lambda i: (grp(i), 0, scale_idx))],
        out_specs=pl.BlockSpec((tm, d), lambda i: (i, 0)),
        out_shape=jax.ShapeDtypeStruct((rows, d), BF16),
        compiler_params=_params("arbitrary"),
        name="norm_modulate",
    )(x, g.reshape(1, d), mod, mod)


def _norm_res_kernel(x_ref, y_ref, g_ref, m_ref, o_ref):
    y = _rms(y_ref[...].astype(F32)) * g_ref[...]
    o_ref[...] = x_ref[...] + m_ref[...] * y


def norm_residual(x, y, g, mod, gate_idx, *, rows, rows_per_group, tm=PREP_ROWS):
    d = x.shape[1]
    tm = min(tm, rows_per_group)
    assert rows % tm == 0 and rows_per_group % tm == 0
    grp = _group_of_rows(tm, rows_per_group, mod.shape[0])
    return pl.pallas_call(
        _norm_res_kernel,
        grid=(rows // tm,),
        in_specs=[pl.BlockSpec((tm, d), lambda i: (i, 0)),
                  pl.BlockSpec((tm, d), lambda i: (i, 0)),
                  pl.BlockSpec((1, d), lambda i: (0, 0)),
                  pl.BlockSpec((None, 1, d), lambda i: (grp(i), 0, gate_idx))],
        out_specs=pl.BlockSpec((tm, d), lambda i: (i, 0)),
        out_shape=jax.ShapeDtypeStruct((rows, d), F32),
        compiler_params=_params("arbitrary"),
        name="norm_residual",
    )(x, y, g.reshape(1, d), mod)


class _Rows:
    def __init__(self, batch, n_lat, n_ctx):
        self.batch, self.n_lat, self.n_ctx = batch, n_lat, n_ctx
        self.m_lat = batch * n_lat
        self.m = batch * (n_lat + n_ctx)

    def split(self, t):
        c = t.shape[-1]
        return (t[:self.m_lat].reshape(self.batch, self.n_lat, c),
                t[self.m_lat:].reshape(self.batch, self.n_ctx, c))

    def join(self, lat, ctx):
        c = lat.shape[-1]
        return jnp.concatenate([lat.reshape(self.m_lat, c), ctx.reshape(self.batch * self.n_ctx, c)], axis=0)

    def seq_edges(self, i, tm):
        r0 = i * tm
        in_lat = r0 < self.m_lat
        pos = jnp.where(in_lat, r0 % self.n_lat, (r0 - self.m_lat) % self.n_ctx)
        length = jnp.where(in_lat, self.n_lat, self.n_ctx)
        return pos == 0, pos + tm == length

    def chunk_block(self, chunk):
        ncc, nlc = self.n_ctx // chunk, self.n_lat // chunk
        ctx0 = self.m_lat // chunk

        def block(d, b, c):
            in_ctx = c < ncc
            cc = jnp.where(in_ctx, c, c - ncc)
            n_seg = jnp.where(in_ctx, ncc, nlc)
            cc = jnp.where(d == 1, n_seg - 1 - cc, cc)
            return jnp.where(in_ctx, ctx0 + b * ncc, b * nlc) + cc
        return block, ncc + nlc


def _conv3_tile(x, prev_row, next_row, w_ref, first, last):
    tm = x.shape[0]
    rid = lax.broadcasted_iota(jnp.int32, x.shape, 0)
    prev_row = jnp.where(first, 0.0, prev_row)
    next_row = jnp.where(last, 0.0, next_row)
    prev = jnp.where(rid == 0, prev_row, pltpu.roll(x, 1, 0))
    nxt = jnp.where(rid == tm - 1, next_row, pltpu.roll(x, tm - 1, 0))
    return prev * w_ref[0:1, :] + x * w_ref[1:2, :] + nxt * w_ref[2:3, :]


def _halo_specs(tm, width, col_block, n_rows):
    per = tm // SUBLANES
    last = n_rows // SUBLANES - 1
    return [pl.BlockSpec((SUBLANES, width), lambda i: (jnp.maximum(i * per - 1, 0), col_block)),
            pl.BlockSpec((SUBLANES, width), lambda i: (jnp.minimum((i + 1) * per, last), col_block))]


def _chunk_cumsum_matrix(tm, chunk, reverse):
    r = lax.broadcasted_iota(jnp.int32, (tm, tm), 0)
    c = lax.broadcasted_iota(jnp.int32, (tm, tm), 1)
    same = (r // chunk) == (c // chunk)
    tri = (r <= c) if reverse else (r >= c)
    return jnp.logical_and(same, tri).astype(BF16)


def _rwkv_prep_kernel(rkv_ref, hp_ref, hn_ref, low_ref, shift_ref, w0_ref, a0_ref, wup_ref, aup_ref, gup_ref,
                      kk_ref, ka_ref, rk_ref, e_ref,
                      at_ref, rt_ref, bt_ref, kt_ref, bc_ref, kc_ref, gend_ref, v_ref, gate_ref, bonus_ref,
                      *, rows, chunk):
    tm = rkv_ref.shape[0]
    bw = v_ref.shape[1]
    first, last = rows.seq_edges(pl.program_id(0), tm)
    rkv = _conv3_tile(rkv_ref[...], hp_ref[SUBLANES - 1:SUBLANES, :], hn_ref[0:1, :], shift_ref, first, last)
    r, k, v = rkv[:, :bw], rkv[:, bw:2 * bw], rkv[:, 2 * bw:]
    e = e_ref[...]
    kk = k * kk_ref[...]
    kk = kk * lax.rsqrt(_dot_split_lhs(kk * kk, e) + 1e-12)
    low = low_ref[...]
    wd = low[:, :RWKV_DECAY_RANK]
    ad = low[:, RWKV_DECAY_RANK:RWKV_DECAY_RANK + RWKV_AAA_RANK]
    gd = low[:, RWKV_DECAY_RANK + RWKV_AAA_RANK:]
    w_pre = _dot_hi(jnp.tanh(wd), wup_ref[...])
    a_pre = _dot_hi(ad, aup_ref[...])
    gate_ref[...] = _dot_hi(jax.nn.sigmoid(gd), gup_ref[...])
    v_ref[...] = v.astype(BF16)
    kt_sum = None
    for d in range(2):
        w_log = -jax.nn.softplus(-(w0_ref[d:d + 1, :] + w_pre[:, d * bw:(d + 1) * bw])) - 0.5
        lw = -jnp.exp(w_log)
        a = jax.nn.sigmoid(a0_ref[d:d + 1, :] + a_pre[:, d * bw:(d + 1) * bw])
        kt = k * (1.0 + (a - 1.0) * ka_ref[...])
        kt_sum = kt if kt_sum is None else kt_sum + kt
        b = kk * a
        lam = _dot_split_rhs(_chunk_cumsum_matrix(tm, chunk, d == 1), lw)
        g_inv = jnp.exp(-lam)
        at_ref[d] = (-kk * jnp.exp(lam - lw)).astype(BF16)
        rt_ref[d] = (r * jnp.exp(lam)).astype(BF16)
        bt_ref[d] = (b * g_inv).astype(BF16)
        kt_ref[d] = (kt * g_inv).astype(BF16)
        for q in range(tm // chunk):
            end = q * chunk if d == 1 else (q + 1) * chunk - 1
            lam_end = lam[end:end + 1, :]
            sl = slice(q * chunk, (q + 1) * chunk)
            to_end = jnp.exp(lam_end - lam[sl])
            bc_ref[d, sl, :] = (b[sl] * to_end).astype(BF16)
            kc_ref[d, sl, :] = (kt[sl] * to_end).astype(BF16)
            gend_ref[d, q] = jnp.exp(lam_end)
    bonus_ref[...] = _dot_split_lhs(r * kt_sum * rk_ref[...], e) * v


def rwkv_prep(pa, rows, layer, shift, w0, w_up, a0, a_up, g_up, k_k, k_a, r_k, seg_ones, *, tm=PREP_ROWS,
              chunk=RWKV_CHUNK):
    m = pa.shape[0]
    bw = k_k.shape[-1]
    low_w = RWKV_DECAY_RANK + RWKV_AAA_RANK + RWKV_GATE_RANK
    tm = min(tm, rows.n_ctx)
    assert m % tm == 0 and rows.n_ctx % tm == 0 and rows.n_lat % tm == 0 and tm % chunk == 0
    assert (3 * bw) % low_w == 0
    wup = jnp.concatenate([w_up[layer, 0], w_up[layer, 1]], axis=1)
    aup = jnp.concatenate([a_up[layer, 0], a_up[layer, 1]], axis=1)
    row = lambda t: t.reshape(1, bw)
    big = jax.ShapeDtypeStruct((2, m, bw), BF16)
    big_spec = pl.BlockSpec((2, tm, bw), lambda i: (0, i, 0))
    one_spec = pl.BlockSpec((tm, bw), lambda i: (i, 0))
    cpt = tm // chunk
    return pl.pallas_call(
        functools.partial(_rwkv_prep_kernel, rows=rows, chunk=chunk),
        grid=(m // tm,),
        in_specs=[pl.BlockSpec((tm, 3 * bw), lambda i: (i, 0))] + _halo_specs(tm, 3 * bw, 0, m)
        + [pl.BlockSpec((tm, low_w), lambda i: (i, 3 * bw // low_w)),
           _full((3, 3 * bw)), _full((2, bw)), _full((2, bw)), _full(wup.shape), _full(aup.shape),
           pl.BlockSpec((None,) + g_up.shape[1:], lambda i: (layer, 0, 0)),
           _full((1, bw)), _full((1, bw)), _full((1, bw)), _full(seg_ones.shape)],
        out_specs=[big_spec] * 6 + [pl.BlockSpec((2, cpt, 1, bw), lambda i: (0, i, 0, 0)),
                                    one_spec, one_spec, one_spec],
        out_shape=[big] * 6 + [jax.ShapeDtypeStruct((2, m // chunk, 1, bw), F32),
                               jax.ShapeDtypeStruct((m, bw), BF16), jax.ShapeDtypeStruct((m, bw), F32),
                               jax.ShapeDtypeStruct((m, bw), F32)],
        compiler_params=_params("arbitrary"),
        name="rwkv_prep",
    )(pa, pa, pa, pa, shift[layer], w0[layer], a0[layer], wup, aup, g_up, row(k_k[layer]), row(k_a[layer]),
      row(r_k[layer]), seg_ones)


def _pair_select(lo, z):
    half = z.shape[0] // 2
    return jnp.where(lo, z[:half], z[half:])


def _rwkv_scan_kernel(at_ref, rt_ref, bt_ref, kt_ref, bc_ref, kc_ref, gend_ref, v_ref, y_ref, s_ref, *, chunk):
    @pl.when(pl.program_id(2) == 0)
    def _():
        s_ref[...] = jnp.zeros_like(s_ref)

    c2 = 2 * chunk
    n_pairs = s_ref.shape[0]
    pw = 2 * RWKV_HEAD
    n_levels = chunk.bit_length() - 1
    order = lax.broadcasted_iota(jnp.int32, (chunk, chunk), 0) - lax.broadcasted_iota(jnp.int32, (chunk, chunk), 1)
    order = jnp.where(pl.program_id(0) == 1, -order, order)
    strict = order > 0
    incl = order >= 0
    lo = lax.broadcasted_iota(jnp.int32, (chunk, pw), 1) < RWKV_HEAD
    lo2 = lax.broadcasted_iota(jnp.int32, (c2, pw), 1) < RWKV_HEAD
    own = ((lax.broadcasted_iota(jnp.int32, (pw, pw), 0) < RWKV_HEAD)
           == (lax.broadcasted_iota(jnp.int32, (pw, pw), 1) < RWKV_HEAD))
    pairs = range(n_pairs)
    sl = [slice(j * pw, (j + 1) * pw) for j in pairs]

    def tri(mask, z):
        return jnp.where(mask, z, 0.0)

    a = [at_ref[:, s] for s in sl]
    r = [rt_ref[:, s] for s in sl]
    v = [v_ref[:, s] for s in sl]
    s0 = [s_ref[j] for j in pairs]
    ar = [jnp.concatenate([a[j], r[j]], axis=0) for j in pairs]
    ar_st = [jnp.concatenate([jnp.where(lo2, ar[j], 0), jnp.where(lo2, 0, ar[j])], axis=0) for j in pairs]
    p_b = [_dot_nt(ar_st[j], bt_ref[:, sl[j]]) for j in pairs]
    p_k = [_dot_nt(ar_st[j], kt_ref[:, sl[j]]) for j in pairs]
    lp = [[tri(strict, p_b[j][e * c2:e * c2 + chunk]) for e in range(2)] for j in pairs]
    l_ak = [jnp.concatenate([tri(strict, p_k[j][e * c2:e * c2 + chunk]) for e in range(2)], axis=0).astype(BF16)
            for j in pairs]
    m_rb = [jnp.concatenate([tri(incl, p_b[j][e * c2 + chunk:(e + 1) * c2]) for e in range(2)], axis=0).astype(BF16)
            for j in pairs]
    m_rk = [jnp.concatenate([tri(incl, p_k[j][e * c2 + chunk:(e + 1) * c2]) for e in range(2)], axis=0).astype(BF16)
            for j in pairs]
    x_w = [a[j].astype(F32) for j in pairs]
    x_u = [_pair_select(lo, _dot(l_ak[j], v[j])) for j in pairs]
    for lvl in range(n_levels):
        lpb = [[lp[j][e].astype(BF16) for e in range(2)] for j in pairs]
        lst = [jnp.concatenate(lpb[j], axis=0) for j in pairs]
        z = [_dot(lst[j], jnp.concatenate([x_w[j], x_u[j]], axis=1).astype(BF16)) for j in pairs]
        x_w = [x_w[j] + _pair_select(lo, z[j][:, :pw]) for j in pairs]
        x_u = [x_u[j] + _pair_select(lo, z[j][:, pw:]) for j in pairs]
        if lvl + 1 < n_levels:
            lp = [[_dot(lpb[j][e], lpb[j][e]) for e in range(2)] for j in pairs]
    s0b = [s0[j].astype(BF16) for j in pairs]
    u = [_dot_nt(x_w[j].astype(BF16), s0b[j]) + x_u[j] for j in pairs]
    ub = [u[j].astype(BF16) for j in pairs]
    y = [_dot_nt(r[j], s0b[j]) + _pair_select(lo, _dot(m_rb[j], ub[j])) + _pair_select(lo, _dot(m_rk[j], v[j]))
         for j in pairs]
    s_new = [s0[j] * gend_ref[:, sl[j]]
             + jnp.where(own, _dot_tn(ub[j], bc_ref[:, sl[j]]) + _dot_tn(v[j], kc_ref[:, sl[j]]), 0.0)
             for j in pairs]
    for j in pairs:
        y_ref[:, sl[j]] = y[j]
    for j in pairs:
        s_ref[j] = s_new[j]


def rwkv_scan(ops, v, rows, *, chunk=RWKV_CHUNK):
    at, rt, bt, kt, bc, kc, gend = ops
    _, m, bw = at.shape
    block, nc = rows.chunk_block(chunk)
    big = pl.BlockSpec((None, chunk, bw), lambda d, b, c: (d, block(d, b, c), 0))
    return pl.pallas_call(
        functools.partial(_rwkv_scan_kernel, chunk=chunk),
        grid=(2, rows.batch, nc),
        in_specs=[big] * 6 + [pl.BlockSpec((None, None, 1, bw), lambda d, b, c: (d, block(d, b, c), 0, 0)),
                              pl.BlockSpec((chunk, bw), lambda d, b, c: (block(d, b, c), 0))],
        out_specs=big,
        out_shape=jax.ShapeDtypeStruct((2, m, bw), F32),
        scratch_shapes=[pltpu.VMEM((bw // (2 * RWKV_HEAD), 2 * RWKV_HEAD, 2 * RWKV_HEAD), F32)],
        compiler_params=_params("arbitrary", "arbitrary", "arbitrary"),
        name="rwkv_scan",
    )(at, rt, bt, kt, bc, kc, gend, v)


def _rwkv_readout_kernel(y_ref, bonus_ref, gate_ref, e_ref, g_ref, b_ref, o_ref):
    y = y_ref[0] + y_ref[1]
    e = e_ref[...]
    mu = _dot_split_lhs(y, e) * (1.0 / RWKV_HEAD)
    yc = y - mu
    var = _dot_split_lhs(yc * yc, e) * (1.0 / RWKV_HEAD)
    yn = yc * lax.rsqrt(var + RWKV_LN_EPS) * g_ref[...] + b_ref[...]
    o_ref[...] = ((yn + bonus_ref[...]) * gate_ref[...]).astype(o_ref.dtype)


def rwkv_readout(y, bonus, gate, seg_ones, ln_g, ln_b, *, rows, tm=PREP_ROWS):
    bw = y.shape[-1]
    tm = math.gcd(tm, rows)
    one = pl.BlockSpec((tm, bw), lambda i: (i, 0))
    return pl.pallas_call(
        _rwkv_readout_kernel,
        grid=(rows // tm,),
        in_specs=[pl.BlockSpec((2, tm, bw), lambda i: (0, i, 0)), one, one, _full(seg_ones.shape),
                  _full((1, bw)), _full((1, bw))],
        out_specs=one,
        out_shape=jax.ShapeDtypeStruct((rows, bw), BF16),
        compiler_params=_params("arbitrary"),
        name="rwkv_readout",
    )(y, bonus, gate, seg_ones, ln_g.reshape(1, bw), ln_b.reshape(1, bw))


def _ssd_prep_kernel(x_ref, hp_ref, hn_ref, w_ref, b_ref, o_ref, *, rows):
    first, last = rows.seq_edges(pl.program_id(0), x_ref.shape[0])
    y = _conv3_tile(x_ref[...], hp_ref[SUBLANES - 1:SUBLANES, :], hn_ref[0:1, :], w_ref, first, last) + b_ref[...]
    o_ref[...] = y * jax.nn.sigmoid(y)


def ssd_prep(pc, rows, conv_w, conv_b, *, width, first_block, n_blocks, tm=PREP_ROWS):
    m = pc.shape[0]
    tm = min(tm, rows.n_ctx)
    assert m % tm == 0 and rows.n_ctx % tm == 0 and rows.n_lat % tm == 0
    per = tm // SUBLANES
    last = m // SUBLANES - 1
    return pl.pallas_call(
        functools.partial(_ssd_prep_kernel, rows=rows),
        grid=(m // tm, n_blocks),
        in_specs=[pl.BlockSpec((tm, width), lambda i, j: (i, first_block + j)),
                  pl.BlockSpec((SUBLANES, width), lambda i, j: (jnp.maximum(i * per - 1, 0), first_block + j)),
                  pl.BlockSpec((SUBLANES, width), lambda i, j: (jnp.minimum((i + 1) * per, last), first_block + j)),
                  pl.BlockSpec((3, width), lambda i, j: (0, j)),
                  pl.BlockSpec((1, width), lambda i, j: (0, j))],
        out_specs=pl.BlockSpec((tm, width), lambda i, j: (i, j)),
        out_shape=jax.ShapeDtypeStruct((m, n_blocks * width), F32),
        compiler_params=_params("arbitrary", "arbitrary"),
        name="ssd_prep",
    )(pc, pc, pc, conv_w, conv_b.reshape(1, n_blocks * width))


def _ssd_scan_kernel(x_ref, b_ref, c_ref, la_ref, lat_ref, dt_ref, y_ref, s_ref, *, chunk, heads, groups):
    @pl.when(pl.program_id(2) == 0)
    def _():
        s_ref[...] = jnp.zeros_like(s_ref)

    rev = pl.program_id(0) == 1
    pw = 2 * SSD_HEAD
    order = lax.broadcasted_iota(jnp.int32, (chunk, chunk), 0) - lax.broadcasted_iota(jnp.int32, (chunk, chunk), 1)
    order = jnp.where(rev, -order, order)
    upto = order >= 0
    lo = lax.broadcasted_iota(jnp.int32, (chunk, pw), 1) < SSD_HEAD
    acs_c_all = _dot_hi(upto.astype(F32), la_ref[...])
    acs_r_all = _dot_hi(lat_ref[...], (order <= 0).astype(F32))
    tot_c_all = jnp.where(rev, acs_c_all[0:1, :], acs_c_all[chunk - 1:chunk, :])
    dt_all = dt_ref[...]
    hpg = heads // groups
    for gi in range(groups):
        cg = c_ref[:, gi * SSD_STATE:(gi + 1) * SSD_STATE].astype(BF16)
        bg = b_ref[:, gi * SSD_STATE:(gi + 1) * SSD_STATE].astype(BF16)
        cb = _dot_nt(cg, bg)
        for qi in range(hpg // 2):
            h0 = gi * hpg + 2 * qi
            j = h0 // 2
            cols = slice(j * pw, (j + 1) * pw)
            w_st, acs_c, tot = [], [], []
            for e in range(2):
                h = h0 + e
                acs_c.append(acs_c_all[:, h:h + 1])
                tot.append(tot_c_all[:, h:h + 1])
                decay = jnp.exp(jnp.where(upto, acs_c[e] - acs_r_all[h:h + 1, :], -1e30))
                w_st.append((cb * decay).astype(BF16))
            xdt = x_ref[:, cols] * jnp.where(lo, dt_all[:, h0:h0 + 1], dt_all[:, h0 + 1:h0 + 2])
            st = s_ref[j]
            y = _pair_select(lo, _dot(jnp.concatenate(w_st, axis=0), xdt.astype(BF16)))
            y = y + _dot(cg, st.astype(BF16)) * jnp.where(lo, jnp.exp(acs_c[0]), jnp.exp(acs_c[1]))
            y_ref[:, cols] = y
            to_end = jnp.where(lo, jnp.exp(tot[0] - acs_c[0]), jnp.exp(tot[1] - acs_c[1]))
            s_ref[j] = (st * jnp.where(lo[0:1, :], jnp.exp(tot[0]), jnp.exp(tot[1]))
                        + _dot_tn(bg, (xdt * to_end).astype(BF16)))


def ssd_scan(xbc, la, dt, rows, *, heads, chunk=SSD_CHUNK):
    m = xbc.shape[0]
    bw = heads * SSD_HEAD
    gn = SSD_GROUPS * SSD_STATE
    assert bw % gn == 0
    block, nc = rows.chunk_block(chunk)
    lat = jnp.swapaxes(la, 1, 2)
    thin = pl.BlockSpec((None, chunk, heads), lambda d, b, c: (d, block(d, b, c), 0))
    return pl.pallas_call(
        functools.partial(_ssd_scan_kernel, chunk=chunk, heads=heads, groups=SSD_GROUPS),
        grid=(2, rows.batch, nc),
        in_specs=[pl.BlockSpec((chunk, bw), lambda d, b, c: (block(d, b, c), 0)),
                  pl.BlockSpec((chunk, gn), lambda d, b, c: (block(d, b, c), bw // gn)),
                  pl.BlockSpec((chunk, gn), lambda d, b, c: (block(d, b, c), bw // gn + 1)),
                  thin,
                  pl.BlockSpec((None, heads, chunk), lambda d, b, c: (d, 0, block(d, b, c))),
                  thin],
        out_specs=pl.BlockSpec((None, chunk, bw), lambda d, b, c: (d, block(d, b, c), 0)),
        out_shape=jax.ShapeDtypeStruct((2, m, bw), F32),
        scratch_shapes=[pltpu.VMEM((heads // 2, SSD_STATE, 2 * SSD_HEAD), F32)],
        compiler_params=_params("arbitrary", "arbitrary", "arbitrary"),
        name="ssd_scan",
    )(xbc, xbc, xbc, la, lat, dt)


def _ssd_readout_kernel(y_ref, x_ref, z_ref, d_ref, g_ref, o_ref):
    y = y_ref[0] + y_ref[1] + d_ref[...] * x_ref[...]
    z = z_ref[...]
    o_ref[...] = (_rms(y * (z * jax.nn.sigmoid(z))) * g_ref[...]).astype(o_ref.dtype)


def ssd_readout(y, xbc, pc, d_skip, norm_g, *, rows, tm=PREP_ROWS):
    bw = y.shape[-1]
    tm = math.gcd(tm, rows)
    one = pl.BlockSpec((tm, bw), lambda i: (i, 0))
    return pl.pallas_call(
        _ssd_readout_kernel,
        grid=(rows // tm,),
        in_specs=[pl.BlockSpec((2, tm, bw), lambda i: (0, i, 0)), one, one, _full((1, bw)), _full((1, bw))],
        out_specs=one,
        out_shape=jax.ShapeDtypeStruct((rows, bw), BF16),
        compiler_params=_params("arbitrary"),
        name="ssd_readout",
    )(y, xbc, pc, jnp.repeat(d_skip, SSD_HEAD).reshape(1, bw), norm_g.reshape(1, bw))


def _softmax_pv(scores, values):
    m = None
    for s in scores:
        ms = jnp.max(s, axis=-1, keepdims=True)
        m = ms if m is None else jnp.maximum(m, ms)
    l, o = None, None
    for s, v in zip(scores, values):
        p = jnp.exp(s - m)
        ls = jnp.sum(p, axis=-1, keepdims=True)
        os_ = _dot(p.astype(BF16), v)
        l = ls if l is None else l + ls
        o = os_ if o is None else o + os_
    return o / l


def _attn_kernel(lam_ref, q_ref, g_ref, *refs, n_seg, scale, diff, out_scale):
    seg_refs = refs[:2 * n_seg]
    o_ref = refs[2 * n_seg]
    q = q_ref[...].astype(F32) * scale
    values = [seg_refs[2 * si + 1][...] for si in range(n_seg)]
    if not diff:
        qb = q.astype(BF16)
        scores = [_dot_nt(qb, seg_refs[2 * si][...]) for si in range(n_seg)]
        o_ref[...] = _softmax_pv(scores, values).astype(o_ref.dtype)
        return
    first = lax.broadcasted_iota(jnp.int32, q.shape, 1) < DIFF_HEAD
    q1 = jnp.where(first, q, 0.0).astype(BF16)
    q2 = jnp.where(first, 0.0, q).astype(BF16)
    o1 = _softmax_pv([_dot_nt(q1, seg_refs[2 * si][...]) for si in range(n_seg)], values)
    o2 = _softmax_pv([_dot_nt(q2, seg_refs[2 * si][...]) for si in range(n_seg)], values)
    o = o1 - lam_ref[0] * o2
    o_ref[...] = (_rms(o) * g_ref[...] * out_scale).astype(o_ref.dtype)


def attention(q, k, v, *, heads, batch, q_rows, q_first, segments, scale, tq=256,
              diff_lam=None, diff_gain=None, out_scale=1.0):
    dqk = q.shape[1] // heads
    dv = v.shape[1] // heads
    tq = min(tq, q_rows)
    nq = q_rows // tq
    diff = diff_lam is not None
    lam = (diff_lam if diff else jnp.zeros((), F32)).reshape(1).astype(F32)
    gain = (diff_gain if diff else jnp.ones((dv,), F32)).reshape(1, dv).astype(F32)
    seg_specs = []
    for seg_rows, first in segments:
        for w in (dqk, dv):
            seg_specs.append(pl.BlockSpec((seg_rows, w), functools.partial(lambda b, h, i, f: (f + b, h), f=first)))
    return pl.pallas_call(
        functools.partial(_attn_kernel, n_seg=len(segments), scale=scale, diff=diff, out_scale=out_scale),
        grid=(batch, heads, nq),
        in_specs=[pl.BlockSpec(memory_space=pltpu.SMEM),
                  pl.BlockSpec((tq, dqk), lambda b, h, i: ((q_first + b) * nq + i, h)),
                  pl.BlockSpec((1, dv), lambda b, h, i: (0, 0))] + seg_specs,
        out_specs=pl.BlockSpec((tq, dv), lambda b, h, i: (b * nq + i, h)),
        out_shape=jax.ShapeDtypeStruct((batch * q_rows, heads * dv), BF16),
        compiler_params=_params("arbitrary", "arbitrary", "arbitrary"),
        name="diff_attention" if diff else "mla_attention",
    )(lam, q, gain, *([k, v] * len(segments)))


ROPE_GROUP = 64
ROPE_QUARTER = ROPE_GROUP // 4


def _axial_rope_tables(n_tok):
    t = jnp.arange(n_tok)
    row = (t // GRID_W).astype(F32)
    col = (t % GRID_W).astype(F32)
    axis_dim = ROPE_GROUP // 2
    inv = 1.0 / (ROPE_BASE ** (jnp.arange(0, axis_dim, 2, dtype=F32) / axis_dim))
    ar = row[:, None] * inv[None]
    ac = col[:, None] * inv[None]
    ang = jnp.concatenate([ar, ar, ac, ac], axis=-1)
    sign = jnp.where((jnp.arange(ROPE_GROUP) % (2 * ROPE_QUARTER)) < ROPE_QUARTER, -1.0, 1.0)
    return jnp.cos(ang), jnp.sin(ang) * sign


def _rope_lanes(x, cos, sin_signed):
    lane = lax.broadcasted_iota(jnp.int32, x.shape, 1)
    takes_next = (lane % (2 * ROPE_QUARTER)) < ROPE_QUARTER
    rot = jnp.where(takes_next, pltpu.roll(x, LANES - ROPE_QUARTER, 1), pltpu.roll(x, ROPE_QUARTER, 1))
    return x * cos + rot * sin_signed


def _rope_table_spec(rows, tm):
    per_seq = rows.n_lat // tm
    return pl.BlockSpec((tm, LANES), lambda i: (jnp.where(i * tm < rows.m_lat, i % per_seq, 0), 0))


def _tile_tables(i, tm, rows, cos_ref, sin_ref):
    is_lat = i * tm < rows.m_lat
    return jnp.where(is_lat, cos_ref[...], 1.0), jnp.where(is_lat, sin_ref[...], 0.0)


def _diff_prep_kernel(q_ref, k_ref, v_ref, cos_ref, sin_ref, qo_ref, ko_ref, vo_ref, *, rows):
    tm = q_ref.shape[0]
    cos, sin = _tile_tables(pl.program_id(0), tm, rows, cos_ref, sin_ref)
    for src, dst in ((q_ref, qo_ref), (k_ref, ko_ref)):
        for cb in range(src.shape[1] // LANES):
            cols = slice(cb * LANES, (cb + 1) * LANES)
            dst[:, cols] = _rope_lanes(src[:, cols], cos, sin).astype(dst.dtype)
    vo_ref[...] = v_ref[...].astype(vo_ref.dtype)


def diff_prep(pd, rows, cos2, sin2, *, tm=PREP_ROWS):
    m = pd.shape[0]
    bw = pd.shape[1] // 3
    tm = min(tm, rows.n_ctx)
    assert m % tm == 0 and rows.n_ctx % tm == 0 and rows.n_lat % tm == 0
    out = jax.ShapeDtypeStruct((m, bw), BF16)
    one = pl.BlockSpec((tm, bw), lambda i: (i, 0))
    tab = _rope_table_spec(rows, tm)
    return pl.pallas_call(
        functools.partial(_diff_prep_kernel, rows=rows),
        grid=(m // tm,),
        in_specs=[pl.BlockSpec((tm, bw), functools.partial(lambda i, j: (i, j), j=j)) for j in range(3)] + [tab, tab],
        out_specs=[one] * 3,
        out_shape=[out] * 3,
        compiler_params=_params("arbitrary"),
        name="diff_prep",
    )(pd, pd, pd, cos2, sin2)


def _mla_q_kernel(cq_ref, g_ref, w_ref, cos_ref, sin_ref, o_ref, *, rows):
    tm = cq_ref.shape[0]
    cos, sin = _tile_tables(pl.program_id(0), tm, rows, cos_ref, sin_ref)
    q = _dot((_rms(cq_ref[...]) * g_ref[...]).astype(BF16), w_ref[...])
    for h in range(MLA_HEADS):
        nope = slice(2 * h * LANES, (2 * h + 1) * LANES)
        rope = slice((2 * h + 1) * LANES, (2 * h + 2) * LANES)
        o_ref[:, nope] = q[:, nope].astype(o_ref.dtype)
        o_ref[:, rope] = _rope_lanes(q[:, rope], cos, sin).astype(o_ref.dtype)


def mla_q_prep(pb, rows, q_norm, w_uq, cos2, sin2, *, tm=PREP_ROWS):
    m = pb.shape[0]
    tm = min(tm, rows.n_ctx)
    w = w_uq.reshape(MLA_Q_RANK, MLA_HEADS, MLA_NOPE + MLA_ROPE)
    w = jnp.pad(w, ((0, 0), (0, 0), (0, 2 * LANES - MLA_NOPE - MLA_ROPE))).reshape(MLA_Q_RANK, -1).astype(BF16)
    tab = _rope_table_spec(rows, tm)
    return pl.pallas_call(
        functools.partial(_mla_q_kernel, rows=rows),
        grid=(m // tm,),
        in_specs=[pl.BlockSpec((tm, MLA_Q_RANK), lambda i: (i, 0)), _full((1, MLA_Q_RANK)), _full(w.shape), tab, tab],
        out_specs=pl.BlockSpec((tm, w.shape[1]), lambda i: (i, 0)),
        out_shape=jax.ShapeDtypeStruct((m, w.shape[1]), BF16),
        compiler_params=_params("arbitrary"),
        name="mla_q_prep",
    )(pb, q_norm.reshape(1, MLA_Q_RANK), w, cos2, sin2)


def _mla_kv_kernel(ckv_ref, kr_ref, g_ref, w_ref, cos_ref, sin_ref, k_ref, v_ref, *, rows):
    tm = ckv_ref.shape[0]
    cos, sin = _tile_tables(pl.program_id(0), tm, rows, cos_ref, sin_ref)
    kv = _dot((_rms(ckv_ref[...]) * g_ref[...]).astype(BF16), w_ref[...])
    lane = lax.broadcasted_iota(jnp.int32, (tm, LANES), 1)
    kr = jnp.where(lane < MLA_ROPE, _rope_lanes(kr_ref[...], cos, sin), 0.0).astype(k_ref.dtype)
    nope_w = MLA_HEADS * MLA_NOPE
    for h in range(MLA_HEADS):
        k_ref[:, 2 * h * LANES:(2 * h + 1) * LANES] = kv[:, h * MLA_NOPE:(h + 1) * MLA_NOPE].astype(k_ref.dtype)
        k_ref[:, (2 * h + 1) * LANES:(2 * h + 2) * LANES] = kr
    v_ref[...] = kv[:, nope_w:].astype(v_ref.dtype)


def mla_kv_prep(pb, rows, kv_norm, w_ukv, cos2, sin2, *, tm=PREP_ROWS):
    m = pb.shape[0]
    tm = min(tm, rows.n_ctx)
    assert MLA_NOPE == LANES and MLA_Q_RANK % MLA_KV_RANK == 0 and (MLA_Q_RANK + MLA_KV_RANK) % LANES == 0
    w = w_ukv.reshape(MLA_KV_RANK, MLA_HEADS, 2 * MLA_NOPE)
    w = jnp.concatenate([w[:, :, :MLA_NOPE].reshape(MLA_KV_RANK, -1), w[:, :, MLA_NOPE:].reshape(MLA_KV_RANK, -1)],
                        axis=1).astype(BF16)
    nope_w = MLA_HEADS * MLA_NOPE
    tab = _rope_table_spec(rows, tm)
    return pl.pallas_call(
        functools.partial(_mla_kv_kernel, rows=rows),
        grid=(m // tm,),
        in_specs=[pl.BlockSpec((tm, MLA_KV_RANK), lambda i: (i, MLA_Q_RANK // MLA_KV_RANK)),
                  pl.BlockSpec((tm, LANES), lambda i: (i, (MLA_Q_RANK + MLA_KV_RANK) // LANES)),
                  _full((1, MLA_KV_RANK)), _full(w.shape), tab, tab],
        out_specs=[pl.BlockSpec((tm, 2 * nope_w), lambda i: (i, 0)), pl.BlockSpec((tm, nope_w), lambda i: (i, 0))],
        out_shape=[jax.ShapeDtypeStruct((m, 2 * nope_w), BF16), jax.ShapeDtypeStruct((m, nope_w), BF16)],
        compiler_params=_params("arbitrary"),
        name="mla_kv_prep",
    )(pb, pb, kv_norm.reshape(1, MLA_KV_RANK), w, cos2, sin2)


def kernel(x, c, ctx, c_ctx, ada_w, ada_b, norm_mix_pre, norm_mix_post, norm_ffn_pre, norm_ffn_post, w_in, rwkv_shift, rwkv_w0, rwkv_w_up, rwkv_a0, rwkv_a_up, rwkv_g_up, rwkv_k_k, rwkv_k_a, rwkv_r_k, rwkv_ln_g, rwkv_ln_b, mla_q_norm, mla_w_uq, mla_kv_norm, mla_w_ukv, ssd_conv_w, ssd_conv_b, ssd_dt_bias, ssd_a_log, ssd_d, ssd_norm, diff_lq1, diff_lk1, diff_lq2, diff_lk2, diff_subln, w_branch, w_gate, w_out, w_ff1, w_ff2):
    batch, n_lat, d = x.shape
    n_ctx = ctx.shape[1]
    depth = ada_w.shape[0]
    bw = d // N_BRANCH
    rows = _Rows(batch, n_lat, n_ctx)
    m_lat, m_all = rows.m_lat, rows.m
    ssd_heads = bw // SSD_HEAD
    gn = SSD_GROUPS * SSD_STATE
    rwkv_in = 3 * bw + RWKV_DECAY_RANK + RWKV_AAA_RANK + RWKV_GATE_RANK
    mla_in = MLA_Q_RANK + MLA_KV_RANK + MLA_ROPE
    ssd_in = 2 * bw + 2 * gn + ssd_heads
    diff_in = 3 * bw
    o_mla, o_ssd, o_diff = rwkv_in, rwkv_in + mla_in, rwkv_in + mla_in + ssd_in
    assert w_in.shape[-1] == o_diff + diff_in
    assert n_lat % SSD_CHUNK == 0 and n_ctx % SSD_CHUNK == 0

    assert MLA_ROPE == ROPE_GROUP and DIFF_HEAD == ROPE_GROUP
    cos2, sin2 = (jnp.tile(t, (1, LANES // ROPE_GROUP)) for t in _axial_rope_tables(n_lat))
    hid = jnp.arange(bw) // RWKV_HEAD
    seg_ones = (hid[:, None] == hid[None, :]).astype(BF16)

    xs = jnp.concatenate([x.reshape(m_lat, d), ctx.reshape(batch * n_ctx, d)], axis=0)
    cvec = jnp.zeros((8, d), F32).at[:batch].set(jax.nn.silu(c)).at[batch].set(jax.nn.silu(c_ctx))
    groups = batch + 1

    for l in range(depth):
        ctx_out = l < depth - 1
        m_out = m_all if ctx_out else m_lat
        mod = (matmul(cvec, ada_w, layer=l, tn=512) + ada_b[l])[:groups].reshape(groups, 1, 6 * d)

        h = norm_modulate(xs, norm_mix_pre[l], mod, 0, 1, rows=m_all, rows_per_group=n_lat)
        pa = matmul(h, w_in[l, :, :o_mla])
        pb = matmul(h, w_in[l, :, o_mla:o_ssd])
        pc = matmul(h, w_in[l, :, o_ssd:o_diff])
        pd = matmul(h, w_in[l, :, o_diff:])

        prep = rwkv_prep(pa, rows, l, rwkv_shift, rwkv_w0, rwkv_w_up, rwkv_a0, rwkv_a_up, rwkv_g_up,
                         rwkv_k_k, rwkv_k_a, rwkv_r_k.reshape(depth, bw), seg_ones)
        y_rwkv = rwkv_scan(prep[:7], prep[7], rows)
        ya = rwkv_readout(y_rwkv, prep[9], prep[8], seg_ones, rwkv_ln_g[l], rwkv_ln_b[l], rows=m_out)

        q_m = mla_q_prep(pb, rows, mla_q_norm[l], mla_w_uq[l], cos2, sin2)
        k_m, vv = mla_kv_prep(pb, rows, mla_kv_norm[l], mla_w_ukv[l], cos2, sin2)
        ctx_blk = m_lat // n_ctx
        segs_lat = [(n_lat, 0), (n_ctx, ctx_blk)]
        mla_scale = (MLA_NOPE + MLA_ROPE) ** -0.5
        yb = attention(q_m, k_m, vv, heads=MLA_HEADS, batch=batch, q_rows=n_lat, q_first=0, segments=segs_lat,
                       scale=mla_scale)
        if ctx_out:
            yb_c = attention(q_m, k_m, vv, heads=MLA_HEADS, batch=batch, q_rows=n_ctx, q_first=ctx_blk,
                             segments=[(n_ctx, ctx_blk)], scale=mla_scale)
            yb = jnp.concatenate([yb, yb_c], axis=0)

        assert (2 * gn) % bw == 0
        xbc = ssd_prep(pc, rows, ssd_conv_w[l], ssd_conv_b[l], width=bw, first_block=1, n_blocks=1 + 2 * gn // bw)
        dt_raw = pc[:, 2 * bw + 2 * gn:2 * bw + 2 * gn + ssd_heads]
        dts = jnp.stack([jax.nn.softplus(dt_raw + ssd_dt_bias[l, dr]) for dr in range(2)])
        las = dts * (-jnp.exp(ssd_a_log[l]))[:, None, :]
        y_ssd = ssd_scan(xbc, las, dts, rows, heads=ssd_heads)
        yc = ssd_readout(y_ssd, xbc, pc, ssd_d[l], ssd_norm[l], rows=m_out)

        q_d, k_d, v_d = diff_prep(pd, rows, cos2, sin2)
        lam_init = 0.8 - 0.6 * math.exp(-0.3 * l)
        lam = (jnp.exp(jnp.sum(diff_lq1[l] * diff_lk1[l])) - jnp.exp(jnp.sum(diff_lq2[l] * diff_lk2[l])) + lam_init)
        diff_kw = dict(heads=DIFF_HEADS, batch=batch, scale=DIFF_HEAD ** -0.5, diff_lam=lam,
                       diff_gain=diff_subln[l], out_scale=1.0 - lam_init)
        yd = attention(q_d, k_d, v_d, q_rows=n_lat, q_first=0, segments=segs_lat, **diff_kw)
        if ctx_out:
            yd_c = attention(q_d, k_d, v_d, q_rows=n_ctx, q_first=ctx_blk, segments=[(n_ctx, ctx_blk)], **diff_kw)
            yd = jnp.concatenate([yd, yd_c], axis=0)

        gates = matmul(h, w_gate, layer=l, rows=m_out, act="sigmoid", out_dtype=BF16)
        merged = merge_branches(gates, [ya, yb, yc, yd], w_branch, l, rows=m_out)
        mix = matmul(merged, w_out, layer=l)
        xs = norm_residual(xs, mix, norm_mix_post[l], mod, 2, rows=m_out, rows_per_group=n_lat)

        hf = norm_modulate(xs, norm_ffn_pre[l], mod, 3, 4, rows=m_out, rows_per_group=n_lat)
        f1 = matmul(hf, w_ff1, layer=l, act="relu2", out_dtype=BF16)
        f2 = matmul(f1, w_ff2, layer=l, tm=512)
        xs = norm_residual(xs, f2, norm_ffn_post[l], mod, 5, rows=m_out, rows_per_group=n_lat)

    return xs[:m_lat].reshape(batch, n_lat, d)
```

```python
import functools
import math

import jax
import jax.numpy as jnp
from jax import lax
from jax.experimental import pallas as pl
from jax.experimental.pallas import tpu as pltpu

F32 = jnp.float32
BF16 = jnp.bfloat16
HIGHEST = lax.Precision.HIGHEST

NORM_EPS = 1e-6
ROPE_BASE = 10000.0
GRID_W = 64
N_BRANCH = 4
LANES = 128
SUBLANES = 8

RWKV_HEAD = 64
RWKV_DECAY_RANK = 64
RWKV_AAA_RANK = 64
RWKV_GATE_RANK = 128
RWKV_LN_EPS = 64e-5
RWKV_CHUNK = 64

MLA_HEADS = 8
MLA_NOPE = 128
MLA_ROPE = 64
MLA_Q_RANK = 768
MLA_KV_RANK = 256

SSD_HEAD = 64
SSD_GROUPS = 4
SSD_STATE = 128
SSD_CHUNK = 128

DIFF_HEADS = 8
DIFF_HEAD = 64

PREP_ROWS = 256

VMEM_LIMIT_BYTES = 56 * 1024 * 1024


def _params(*sem):
    return pltpu.CompilerParams(dimension_semantics=sem, vmem_limit_bytes=VMEM_LIMIT_BYTES)


def _dot(a, b):
    return jnp.dot(a, b, preferred_element_type=F32)


def _dot_nt(a, b):
    return lax.dot_general(a, b, (((1,), (1,)), ((), ())), preferred_element_type=F32)


def _dot_tn(a, b):
    return lax.dot_general(a, b, (((0,), (0,)), ((), ())), preferred_element_type=F32)


def _dot_hi(a, b):
    return jnp.dot(a, b, preferred_element_type=F32, precision=HIGHEST)


def _split_bf16(x, terms):
    parts = []
    for _ in range(terms):
        p = x.astype(BF16)
        parts.append(p)
        x = x - p.astype(F32)
    return parts


def _dot_split_lhs(x, e, terms=2):
    return sum(_dot(p, e) for p in _split_bf16(x, terms))


def _dot_split_rhs(e, x, terms=3):
    return sum(_dot(e, p) for p in _split_bf16(x, terms))


def _full(shape):
    nd = len(shape)
    return pl.BlockSpec(shape, lambda *_: (0,) * nd)


def _apply_act(y, act):
    if act is None:
        return y
    if act == "sigmoid":
        return jax.nn.sigmoid(y)
    if act == "relu2":
        return jnp.square(jnp.maximum(y, 0.0))
    raise ValueError(act)


def _mm_kernel(*refs, nk, act, x_parts, row_parts):
    x_refs = refs[:x_parts]
    w_ref, o_ref, wbf_ref = refs[x_parts:x_parts + 3]
    acc = refs[x_parts + 3:]
    k = pl.program_id(1)
    i = pl.program_id(2)

    @pl.when(i == 0)
    def _():
        wbf_ref[...] = w_ref[...].astype(BF16)

    if nk == 1:
        rows_per = x_refs[0].shape[0] // row_parts
        kp = x_refs[0].shape[1]
        for r in range(row_parts):
            rs = slice(r * rows_per, (r + 1) * rows_per)
            y = None
            for p, x_ref in enumerate(x_refs):
                t = _dot(x_ref[rs, :].astype(BF16), wbf_ref[p * kp:(p + 1) * kp, :])
                y = t if y is None else y + t
            o_ref[rs, :] = _apply_act(y, act).astype(o_ref.dtype)
        return
    x_ref, = x_refs
    y = _dot(x_ref[...].astype(BF16), wbf_ref[...])
    (acc_ref,) = acc
    tm = x_ref.shape[0]
    rows = pl.ds(pl.multiple_of(i * tm, tm), tm)

    @pl.when(k == 0)
    def _():
        acc_ref[rows, :] = y

    @pl.when(jnp.logical_and(k > 0, k < nk - 1))
    def _():
        acc_ref[rows, :] += y

    @pl.when(k == nk - 1)
    def _():
        o_ref[...] = _apply_act(acc_ref[rows, :] + y, act).astype(o_ref.dtype)


def matmul(x, w, *, layer=None, rows=None, tm=1024, tn=512, tk=4096, act=None, out_dtype=F32,
           x_parts=1, row_parts=1):
    m = x.shape[0] if rows is None else rows
    kdim = x.shape[1]
    n = w.shape[-1]
    assert w.shape[-2] == kdim
    tm = math.gcd(tm, m)
    tk = min(tk, kdim)
    tn = min(tn, n)
    assert m % tm == 0 and kdim % tk == 0
    ni, nk, nj = m // tm, kdim // tk, pl.cdiv(n, tn)
    if nk > 1 or tk % (x_parts * LANES) or tm % (row_parts * 2 * SUBLANES):
        x_parts = row_parts = 1
    kp = tk // x_parts
    if w.ndim == 3:
        w_spec = pl.BlockSpec((None, tk, tn), lambda j, k, i: (layer, k, j))
    else:
        w_spec = pl.BlockSpec((tk, tn), lambda j, k, i: (k, j))
    if nk == 1:
        o_map = lambda j, k, i: (i, j)
    else:
        o_map = lambda j, k, i: (jnp.where(k == nk - 1, i, 0), j)
    scratch = [pltpu.VMEM((tk, tn), BF16)]
    if nk > 1:
        scratch.append(pltpu.VMEM((m, tn), F32))
    x_specs = [pl.BlockSpec((tm, kp), functools.partial(lambda j, k, i, p: (i, k * x_parts + p), p=p))
               for p in range(x_parts)]
    return pl.pallas_call(
        functools.partial(_mm_kernel, nk=nk, act=act, x_parts=x_parts, row_parts=row_parts),
        grid=(nj, nk, ni),
        in_specs=x_specs + [w_spec],
        out_specs=pl.BlockSpec((tm, tn), o_map),
        out_shape=jax.ShapeDtypeStruct((m, n), out_dtype),
        scratch_shapes=scratch,
        compiler_params=_params("arbitrary", "arbitrary", "arbitrary"),
        name="matmul_ws",
    )(*([x] * x_parts), w)


def _merge_kernel(g0, g1, g2, g3, y0, y1, y2, y3, w_ref, o_ref, wbf_ref):
    @pl.when(pl.program_id(1) == 0)
    def _():
        wbf_ref[...] = w_ref[...].astype(BF16)

    acc = None
    for b, (g, y) in enumerate(((g0, y0), (g1, y1), (g2, y2), (g3, y3))):
        t = g[...].astype(F32) * _dot(y[...].astype(BF16), wbf_ref[b])
        acc = t if acc is None else acc + t
    o_ref[...] = acc.astype(o_ref.dtype)


def merge_branches(gates, ys, w_branch, layer, *, rows, tm=1024, tn=512):
    d = w_branch.shape[-1]
    bw = w_branch.shape[-2]
    tm = math.gcd(tm, rows)
    assert rows % tm == 0 and d % tn == 0
    nj = d // tn
    g_specs = [pl.BlockSpec((tm, tn), functools.partial(lambda j, i, b: (i, b * nj + j), b=b))
               for b in range(N_BRANCH)]
    y_specs = [pl.BlockSpec((tm, bw), lambda j, i: (i, 0)) for _ in range(N_BRANCH)]
    return pl.pallas_call(
        _merge_kernel,
        grid=(nj, rows // tm),
        in_specs=g_specs + y_specs + [pl.BlockSpec((None, N_BRANCH, bw, tn), lambda j, i: (layer, 0, 0, j))],
        out_specs=pl.BlockSpec((tm, tn), lambda j, i: (i, j)),
        out_shape=jax.ShapeDtypeStruct((rows, d), BF16),
        scratch_shapes=[pltpu.VMEM((N_BRANCH, bw, tn), BF16)],
        compiler_params=_params("arbitrary", "arbitrary"),
        name="merge_branches",
    )(gates, gates, gates, gates, *ys, w_branch)


def _rms(x, eps=NORM_EPS):
    return x * lax.rsqrt(jnp.mean(x * x, axis=-1, keepdims=True) + eps)


def _norm_mod_kernel(x_ref, g_ref, shift_ref, scale_ref, o_ref):
    x = x_ref[...].astype(F32)
    y = _rms(x) * g_ref[...]
    o_ref[...] = (y * (1.0 + scale_ref[...]) + shift_ref[...]).astype(o_ref.dtype)


def _group_of_rows(tm, rows_per_group, n_groups):
    return lambda i: jnp.minimum((i * tm) // rows_per_group, n_groups - 1)


def norm_modulate(x, g, mod, shift_idx, scale_idx, *, rows, rows_per_group, tm=PREP_ROWS):
    d = x.shape[1]
    tm = min(tm, rows_per_group)
    assert rows % tm == 0 and rows_per_group % tm == 0
    grp = _group_of_rows(tm, rows_per_group, mod.shape[0])
    return pl.pallas_call(
        _norm_mod_kernel,
        grid=(rows // tm,),
        in_specs=[pl.BlockSpec((tm, d), lambda i: (i, 0)),
                  pl.BlockSpec((1, d), lambda i: (0, 0)),
                  pl.BlockSpec((None, 1, d), lambda i: (grp(i), 0, shift_idx)),
                  pl.BlockSpec((None, 1, d), lambda i: (grp(i), 0, scale_idx))],
        out_specs=pl.BlockSpec((tm, d), lambda i: (i, 0)),
        out_shape=jax.ShapeDtypeStruct((rows, d), BF16),
        compiler_params=_params("arbitrary"),
        name="norm_modulate",
    )(x, g.reshape(1, d), mod, mod)


def _norm_res_kernel(x_ref, y_ref, g_ref, m_ref, *rest):
    y = _rms(y_ref[...].astype(F32)) * g_ref[...]
    x = x_ref[...] + m_ref[...] * y
    if len(rest) == 1:
        rest[0][...] = x
        return
    g2_ref, shift_ref, scale_ref, o_ref, h_ref = rest
    o_ref[...] = x
    h_ref[...] = (_rms(x) * g2_ref[...] * (1.0 + scale_ref[...]) + shift_ref[...]).astype(h_ref.dtype)


def norm_residual(x, y, g, mod, gate_idx, *, rows, rows_per_group, tm=PREP_ROWS, next_norm=None):
    d = x.shape[1]
    tm = min(tm, rows_per_group)
    assert rows % tm == 0 and rows_per_group % tm == 0
    grp = _group_of_rows(tm, rows_per_group, mod.shape[0])
    tile = pl.BlockSpec((tm, d), lambda i: (i, 0))
    vec = pl.BlockSpec((1, d), lambda i: (0, 0))
    mod_spec = lambda idx: pl.BlockSpec((None, 1, d), lambda i: (grp(i), 0, idx))
    in_specs = [tile, tile, vec, mod_spec(gate_idx)]
    args = [x, y, g.reshape(1, d), mod]
    out_specs, out_shape = tile, jax.ShapeDtypeStruct((rows, d), F32)
    if next_norm is not None:
        g2, mod2, shift_idx, scale_idx = next_norm
        in_specs += [vec, mod_spec(shift_idx), mod_spec(scale_idx)]
        args += [g2.reshape(1, d), mod2, mod2]
        out_specs, out_shape = [tile, tile], [out_shape, jax.ShapeDtypeStruct((rows, d), BF16)]
    return pl.pallas_call(
        _norm_res_kernel,
        grid=(rows // tm,),
        in_specs=in_specs,
        out_specs=out_specs,
        out_shape=out_shape,
        compiler_params=_params("arbitrary"),
        name="norm_residual",
    )(*args)


class _Rows:
    def __init__(self, batch, n_lat, n_ctx):
        self.batch, self.n_lat, self.n_ctx = batch, n_lat, n_ctx
        self.m_lat = batch * n_lat
        self.m = batch * (n_lat + n_ctx)

    def split(self, t):
        c = t.shape[-1]
        return (t[:self.m_lat].reshape(self.batch, self.n_lat, c),
                t[self.m_lat:].reshape(self.batch, self.n_ctx, c))

    def join(self, lat, ctx):
        c = lat.shape[-1]
        return jnp.concatenate([lat.reshape(self.m_lat, c), ctx.reshape(self.batch * self.n_ctx, c)], axis=0)

    def seq_edges(self, i, tm):
        r0 = i * tm
        in_lat = r0 < self.m_lat
        pos = jnp.where(in_lat, r0 % self.n_lat, (r0 - self.m_lat) % self.n_ctx)
        length = jnp.where(in_lat, self.n_lat, self.n_ctx)
        return pos == 0, pos + tm == length

    def chunk_block(self, chunk):
        ncc, nlc = self.n_ctx // chunk, self.n_lat // chunk
        ctx0 = self.m_lat // chunk

        def block(d, b, c):
            in_ctx = c < ncc
            cc = jnp.where(in_ctx, c, c - ncc)
            n_seg = jnp.where(in_ctx, ncc, nlc)
            cc = jnp.where(d == 1, n_seg - 1 - cc, cc)
            return jnp.where(in_ctx, ctx0 + b * ncc, b * nlc) + cc
        return block, ncc + nlc


def _conv3_tile(x, prev_row, next_row, w_ref, first, last):
    tm = x.shape[0]
    rid = lax.broadcasted_iota(jnp.int32, x.shape, 0)
    prev_row = jnp.where(first, 0.0, prev_row)
    next_row = jnp.where(last, 0.0, next_row)
    prev = jnp.where(rid == 0, prev_row, pltpu.roll(x, 1, 0))
    nxt = jnp.where(rid == tm - 1, next_row, pltpu.roll(x, tm - 1, 0))
    return prev * w_ref[0:1, :] + x * w_ref[1:2, :] + nxt * w_ref[2:3, :]


def _halo_specs(tm, width, col_block, n_rows):
    per = tm // SUBLANES
    last = n_rows // SUBLANES - 1
    return [pl.BlockSpec((SUBLANES, width), lambda i: (jnp.maximum(i * per - 1, 0), col_block)),
            pl.BlockSpec((SUBLANES, width), lambda i: (jnp.minimum((i + 1) * per, last), col_block))]


def _chunk_cumsum_matrix(tm, chunk, reverse):
    r = lax.broadcasted_iota(jnp.int32, (tm, tm), 0)
    c = lax.broadcasted_iota(jnp.int32, (tm, tm), 1)
    same = (r // chunk) == (c // chunk)
    tri = (r <= c) if reverse else (r >= c)
    return jnp.logical_and(same, tri).astype(BF16)


def _rwkv_prep_kernel(rkv_ref, hp_ref, hn_ref, low_ref, shift_ref, w0_ref, a0_ref, wup_ref, aup_ref, gup_ref,
                      kk_ref, ka_ref, rk_ref, e_ref,
                      at_ref, rt_ref, bt_ref, kt_ref, bc_ref, kc_ref, gend_ref, v_ref, gate_ref, bonus_ref,
                      *, rows, chunk):
    tm = rkv_ref.shape[0]
    bw = v_ref.shape[1]
    first, last = rows.seq_edges(pl.program_id(0), tm)
    rkv = _conv3_tile(rkv_ref[...], hp_ref[SUBLANES - 1:SUBLANES, :], hn_ref[0:1, :], shift_ref, first, last)
    r, k, v = rkv[:, :bw], rkv[:, bw:2 * bw], rkv[:, 2 * bw:]
    e = e_ref[...]
    kk = k * kk_ref[...]
    kk = kk * lax.rsqrt(_dot_split_lhs(kk * kk, e) + 1e-12)
    low = low_ref[...]
    wd = low[:, :RWKV_DECAY_RANK]
    ad = low[:, RWKV_DECAY_RANK:RWKV_DECAY_RANK + RWKV_AAA_RANK]
    gd = low[:, RWKV_DECAY_RANK + RWKV_AAA_RANK:]
    w_pre = _dot_hi(jnp.tanh(wd), wup_ref[...])
    a_pre = _dot_hi(ad, aup_ref[...])
    gate_ref[...] = _dot_hi(jax.nn.sigmoid(gd), gup_ref[...])
    v_ref[...] = v.astype(BF16)
    kt_sum = None
    for d in range(2):
        w_log = -jax.nn.softplus(-(w0_ref[d:d + 1, :] + w_pre[:, d * bw:(d + 1) * bw])) - 0.5
        lw = -jnp.exp(w_log)
        a = jax.nn.sigmoid(a0_ref[d:d + 1, :] + a_pre[:, d * bw:(d + 1) * bw])
        kt = k * (1.0 + (a - 1.0) * ka_ref[...])
        kt_sum = kt if kt_sum is None else kt_sum + kt
        b = kk * a
        lam = _dot_split_rhs(_chunk_cumsum_matrix(tm, chunk, d == 1), lw)
        g_inv = jnp.exp(-lam)
        at_ref[d] = (-kk * jnp.exp(lam - lw)).astype(BF16)
        rt_ref[d] = (r * jnp.exp(lam)).astype(BF16)
        bt_ref[d] = (b * g_inv).astype(BF16)
        kt_ref[d] = (kt * g_inv).astype(BF16)
        for q in range(tm // chunk):
            end = q * chunk if d == 1 else (q + 1) * chunk - 1
            lam_end = lam[end:end + 1, :]
            sl = slice(q * chunk, (q + 1) * chunk)
            to_end = jnp.exp(lam_end - lam[sl])
            bc_ref[d, sl, :] = (b[sl] * to_end).astype(BF16)
            kc_ref[d, sl, :] = (kt[sl] * to_end).astype(BF16)
            gend_ref[d, q] = jnp.exp(lam_end)
    bonus_ref[...] = _dot_split_lhs(r * kt_sum * rk_ref[...], e) * v


def rwkv_prep(pa, rows, layer, shift, w0, w_up, a0, a_up, g_up, k_k, k_a, r_k, seg_ones, *, tm=PREP_ROWS,
              chunk=RWKV_CHUNK):
    m = pa.shape[0]
    bw = k_k.shape[-1]
    low_w = RWKV_DECAY_RANK + RWKV_AAA_RANK + RWKV_GATE_RANK
    tm = min(tm, rows.n_ctx)
    assert m % tm == 0 and rows.n_ctx % tm == 0 and rows.n_lat % tm == 0 and tm % chunk == 0
    assert (3 * bw) % low_w == 0
    wup = jnp.concatenate([w_up[layer, 0], w_up[layer, 1]], axis=1)
    aup = jnp.concatenate([a_up[layer, 0], a_up[layer, 1]], axis=1)
    row = lambda t: t.reshape(1, bw)
    big = jax.ShapeDtypeStruct((2, m, bw), BF16)
    big_spec = pl.BlockSpec((2, tm, bw), lambda i: (0, i, 0))
    one_spec = pl.BlockSpec((tm, bw), lambda i: (i, 0))
    cpt = tm // chunk
    return pl.pallas_call(
        functools.partial(_rwkv_prep_kernel, rows=rows, chunk=chunk),
        grid=(m // tm,),
        in_specs=[pl.BlockSpec((tm, 3 * bw), lambda i: (i, 0))] + _halo_specs(tm, 3 * bw, 0, m)
        + [pl.BlockSpec((tm, low_w), lambda i: (i, 3 * bw // low_w)),
           _full((3, 3 * bw)), _full((2, bw)), _full((2, bw)), _full(wup.shape), _full(aup.shape),
           pl.BlockSpec((None,) + g_up.shape[1:], lambda i: (layer, 0, 0)),
           _full((1, bw)), _full((1, bw)), _full((1, bw)), _full(seg_ones.shape)],
        out_specs=[big_spec] * 6 + [pl.BlockSpec((2, cpt, 1, bw), lambda i: (0, i, 0, 0)),
                                    one_spec, one_spec, one_spec],
        out_shape=[big] * 6 + [jax.ShapeDtypeStruct((2, m // chunk, 1, bw), F32),
                               jax.ShapeDtypeStruct((m, bw), BF16), jax.ShapeDtypeStruct((m, bw), F32),
                               jax.ShapeDtypeStruct((m, bw), F32)],
        compiler_params=_params("arbitrary"),
        name="rwkv_prep",
    )(pa, pa, pa, pa, shift[layer], w0[layer], a0[layer], wup, aup, g_up, row(k_k[layer]), row(k_a[layer]),
      row(r_k[layer]), seg_ones)


def _pair_select(lo, z):
    half = z.shape[0] // 2
    return jnp.where(lo, z[:half], z[half:])


def _rwkv_scan_kernel(at_ref, rt_ref, bt_ref, kt_ref, bc_ref, kc_ref, gend_ref, v_ref, y_ref, s_ref, *, chunk):
    @pl.when(pl.program_id(2) == 0)
    def _():
        s_ref[...] = jnp.zeros_like(s_ref)

    c2 = 2 * chunk
    n_pairs = s_ref.shape[0]
    pw = 2 * RWKV_HEAD
    n_levels = chunk.bit_length() - 1
    order = lax.broadcasted_iota(jnp.int32, (chunk, chunk), 0) - lax.broadcasted_iota(jnp.int32, (chunk, chunk), 1)
    order = jnp.where(pl.program_id(0) == 1, -order, order)
    strict = order > 0
    incl = order >= 0
    lo = lax.broadcasted_iota(jnp.int32, (chunk, pw), 1) < RWKV_HEAD
    lo2 = lax.broadcasted_iota(jnp.int32, (c2, pw), 1) < RWKV_HEAD
    own = ((lax.broadcasted_iota(jnp.int32, (pw, pw), 0) < RWKV_HEAD)
           == (lax.broadcasted_iota(jnp.int32, (pw, pw), 1) < RWKV_HEAD))
    pairs = range(n_pairs)
    sl = [slice(j * pw, (j + 1) * pw) for j in pairs]

    def tri(mask, z):
        return jnp.where(mask, z, 0.0)

    a = [at_ref[:, s] for s in sl]
    r = [rt_ref[:, s] for s in sl]
    v = [v_ref[:, s] for s in sl]
    s0 = [s_ref[j] for j in pairs]
    ar = [jnp.concatenate([a[j], r[j]], axis=0) for j in pairs]
    ar_st = [jnp.concatenate([jnp.where(lo2, ar[j], 0), jnp.where(lo2, 0, ar[j])], axis=0) for j in pairs]
    p_b = [_dot_nt(ar_st[j], bt_ref[:, sl[j]]) for j in pairs]
    p_k = [_dot_nt(ar_st[j], kt_ref[:, sl[j]]) for j in pairs]
    lp = [[tri(strict, p_b[j][e * c2:e * c2 + chunk]) for e in range(2)] for j in pairs]
    l_ak = [jnp.concatenate([tri(strict, p_k[j][e * c2:e * c2 + chunk]) for e in range(2)], axis=0).astype(BF16)
            for j in pairs]
    m_rb = [jnp.concatenate([tri(incl, p_b[j][e * c2 + chunk:(e + 1) * c2]) for e in range(2)], axis=0).astype(BF16)
            for j in pairs]
    m_rk = [jnp.concatenate([tri(incl, p_k[j][e * c2 + chunk:(e + 1) * c2]) for e in range(2)], axis=0).astype(BF16)
            for j in pairs]
    x_w = [a[j].astype(F32) for j in pairs]
    x_u = [_pair_select(lo, _dot(l_ak[j], v[j])) for j in pairs]
    for lvl in range(n_levels):
        lpb = [[lp[j][e].astype(BF16) for e in range(2)] for j in pairs]
        lst = [jnp.concatenate(lpb[j], axis=0) for j in pairs]
        z = [_dot(lst[j], jnp.concatenate([x_w[j], x_u[j]], axis=1).astype(BF16)) for j in pairs]
        x_w = [x_w[j] + _pair_select(lo, z[j][:, :pw]) for j in pairs]
        x_u = [x_u[j] + _pair_select(lo, z[j][:, pw:]) for j in pairs]
        if lvl + 1 < n_levels:
            lp = [[_dot(lpb[j][e], lpb[j][e]) for e in range(2)] for j in pairs]
    s0b = [s0[j].astype(BF16) for j in pairs]
    u = [_dot_nt(x_w[j].astype(BF16), s0b[j]) + x_u[j] for j in pairs]
    ub = [u[j].astype(BF16) for j in pairs]
    y = [_dot_nt(r[j], s0b[j]) + _pair_select(lo, _dot(m_rb[j], ub[j])) + _pair_select(lo, _dot(m_rk[j], v[j]))
         for j in pairs]
    s_new = [s0[j] * gend_ref[:, sl[j]]
             + jnp.where(own, _dot_tn(ub[j], bc_ref[:, sl[j]]) + _dot_tn(v[j], kc_ref[:, sl[j]]), 0.0)
             for j in pairs]
    for j in pairs:
        y_ref[:, sl[j]] = y[j]
    for j in pairs:
        s_ref[j] = s_new[j]


def rwkv_scan(ops, v, rows, *, chunk=RWKV_CHUNK):
    at, rt, bt, kt, bc, kc, gend = ops
    _, m, bw = at.shape
    block, nc = rows.chunk_block(chunk)
    big = pl.BlockSpec((None, chunk, bw), lambda d, b, c: (d, block(d, b, c), 0))
    return pl.pallas_call(
        functools.partial(_rwkv_scan_kernel, chunk=chunk),
        grid=(2, rows.batch, nc),
        in_specs=[big] * 6 + [pl.BlockSpec((None, None, 1, bw), lambda d, b, c: (d, block(d, b, c), 0, 0)),
                              pl.BlockSpec((chunk, bw), lambda d, b, c: (block(d, b, c), 0))],
        out_specs=big,
        out_shape=jax.ShapeDtypeStruct((2, m, bw), F32),
        scratch_shapes=[pltpu.VMEM((bw // (2 * RWKV_HEAD), 2 * RWKV_HEAD, 2 * RWKV_HEAD), F32)],
        compiler_params=_params("arbitrary", "arbitrary", "arbitrary"),
        name="rwkv_scan",
    )(at, rt, bt, kt, bc, kc, gend, v)


def _rwkv_readout_kernel(y_ref, bonus_ref, gate_ref, e_ref, g_ref, b_ref, o_ref):
    y = y_ref[0] + y_ref[1]
    e = e_ref[...]
    mu = _dot_split_lhs(y, e) * (1.0 / RWKV_HEAD)
    yc = y - mu
    var = _dot_split_lhs(yc * yc, e) * (1.0 / RWKV_HEAD)
    yn = yc * lax.rsqrt(var + RWKV_LN_EPS) * g_ref[...] + b_ref[...]
    o_ref[...] = ((yn + bonus_ref[...]) * gate_ref[...]).astype(o_ref.dtype)


def rwkv_readout(y, bonus, gate, seg_ones, ln_g, ln_b, *, rows, tm=PREP_ROWS):
    bw = y.shape[-1]
    tm = math.gcd(tm, rows)
    one = pl.BlockSpec((tm, bw), lambda i: (i, 0))
    return pl.pallas_call(
        _rwkv_readout_kernel,
        grid=(rows // tm,),
        in_specs=[pl.BlockSpec((2, tm, bw), lambda i: (0, i, 0)), one, one, _full(seg_ones.shape),
                  _full((1, bw)), _full((1, bw))],
        out_specs=one,
        out_shape=jax.ShapeDtypeStruct((rows, bw), BF16),
        compiler_params=_params("arbitrary"),
        name="rwkv_readout",
    )(y, bonus, gate, seg_ones, ln_g.reshape(1, bw), ln_b.reshape(1, bw))


def _ssd_prep_kernel(x_ref, hp_ref, hn_ref, w_ref, b_ref, o_ref, *, rows):
    first, last = rows.seq_edges(pl.program_id(0), x_ref.shape[0])
    y = _conv3_tile(x_ref[...], hp_ref[SUBLANES - 1:SUBLANES, :], hn_ref[0:1, :], w_ref, first, last) + b_ref[...]
    o_ref[...] = y * jax.nn.sigmoid(y)


def ssd_prep(pc, rows, conv_w, conv_b, *, width, first_block, n_blocks, tm=PREP_ROWS):
    m = pc.shape[0]
    tm = min(tm, rows.n_ctx)
    assert m % tm == 0 and rows.n_ctx % tm == 0 and rows.n_lat % tm == 0
    per = tm // SUBLANES
    last = m // SUBLANES - 1
    return pl.pallas_call(
        functools.partial(_ssd_prep_kernel, rows=rows),
        grid=(m // tm, n_blocks),
        in_specs=[pl.BlockSpec((tm, width), lambda i, j: (i, first_block + j)),
                  pl.BlockSpec((SUBLANES, width), lambda i, j: (jnp.maximum(i * per - 1, 0), first_block + j)),
                  pl.BlockSpec((SUBLANES, width), lambda i, j: (jnp.minimum((i + 1) * per, last), first_block + j)),
                  pl.BlockSpec((3, width), lambda i, j: (0, j)),
                  pl.BlockSpec((1, width), lambda i, j: (0, j))],
        out_specs=pl.BlockSpec((tm, width), lambda i, j: (i, j)),
        out_shape=jax.ShapeDtypeStruct((m, n_blocks * width), F32),
        compiler_params=_params("arbitrary", "arbitrary"),
        name="ssd_prep",
    )(pc, pc, pc, conv_w, conv_b.reshape(1, n_blocks * width))


def _ssd_scan_kernel(x_ref, b_ref, c_ref, la_ref, lat_ref, dt_ref, y_ref, s_ref, *, chunk, heads, groups):
    @pl.when(pl.program_id(2) == 0)
    def _():
        s_ref[...] = jnp.zeros_like(s_ref)

    rev = pl.program_id(0) == 1
    pw = 2 * SSD_HEAD
    order = lax.broadcasted_iota(jnp.int32, (chunk, chunk), 0) - lax.broadcasted_iota(jnp.int32, (chunk, chunk), 1)
    order = jnp.where(rev, -order, order)
    upto = order >= 0
    lo = lax.broadcasted_iota(jnp.int32, (chunk, pw), 1) < SSD_HEAD
    acs_c_all = _dot_hi(upto.astype(F32), la_ref[...])
    acs_r_all = _dot_hi(lat_ref[...], (order <= 0).astype(F32))
    tot_c_all = jnp.where(rev, acs_c_all[0:1, :], acs_c_all[chunk - 1:chunk, :])
    dt_all = dt_ref[...]
    hpg = heads // groups
    for gi in range(groups):
        cg = c_ref[:, gi * SSD_STATE:(gi + 1) * SSD_STATE].astype(BF16)
        bg = b_ref[:, gi * SSD_STATE:(gi + 1) * SSD_STATE].astype(BF16)
        cb = _dot_nt(cg, bg)
        for qi in range(hpg // 2):
            h0 = gi * hpg + 2 * qi
            j = h0 // 2
            cols = slice(j * pw, (j + 1) * pw)
            w_st, acs_c, tot = [], [], []
            for e in range(2):
                h = h0 + e
                acs_c.append(acs_c_all[:, h:h + 1])
                tot.append(tot_c_all[:, h:h + 1])
                decay = jnp.exp(jnp.where(upto, acs_c[e] - acs_r_all[h:h + 1, :], -1e30))
                w_st.append((cb * decay).astype(BF16))
            xdt = x_ref[:, cols] * jnp.where(lo, dt_all[:, h0:h0 + 1], dt_all[:, h0 + 1:h0 + 2])
            st = s_ref[j]
            y = _pair_select(lo, _dot(jnp.concatenate(w_st, axis=0), xdt.astype(BF16)))
            y = y + _dot(cg, st.astype(BF16)) * jnp.where(lo, jnp.exp(acs_c[0]), jnp.exp(acs_c[1]))
            y_ref[:, cols] = y
            to_end = jnp.where(lo, jnp.exp(tot[0] - acs_c[0]), jnp.exp(tot[1] - acs_c[1]))
            s_ref[j] = (st * jnp.where(lo[0:1, :], jnp.exp(tot[0]), jnp.exp(tot[1]))
                        + _dot_tn(bg, (xdt * to_end).astype(BF16)))


def ssd_scan(xbc, la, dt, rows, *, heads, chunk=SSD_CHUNK):
    m = xbc.shape[0]
    bw = heads * SSD_HEAD
    gn = SSD_GROUPS * SSD_STATE
    assert bw % gn == 0
    block, nc = rows.chunk_block(chunk)
    lat = jnp.swapaxes(la, 1, 2)
    thin = pl.BlockSpec((None, chunk, heads), lambda d, b, c: (d, block(d, b, c), 0))
    return pl.pallas_call(
        functools.partial(_ssd_scan_kernel, chunk=chunk, heads=heads, groups=SSD_GROUPS),
        grid=(2, rows.batch, nc),
        in_specs=[pl.BlockSpec((chunk, bw), lambda d, b, c: (block(d, b, c), 0)),
                  pl.BlockSpec((chunk, gn), lambda d, b, c: (block(d, b, c), bw // gn)),
                  pl.BlockSpec((chunk, gn), lambda d, b, c: (block(d, b, c), bw // gn + 1)),
                  thin,
                  pl.BlockSpec((None, heads, chunk), lambda d, b, c: (d, 0, block(d, b, c))),
                  thin],
        out_specs=pl.BlockSpec((None, chunk, bw), lambda d, b, c: (d, block(d, b, c), 0)),
        out_shape=jax.ShapeDtypeStruct((2, m, bw), F32),
        scratch_shapes=[pltpu.VMEM((heads // 2, SSD_STATE, 2 * SSD_HEAD), F32)],
        compiler_params=_params("arbitrary", "arbitrary", "arbitrary"),
        name="ssd_scan",
    )(xbc, xbc, xbc, la, lat, dt)


def _ssd_readout_kernel(y_ref, x_ref, z_ref, d_ref, g_ref, o_ref):
    y = y_ref[0] + y_ref[1] + d_ref[...] * x_ref[...]
    z = z_ref[...]
    o_ref[...] = (_rms(y * (z * jax.nn.sigmoid(z))) * g_ref[...]).astype(o_ref.dtype)


def ssd_readout(y, xbc, pc, d_skip, norm_g, *, rows, tm=PREP_ROWS):
    bw = y.shape[-1]
    tm = math.gcd(tm, rows)
    one = pl.BlockSpec((tm, bw), lambda i: (i, 0))
    return pl.pallas_call(
        _ssd_readout_kernel,
        grid=(rows // tm,),
        in_specs=[pl.BlockSpec((2, tm, bw), lambda i: (0, i, 0)), one, one, _full((1, bw)), _full((1, bw))],
        out_specs=one,
        out_shape=jax.ShapeDtypeStruct((rows, bw), BF16),
        compiler_params=_params("arbitrary"),
        name="ssd_readout",
    )(y, xbc, pc, jnp.repeat(d_skip, SSD_HEAD).reshape(1, bw), norm_g.reshape(1, bw))


def _softmax_pv(scores, values):
    m = None
    for s in scores:
        ms = jnp.max(s, axis=-1, keepdims=True)
        m = ms if m is None else jnp.maximum(m, ms)
    l, o = None, None
    for s, v in zip(scores, values):
        p = jnp.exp(s - m)
        ls = jnp.sum(p, axis=-1, keepdims=True)
        os_ = _dot(p.astype(BF16), v)
        l = ls if l is None else l + ls
        o = os_ if o is None else o + os_
    return o / l


def _attn_kernel(lam_ref, q_ref, g_ref, *refs, n_seg, scale, diff, out_scale, row_parts):
    seg_refs = refs[:2 * n_seg]
    o_ref = refs[2 * n_seg]
    values = [seg_refs[2 * si + 1][...] for si in range(n_seg)]
    rows_per = q_ref.shape[0] // row_parts
    for r in range(row_parts):
        rs = slice(r * rows_per, (r + 1) * rows_per)
        q = q_ref[rs, :].astype(F32) * scale
        if not diff:
            qb = q.astype(BF16)
            scores = [_dot_nt(qb, seg_refs[2 * si][...]) for si in range(n_seg)]
            o_ref[rs, :] = _softmax_pv(scores, values).astype(o_ref.dtype)
            continue
        first = lax.broadcasted_iota(jnp.int32, q.shape, 1) < DIFF_HEAD
        q1 = jnp.where(first, q, 0.0).astype(BF16)
        q2 = jnp.where(first, 0.0, q).astype(BF16)
        o1 = _softmax_pv([_dot_nt(q1, seg_refs[2 * si][...]) for si in range(n_seg)], values)
        o2 = _softmax_pv([_dot_nt(q2, seg_refs[2 * si][...]) for si in range(n_seg)], values)
        o = o1 - lam_ref[0] * o2
        o_ref[rs, :] = (_rms(o) * g_ref[...] * out_scale).astype(o_ref.dtype)


def attention(q, k, v, *, heads, batch, q_rows, q_first, segments, scale, tq=256, row_parts=1,
              diff_lam=None, diff_gain=None, out_scale=1.0):
    dqk = q.shape[1] // heads
    dv = v.shape[1] // heads
    tq = min(tq, q_rows)
    nq = q_rows // tq
    if tq % (row_parts * 2 * SUBLANES):
        row_parts = 1
    diff = diff_lam is not None
    lam = (diff_lam if diff else jnp.zeros((), F32)).reshape(1).astype(F32)
    gain = (diff_gain if diff else jnp.ones((dv,), F32)).reshape(1, dv).astype(F32)
    seg_specs = []
    for seg_rows, first in segments:
        for w in (dqk, dv):
            seg_specs.append(pl.BlockSpec((seg_rows, w), functools.partial(lambda b, h, i, f: (f + b, h), f=first)))
    return pl.pallas_call(
        functools.partial(_attn_kernel, n_seg=len(segments), scale=scale, diff=diff, out_scale=out_scale,
                          row_parts=row_parts),
        grid=(batch, heads, nq),
        in_specs=[pl.BlockSpec(memory_space=pltpu.SMEM),
                  pl.BlockSpec((tq, dqk), lambda b, h, i: ((q_first + b) * nq + i, h)),
                  pl.BlockSpec((1, dv), lambda b, h, i: (0, 0))] + seg_specs,
        out_specs=pl.BlockSpec((tq, dv), lambda b, h, i: (b * nq + i, h)),
        out_shape=jax.ShapeDtypeStruct((batch * q_rows, heads * dv), BF16),
        compiler_params=_params("arbitrary", "arbitrary", "arbitrary"),
        name="diff_attention" if diff else "mla_attention",
    )(lam, q, gain, *([k, v] * len(segments)))


ROPE_GROUP = 64
ROPE_QUARTER = ROPE_GROUP // 4


def _axial_rope_tables(n_tok):
    t = jnp.arange(n_tok)
    row = (t // GRID_W).astype(F32)
    col = (t % GRID_W).astype(F32)
    axis_dim = ROPE_GROUP // 2
    inv = 1.0 / (ROPE_BASE ** (jnp.arange(0, axis_dim, 2, dtype=F32) / axis_dim))
    ar = row[:, None] * inv[None]
    ac = col[:, None] * inv[None]
    ang = jnp.concatenate([ar, ar, ac, ac], axis=-1)
    sign = jnp.where((jnp.arange(ROPE_GROUP) % (2 * ROPE_QUARTER)) < ROPE_QUARTER, -1.0, 1.0)
    return jnp.cos(ang), jnp.sin(ang) * sign


def _rope_lanes(x, cos, sin_signed):
    lane = lax.broadcasted_iota(jnp.int32, x.shape, 1)
    takes_next = (lane % (2 * ROPE_QUARTER)) < ROPE_QUARTER
    rot = jnp.where(takes_next, pltpu.roll(x, LANES - ROPE_QUARTER, 1), pltpu.roll(x, ROPE_QUARTER, 1))
    return x * cos + rot * sin_signed


def _rope_table_spec(rows, tm):
    per_seq = rows.n_lat // tm
    return pl.BlockSpec((tm, LANES), lambda i: (jnp.where(i * tm < rows.m_lat, i % per_seq, 0), 0))


def _tile_tables(i, tm, rows, cos_ref, sin_ref):
    is_lat = i * tm < rows.m_lat
    return jnp.where(is_lat, cos_ref[...], 1.0), jnp.where(is_lat, sin_ref[...], 0.0)


def _diff_prep_kernel(q_ref, k_ref, v_ref, cos_ref, sin_ref, qo_ref, ko_ref, vo_ref, *, rows):
    tm = q_ref.shape[0]
    cos, sin = _tile_tables(pl.program_id(0), tm, rows, cos_ref, sin_ref)
    for src, dst in ((q_ref, qo_ref), (k_ref, ko_ref)):
        for cb in range(src.shape[1] // LANES):
            cols = slice(cb * LANES, (cb + 1) * LANES)
            dst[:, cols] = _rope_lanes(src[:, cols], cos, sin).astype(dst.dtype)
    vo_ref[...] = v_ref[...].astype(vo_ref.dtype)


def diff_prep(pd, rows, cos2, sin2, *, tm=PREP_ROWS):
    m = pd.shape[0]
    bw = pd.shape[1] // 3
    tm = min(tm, rows.n_ctx)
    assert m % tm == 0 and rows.n_ctx % tm == 0 and rows.n_lat % tm == 0
    out = jax.ShapeDtypeStruct((m, bw), BF16)
    one = pl.BlockSpec((tm, bw), lambda i: (i, 0))
    tab = _rope_table_spec(rows, tm)
    return pl.pallas_call(
        functools.partial(_diff_prep_kernel, rows=rows),
        grid=(m // tm,),
        in_specs=[pl.BlockSpec((tm, bw), functools.partial(lambda i, j: (i, j), j=j)) for j in range(3)] + [tab, tab],
        out_specs=[one] * 3,
        out_shape=[out] * 3,
        compiler_params=_params("arbitrary"),
        name="diff_prep",
    )(pd, pd, pd, cos2, sin2)


def _mla_q_kernel(cq_ref, g_ref, w_ref, cos_ref, sin_ref, o_ref, *, rows):
    tm = cq_ref.shape[0]
    cos, sin = _tile_tables(pl.program_id(0), tm, rows, cos_ref, sin_ref)
    q = _dot((_rms(cq_ref[...]) * g_ref[...]).astype(BF16), w_ref[...])
    for h in range(MLA_HEADS):
        nope = slice(2 * h * LANES, (2 * h + 1) * LANES)
        rope = slice((2 * h + 1) * LANES, (2 * h + 2) * LANES)
        o_ref[:, nope] = q[:, nope].astype(o_ref.dtype)
        o_ref[:, rope] = _rope_lanes(q[:, rope], cos, sin).astype(o_ref.dtype)


def mla_q_prep(pb, rows, q_norm, w_uq, cos2, sin2, *, tm=PREP_ROWS):
    m = pb.shape[0]
    tm = min(tm, rows.n_ctx)
    w = w_uq.reshape(MLA_Q_RANK, MLA_HEADS, MLA_NOPE + MLA_ROPE)
    w = jnp.pad(w, ((0, 0), (0, 0), (0, 2 * LANES - MLA_NOPE - MLA_ROPE))).reshape(MLA_Q_RANK, -1).astype(BF16)
    tab = _rope_table_spec(rows, tm)
    return pl.pallas_call(
        functools.partial(_mla_q_kernel, rows=rows),
        grid=(m // tm,),
        in_specs=[pl.BlockSpec((tm, MLA_Q_RANK), lambda i: (i, 0)), _full((1, MLA_Q_RANK)), _full(w.shape), tab, tab],
        out_specs=pl.BlockSpec((tm, w.shape[1]), lambda i: (i, 0)),
        out_shape=jax.ShapeDtypeStruct((m, w.shape[1]), BF16),
        compiler_params=_params("arbitrary"),
        name="mla_q_prep",
    )(pb, q_norm.reshape(1, MLA_Q_RANK), w, cos2, sin2)


def _mla_kv_kernel(ckv_ref, kr_ref, g_ref, w_ref, cos_ref, sin_ref, k_ref, v_ref, *, rows):
    tm = ckv_ref.shape[0]
    cos, sin = _tile_tables(pl.program_id(0), tm, rows, cos_ref, sin_ref)
    kv = _dot((_rms(ckv_ref[...]) * g_ref[...]).astype(BF16), w_ref[...])
    lane = lax.broadcasted_iota(jnp.int32, (tm, LANES), 1)
    kr = jnp.where(lane < MLA_ROPE, _rope_lanes(kr_ref[...], cos, sin), 0.0).astype(k_ref.dtype)
    nope_w = MLA_HEADS * MLA_NOPE
    for h in range(MLA_HEADS):
        k_ref[:, 2 * h * LANES:(2 * h + 1) * LANES] = kv[:, h * MLA_NOPE:(h + 1) * MLA_NOPE].astype(k_ref.dtype)
        k_ref[:, (2 * h + 1) * LANES:(2 * h + 2) * LANES] = kr
    v_ref[...] = kv[:, nope_w:].astype(v_ref.dtype)


def mla_kv_prep(pb, rows, kv_norm, w_ukv, cos2, sin2, *, tm=PREP_ROWS):
    m = pb.shape[0]
    tm = min(tm, rows.n_ctx)
    assert MLA_NOPE == LANES and MLA_Q_RANK % MLA_KV_RANK == 0 and (MLA_Q_RANK + MLA_KV_RANK) % LANES == 0
    w = w_ukv.reshape(MLA_KV_RANK, MLA_HEADS, 2 * MLA_NOPE)
    w = jnp.concatenate([w[:, :, :MLA_NOPE].reshape(MLA_KV_RANK, -1), w[:, :, MLA_NOPE:].reshape(MLA_KV_RANK, -1)],
                        axis=1).astype(BF16)
    nope_w = MLA_HEADS * MLA_NOPE
    tab = _rope_table_spec(rows, tm)
    return pl.pallas_call(
        functools.partial(_mla_kv_kernel, rows=rows),
        grid=(m // tm,),
        in_specs=[pl.BlockSpec((tm, MLA_KV_RANK), lambda i: (i, MLA_Q_RANK // MLA_KV_RANK)),
                  pl.BlockSpec((tm, LANES), lambda i: (i, (MLA_Q_RANK + MLA_KV_RANK) // LANES)),
                  _full((1, MLA_KV_RANK)), _full(w.shape), tab, tab],
        out_specs=[pl.BlockSpec((tm, 2 * nope_w), lambda i: (i, 0)), pl.BlockSpec((tm, nope_w), lambda i: (i, 0))],
        out_shape=[jax.ShapeDtypeStruct((m, 2 * nope_w), BF16), jax.ShapeDtypeStruct((m, nope_w), BF16)],
        compiler_params=_params("arbitrary"),
        name="mla_kv_prep",
    )(pb, pb, kv_norm.reshape(1, MLA_KV_RANK), w, cos2, sin2)


def kernel(x, c, ctx, c_ctx, ada_w, ada_b, norm_mix_pre, norm_mix_post, norm_ffn_pre, norm_ffn_post, w_in, rwkv_shift, rwkv_w0, rwkv_w_up, rwkv_a0, rwkv_a_up, rwkv_g_up, rwkv_k_k, rwkv_k_a, rwkv_r_k, rwkv_ln_g, rwkv_ln_b, mla_q_norm, mla_w_uq, mla_kv_norm, mla_w_ukv, ssd_conv_w, ssd_conv_b, ssd_dt_bias, ssd_a_log, ssd_d, ssd_norm, diff_lq1, diff_lk1, diff_lq2, diff_lk2, diff_subln, w_branch, w_gate, w_out, w_ff1, w_ff2):
    batch, n_lat, d = x.shape
    n_ctx = ctx.shape[1]
    depth = ada_w.shape[0]
    bw = d // N_BRANCH
    rows = _Rows(batch, n_lat, n_ctx)
    m_lat, m_all = rows.m_lat, rows.m
    ssd_heads = bw // SSD_HEAD
    gn = SSD_GROUPS * SSD_STATE
    rwkv_in = 3 * bw + RWKV_DECAY_RANK + RWKV_AAA_RANK + RWKV_GATE_RANK
    mla_in = MLA_Q_RANK + MLA_KV_RANK + MLA_ROPE
    ssd_in = 2 * bw + 2 * gn + ssd_heads
    diff_in = 3 * bw
    o_mla, o_ssd, o_diff = rwkv_in, rwkv_in + mla_in, rwkv_in + mla_in + ssd_in
    assert w_in.shape[-1] == o_diff + diff_in
    assert n_lat % SSD_CHUNK == 0 and n_ctx % SSD_CHUNK == 0

    assert MLA_ROPE == ROPE_GROUP and DIFF_HEAD == ROPE_GROUP
    cos2, sin2 = (jnp.tile(t, (1, LANES // ROPE_GROUP)) for t in _axial_rope_tables(n_lat))
    hid = jnp.arange(bw) // RWKV_HEAD
    seg_ones = (hid[:, None] == hid[None, :]).astype(BF16)

    xs = jnp.concatenate([x.reshape(m_lat, d), ctx.reshape(batch * n_ctx, d)], axis=0)
    cvec = jnp.zeros((8, d), F32).at[:batch].set(jax.nn.silu(c)).at[batch].set(jax.nn.silu(c_ctx))
    groups = batch + 1

    mods = [(matmul(cvec, ada_w, layer=l, tn=512) + ada_b[l])[:groups].reshape(groups, 1, 6 * d)
            for l in range(depth)]
    h = norm_modulate(xs, norm_mix_pre[0], mods[0], 0, 1, rows=m_all, rows_per_group=n_lat)

    for l in range(depth):
        ctx_out = l < depth - 1
        m_out = m_all if ctx_out else m_lat
        mod = mods[l]

        trial = l == depth - 1
        in_kw = dict(x_parts=2) if trial else {}
        pa = matmul(h, w_in[l, :, :o_mla], **in_kw)
        pb = matmul(h, w_in[l, :, o_mla:o_ssd], **in_kw)
        pc = matmul(h, w_in[l, :, o_ssd:o_diff], **in_kw)
        pd = matmul(h, w_in[l, :, o_diff:], **in_kw)

        prep = rwkv_prep(pa, rows, l, rwkv_shift, rwkv_w0, rwkv_w_up, rwkv_a0, rwkv_a_up, rwkv_g_up,
                         rwkv_k_k, rwkv_k_a, rwkv_r_k.reshape(depth, bw), seg_ones)
        y_rwkv = rwkv_scan(prep[:7], prep[7], rows)
        ya = rwkv_readout(y_rwkv, prep[9], prep[8], seg_ones, rwkv_ln_g[l], rwkv_ln_b[l], rows=m_out)

        q_m = mla_q_prep(pb, rows, mla_q_norm[l], mla_w_uq[l], cos2, sin2)
        k_m, vv = mla_kv_prep(pb, rows, mla_kv_norm[l], mla_w_ukv[l], cos2, sin2)
        ctx_blk = m_lat // n_ctx
        segs_lat = [(n_lat, 0), (n_ctx, ctx_blk)]
        mla_scale = (MLA_NOPE + MLA_ROPE) ** -0.5
        yb = attention(q_m, k_m, vv, heads=MLA_HEADS, batch=batch, q_rows=n_lat, q_first=0, segments=segs_lat,
                       scale=mla_scale, tq=512, row_parts=2)
        if ctx_out:
            yb_c = attention(q_m, k_m, vv, heads=MLA_HEADS, batch=batch, q_rows=n_ctx, q_first=ctx_blk,
                             segments=[(n_ctx, ctx_blk)], scale=mla_scale)
            yb = jnp.concatenate([yb, yb_c], axis=0)

        assert (2 * gn) % bw == 0
        xbc = ssd_prep(pc, rows, ssd_conv_w[l], ssd_conv_b[l], width=bw, first_block=1, n_blocks=1 + 2 * gn // bw)
        dt_raw = pc[:, 2 * bw + 2 * gn:2 * bw + 2 * gn + ssd_heads]
        dts = jnp.stack([jax.nn.softplus(dt_raw + ssd_dt_bias[l, dr]) for dr in range(2)])
        las = dts * (-jnp.exp(ssd_a_log[l]))[:, None, :]
        y_ssd = ssd_scan(xbc, las, dts, rows, heads=ssd_heads)
        yc = ssd_readout(y_ssd, xbc, pc, ssd_d[l], ssd_norm[l], rows=m_out)

        q_d, k_d, v_d = diff_prep(pd, rows, cos2, sin2)
        lam_init = 0.8 - 0.6 * math.exp(-0.3 * l)
        lam = (jnp.exp(jnp.sum(diff_lq1[l] * diff_lk1[l])) - jnp.exp(jnp.sum(diff_lq2[l] * diff_lk2[l])) + lam_init)
        diff_kw = dict(heads=DIFF_HEADS, batch=batch, scale=DIFF_HEAD ** -0.5, diff_lam=lam,
                       diff_gain=diff_subln[l], out_scale=1.0 - lam_init)
        yd = attention(q_d, k_d, v_d, q_rows=n_lat, q_first=0, segments=segs_lat, tq=1024, row_parts=2, **diff_kw)
        if ctx_out:
            yd_c = attention(q_d, k_d, v_d, q_rows=n_ctx, q_first=ctx_blk, segments=[(n_ctx, ctx_blk)], **diff_kw)
            yd = jnp.concatenate([yd, yd_c], axis=0)

        gates = matmul(h, w_gate, layer=l, rows=m_out, act="sigmoid", out_dtype=BF16, row_parts=4,
                       x_parts=2 if trial else 1)
        merged = merge_branches(gates, [ya, yb, yc, yd], w_branch, l, rows=m_out)
        mix = matmul(merged, w_out, layer=l, row_parts=4 if trial else 1)
        xs, hf = norm_residual(xs, mix, norm_mix_post[l], mod, 2, rows=m_out, rows_per_group=n_lat,
                               next_norm=(norm_ffn_pre[l], mod, 3, 4))
        f1 =matmul(hf, w_ff1, layer=l, act="relu2", out_dtype=BF16, **(dict(tm=512, row_parts=2) if trial else {}))
        f2 = matmul(f1, w_ff2, layer=l, tm=512)
        if ctx_out:
            xs, h = norm_residual(xs, f2, norm_ffn_post[l], mod, 5, rows=m_out, rows_per_group=n_lat,
                                  next_norm=(norm_mix_pre[l + 1], mods[l + 1], 0, 1))
        else:
            xs = norm_residual(xs, f2, norm_ffn_post[l], mod, 5, rows=m_out, rows_per_group=n_lat)

    return xs[:m_lat].reshape(batch, n_lat, d)
```

```python
import functools
import math

import jax
import jax.numpy as jnp
from jax import lax
from jax.experimental import pallas as pl
from jax.experimental.pallas import tpu as pltpu

F32 = jnp.float32
BF16 = jnp.bfloat16
HIGHEST = lax.Precision.HIGHEST

NORM_EPS = 1e-6
ROPE_BASE = 10000.0
GRID_W = 64
N_BRANCH = 4
LANES = 128
SUBLANES = 8

RWKV_HEAD = 64
RWKV_DECAY_RANK = 64
RWKV_AAA_RANK = 64
RWKV_GATE_RANK = 128
RWKV_LN_EPS = 64e-5
RWKV_CHUNK = 64

MLA_HEADS = 8
MLA_NOPE = 128
MLA_ROPE = 64
MLA_Q_RANK = 768
MLA_KV_RANK = 256

SSD_HEAD = 64
SSD_GROUPS = 4
SSD_STATE = 128
SSD_CHUNK = 128

DIFF_HEADS = 8
DIFF_HEAD = 64

PREP_ROWS = 256

VMEM_LIMIT_BYTES = 56 * 1024 * 1024


def _params(*sem):
    return pltpu.CompilerParams(dimension_semantics=sem, vmem_limit_bytes=VMEM_LIMIT_BYTES)


def _dot(a, b):
    return jnp.dot(a, b, preferred_element_type=F32)


def _dot_nt(a, b):
    return lax.dot_general(a, b, (((1,), (1,)), ((), ())), preferred_element_type=F32)


def _dot_tn(a, b):
    return lax.dot_general(a, b, (((0,), (0,)), ((), ())), preferred_element_type=F32)


def _dot_hi(a, b):
    return jnp.dot(a, b, preferred_element_type=F32, precision=HIGHEST)


def _split_bf16(x, terms):
    parts = []
    for _ in range(terms):
        p = x.astype(BF16)
        parts.append(p)
        x = x - p.astype(F32)
    return parts


def _dot_split_lhs(x, e, terms=2):
    return sum(_dot(p, e) for p in _split_bf16(x, terms))


def _dot_split_rhs(e, x, terms=2):
    return sum(_dot(e, p) for p in _split_bf16(x, terms))


def _dot3(a, b):
    a_hi, a_lo = _split_bf16(a, 2)
    b_hi, b_lo = _split_bf16(b, 2)
    return _dot(a_hi, b_hi) + (_dot(a_lo, b_hi) + _dot(a_hi, b_lo))


def _full(shape):
    nd = len(shape)
    return pl.BlockSpec(shape, lambda *_: (0,) * nd)


def _apply_act(y, act):
    if act is None:
        return y
    if act == "sigmoid":
        return 0.5 * jnp.tanh(0.5 * y) + 0.5
    if act == "relu2":
        return jnp.square(jnp.maximum(y, 0.0))
    raise ValueError(act)


def _mm_kernel(x_ref, w_ref, o_ref, wbf_ref, *acc, nk, act):
    k = pl.program_id(1)
    i = pl.program_id(2)

    @pl.when(i == 0)
    def _():
        wbf_ref[...] = w_ref[...].astype(BF16)

    y = _dot(x_ref[...].astype(BF16), wbf_ref[...])
    if nk == 1:
        o_ref[...] = _apply_act(y, act).astype(o_ref.dtype)
        return
    (acc_ref,) = acc
    tm = x_ref.shape[0]
    rows = pl.ds(pl.multiple_of(i * tm, tm), tm)

    @pl.when(k == 0)
    def _():
        acc_ref[rows, :] = y

    @pl.when(jnp.logical_and(k > 0, k < nk - 1))
    def _():
        acc_ref[rows, :] += y

    @pl.when(k == nk - 1)
    def _():
        o_ref[...] = _apply_act(acc_ref[rows, :] + y, act).astype(o_ref.dtype)


def matmul(x, w, *, layer=None, rows=None, tm=1024, tn=512, tk=4096, act=None, out_dtype=F32):
    m = x.shape[0] if rows is None else rows
    kdim = x.shape[1]
    n = w.shape[-1]
    assert w.shape[-2] == kdim
    tm = math.gcd(tm, m)
    tk = min(tk, kdim)
    tn = min(tn, n)
    assert m % tm == 0 and kdim % tk == 0
    ni, nk, nj = m // tm, kdim // tk, pl.cdiv(n, tn)
    if w.ndim == 3:
        w_spec = pl.BlockSpec((None, tk, tn), lambda j, k, i: (layer, k, j))
    else:
        w_spec = pl.BlockSpec((tk, tn), lambda j, k, i: (k, j))
    if nk == 1:
        o_map = lambda j, k, i: (i, j)
    else:
        o_map = lambda j, k, i: (jnp.where(k == nk - 1, i, 0), j)
    scratch = [pltpu.VMEM((tk, tn), BF16)]
    if nk > 1:
        scratch.append(pltpu.VMEM((m, tn), F32))
    return pl.pallas_call(
        functools.partial(_mm_kernel, nk=nk, act=act),
        grid=(nj, nk, ni),
        in_specs=[pl.BlockSpec((tm, tk), lambda j, k, i: (i, k)), w_spec],
        out_specs=pl.BlockSpec((tm, tn), o_map),
        out_shape=jax.ShapeDtypeStruct((m, n), out_dtype),
        scratch_shapes=scratch,
        compiler_params=_params("arbitrary", "arbitrary", "arbitrary"),
        name="matmul_ws",
    )(x, w)


def _merge_kernel(g0, g1, g2, g3, y0, y1, y2, y3, w_ref, o_ref, wbf_ref):
    @pl.when(pl.program_id(1) == 0)
    def _():
        wbf_ref[...] = w_ref[...].astype(BF16)

    acc = None
    for b, (g, y) in enumerate(((g0, y0), (g1, y1), (g2, y2), (g3, y3))):
        t = g[...].astype(F32) * _dot(y[...].astype(BF16), wbf_ref[b])
        acc = t if acc is None else acc + t
    o_ref[...] = acc.astype(o_ref.dtype)


def merge_branches(gates, ys, w_branch, layer, *, rows, tm=1024, tn=512):
    d = w_branch.shape[-1]
    bw = w_branch.shape[-2]
    tm = math.gcd(tm, rows)
    assert rows % tm == 0 and d % tn == 0
    nj = d // tn
    g_specs = [pl.BlockSpec((tm, tn), functools.partial(lambda j, i, b: (i, b * nj + j), b=b))
               for b in range(N_BRANCH)]
    y_specs = [pl.BlockSpec((tm, bw), lambda j, i: (i, 0)) for _ in range(N_BRANCH)]
    return pl.pallas_call(
        _merge_kernel,
        grid=(nj, rows // tm),
        in_specs=g_specs + y_specs + [pl.BlockSpec((None, N_BRANCH, bw, tn), lambda j, i: (layer, 0, 0, j))],
        out_specs=pl.BlockSpec((tm, tn), lambda j, i: (i, j)),
        out_shape=jax.ShapeDtypeStruct((rows, d), BF16),
        scratch_shapes=[pltpu.VMEM((N_BRANCH, bw, tn), BF16)],
        compiler_params=_params("arbitrary", "arbitrary"),
        name="merge_branches",
    )(gates, gates, gates, gates, *ys, w_branch)


def _rms(x, eps=NORM_EPS):
    return x * lax.rsqrt(jnp.mean(x * x, axis=-1, keepdims=True) + eps)


def _norm_mod_kernel(x_ref, g_ref, shift_ref, scale_ref, o_ref):
    x = x_ref[...].astype(F32)
    y = _rms(x) * g_ref[...]
    o_ref[...] = (y * (1.0 + scale_ref[...]) + shift_ref[...]).astype(o_ref.dtype)


def _group_of_rows(tm, rows_per_group, n_groups):
    return lambda i: jnp.minimum((i * tm) // rows_per_group, n_groups - 1)


def norm_modulate(x, g, mod, shift_idx, scale_idx, *, rows, rows_per_group, tm=PREP_ROWS):
    d = x.shape[1]
    tm = min(tm, rows_per_group)
    assert rows % tm == 0 and rows_per_group % tm == 0
    grp = _group_of_rows(tm, rows_per_group, mod.shape[0])
    return pl.pallas_call(
        _norm_mod_kernel,
        grid=(rows // tm,),
        in_specs=[pl.BlockSpec((tm, d), lambda i: (i, 0)),
                  pl.BlockSpec((1, d), lambda i: (0, 0)),
                  pl.BlockSpec((None, 1, d), lambda i: (grp(i), 0, shift_idx)),
                  pl.BlockSpec((None, 1, d), lambda i: (grp(i), 0, scale_idx))],
        out_specs=pl.BlockSpec((tm, d), lambda i: (i, 0)),
        out_shape=jax.ShapeDtypeStruct((rows, d), BF16),
        compiler_params=_params("arbitrary"),
        name="norm_modulate",
    )(x, g.reshape(1, d), mod, mod)


def _norm_res_kernel(x_ref, y_ref, g_ref, m_ref, *rest):
    y = _rms(y_ref[...].astype(F32)) * g_ref[...]
    x = x_ref[...] + m_ref[...] * y
    if len(rest) == 1:
        rest[0][...] = x
        return
    g2_ref, shift_ref, scale_ref, o_ref, h_ref = rest
    o_ref[...] = x
    h_ref[...] = (_rms(x) * g2_ref[...] * (1.0 + scale_ref[...]) + shift_ref[...]).astype(h_ref.dtype)


def norm_residual(x, y, g, mod, gate_idx, *, rows, rows_per_group, tm=PREP_ROWS, next_norm=None):
    d = x.shape[1]
    tm = min(tm, rows_per_group)
    assert rows % tm == 0 and rows_per_group % tm == 0
    grp = _group_of_rows(tm, rows_per_group, mod.shape[0])
    tile = pl.BlockSpec((tm, d), lambda i: (i, 0))
    vec = pl.BlockSpec((1, d), lambda i: (0, 0))
    mod_spec = lambda idx: pl.BlockSpec((None, 1, d), lambda i: (grp(i), 0, idx))
    in_specs = [tile, tile, vec, mod_spec(gate_idx)]
    args = [x, y, g.reshape(1, d), mod]
    out_specs, out_shape = tile, jax.ShapeDtypeStruct((rows, d), F32)
    if next_norm is not None:
        g2, mod2, shift_idx, scale_idx = next_norm
        in_specs += [vec, mod_spec(shift_idx), mod_spec(scale_idx)]
        args += [g2.reshape(1, d), mod2, mod2]
        out_specs, out_shape = [tile, tile], [out_shape, jax.ShapeDtypeStruct((rows, d), BF16)]
    return pl.pallas_call(
        _norm_res_kernel,
        grid=(rows // tm,),
        in_specs=in_specs,
        out_specs=out_specs,
        out_shape=out_shape,
        compiler_params=_params("arbitrary"),
        name="norm_residual",
    )(*args)


class _Rows:
    def __init__(self, batch, n_lat, n_ctx):
        self.batch, self.n_lat, self.n_ctx = batch, n_lat, n_ctx
        self.m_lat = batch * n_lat
        self.m = batch * (n_lat + n_ctx)

    def split(self, t):
        c = t.shape[-1]
        return (t[:self.m_lat].reshape(self.batch, self.n_lat, c),
                t[self.m_lat:].reshape(self.batch, self.n_ctx, c))

    def join(self, lat, ctx):
        c = lat.shape[-1]
        return jnp.concatenate([lat.reshape(self.m_lat, c), ctx.reshape(self.batch * self.n_ctx, c)], axis=0)

    def seq_edges(self, i, tm):
        r0 = i * tm
        in_lat = r0 < self.m_lat
        pos = jnp.where(in_lat, r0 % self.n_lat, (r0 - self.m_lat) % self.n_ctx)
        length = jnp.where(in_lat, self.n_lat, self.n_ctx)
        return pos == 0, pos + tm == length

    def chunk_block(self, chunk):
        ncc, nlc = self.n_ctx // chunk, self.n_lat // chunk
        ctx0 = self.m_lat // chunk

        def block(d, b, c):
            in_ctx = c < ncc
            cc = jnp.where(in_ctx, c, c - ncc)
            n_seg = jnp.where(in_ctx, ncc, nlc)
            cc = jnp.where(d == 1, n_seg - 1 - cc, cc)
            return jnp.where(in_ctx, ctx0 + b * ncc, b * nlc) + cc
        return block, ncc + nlc


def _conv3_tile(x, prev_row, next_row, w_ref, first, last):
    tm = x.shape[0]
    rid = lax.broadcasted_iota(jnp.int32, x.shape, 0)
    prev_row = jnp.where(first, 0.0, prev_row)
    next_row = jnp.where(last, 0.0, next_row)
    prev = jnp.where(rid == 0, prev_row, pltpu.roll(x, 1, 0))
    nxt = jnp.where(rid == tm - 1, next_row, pltpu.roll(x, tm - 1, 0))
    return prev * w_ref[0:1, :] + x * w_ref[1:2, :] + nxt * w_ref[2:3, :]


def _halo_specs(tm, width, col_block, n_rows):
    per = tm // SUBLANES
    last = n_rows // SUBLANES - 1
    return [pl.BlockSpec((SUBLANES, width), lambda i: (jnp.maximum(i * per - 1, 0), col_block)),
            pl.BlockSpec((SUBLANES, width), lambda i: (jnp.minimum((i + 1) * per, last), col_block))]


def _chunk_cumsum_matrix(tm, chunk, reverse):
    r = lax.broadcasted_iota(jnp.int32, (tm, tm), 0)
    c = lax.broadcasted_iota(jnp.int32, (tm, tm), 1)
    same = (r // chunk) == (c // chunk)
    tri = (r <= c) if reverse else (r >= c)
    return jnp.logical_and(same, tri).astype(BF16)


assert RWKV_CHUNK * math.exp(-0.5) < 80.0

def _rwkv_prep_kernel(rkv_ref, hp_ref, hn_ref, low_ref, shift_ref, w0_ref, a0_ref, wup_ref, aup_ref, gup_ref,
                      kk_ref, ka_ref, rk_ref, e_ref,
                      at_ref, rt_ref, bt_ref, kt_ref, bc_ref, kc_ref, gend_ref, v_ref, gate_ref, bonus_ref,
                      *, rows, chunk):
    tm = rkv_ref.shape[0]
    bw = v_ref.shape[1]
    first, last = rows.seq_edges(pl.program_id(0), tm)
    rkv = _conv3_tile(rkv_ref[...], hp_ref[SUBLANES - 1:SUBLANES, :], hn_ref[0:1, :], shift_ref, first, last)
    r, k, v = rkv[:, :bw], rkv[:, bw:2 * bw], rkv[:, 2 * bw:]
    e = e_ref[...]
    kk = k * kk_ref[...]
    kk = kk * lax.rsqrt(_dot_split_lhs(kk * kk, e) + 1e-12)
    low = low_ref[...]
    wd = low[:, :RWKV_DECAY_RANK]
    ad = low[:, RWKV_DECAY_RANK:RWKV_DECAY_RANK + RWKV_AAA_RANK]
    gd = low[:, RWKV_DECAY_RANK + RWKV_AAA_RANK:]
    w_pre = _dot3(jnp.tanh(wd), wup_ref[...])
    a_pre = _dot3(ad, aup_ref[...])
    gate_ref[...] = _dot3(jax.nn.sigmoid(gd), gup_ref[...])
    v_ref[...] = v.astype(BF16)
    kt_sum = None
    for d in range(2):
        w_log = -jax.nn.softplus(-(w0_ref[d:d + 1, :] + w_pre[:, d * bw:(d + 1) * bw])) - 0.5
        lw = -jnp.exp(w_log)
        a = jax.nn.sigmoid(a0_ref[d:d + 1, :] + a_pre[:, d * bw:(d + 1) * bw])
        kt = k * (1.0 + (a - 1.0) * ka_ref[...])
        kt_sum = kt if kt_sum is None else kt_sum + kt
        b = kk * a
        lam = _dot_split_rhs(_chunk_cumsum_matrix(tm, chunk, d == 1), lw)
        g_inv = jnp.exp(-lam)
        at_ref[d] = (-kk * jnp.exp(lam - lw)).astype(BF16)
        rt_ref[d] = (r * jnp.exp(lam)).astype(BF16)
        bt_ref[d] = (b * g_inv).astype(BF16)
        kt_ref[d] = (kt * g_inv).astype(BF16)
        for q in range(tm // chunk):
            end = q * chunk if d == 1 else (q + 1) * chunk - 1
            lam_end = lam[end:end + 1, :]
            sl = slice(q * chunk, (q + 1) * chunk)
            to_end = jnp.exp(lam_end - lam[sl])
            bc_ref[d, sl, :] = (b[sl] * to_end).astype(BF16)
            kc_ref[d, sl, :] = (kt[sl] * to_end).astype(BF16)
            gend_ref[d, q] = jnp.exp(lam_end)
    bonus_ref[...] = _dot_split_lhs(r * kt_sum * rk_ref[...], e, terms=1) * v


def rwkv_prep(pa, rows, layer, shift, w0, w_up, a0, a_up, g_up, k_k, k_a, r_k, seg_ones, *, tm=PREP_ROWS,
              chunk=RWKV_CHUNK):
    m = pa.shape[0]
    bw = k_k.shape[-1]
    low_w = RWKV_DECAY_RANK + RWKV_AAA_RANK + RWKV_GATE_RANK
    tm = min(tm, rows.n_ctx)
    assert m % tm == 0 and rows.n_ctx % tm == 0 and rows.n_lat % tm == 0 and tm % chunk == 0
    assert (3 * bw) % low_w == 0
    wup = jnp.concatenate([w_up[layer, 0], w_up[layer, 1]], axis=1)
    aup = jnp.concatenate([a_up[layer, 0], a_up[layer, 1]], axis=1)
    row = lambda t: t.reshape(1, bw)
    big = jax.ShapeDtypeStruct((2, m, bw), BF16)
    big_spec = pl.BlockSpec((2, tm, bw), lambda i: (0, i, 0))
    one_spec = pl.BlockSpec((tm, bw), lambda i: (i, 0))
    cpt = tm // chunk
    return pl.pallas_call(
        functools.partial(_rwkv_prep_kernel, rows=rows, chunk=chunk),
        grid=(m // tm,),
        in_specs=[pl.BlockSpec((tm, 3 * bw), lambda i: (i, 0))] + _halo_specs(tm, 3 * bw, 0, m)
        + [pl.BlockSpec((tm, low_w), lambda i: (i, 3 * bw // low_w)),
           _full((3, 3 * bw)), _full((2, bw)), _full((2, bw)), _full(wup.shape), _full(aup.shape),
           pl.BlockSpec((None,) + g_up.shape[1:], lambda i: (layer, 0, 0)),
           _full((1, bw)), _full((1, bw)), _full((1, bw)), _full(seg_ones.shape)],
        out_specs=[big_spec] * 6 + [pl.BlockSpec((2, cpt, 1, bw), lambda i: (0, i, 0, 0)),
                                    one_spec, one_spec, one_spec],
        out_shape=[big] * 6 + [jax.ShapeDtypeStruct((2, m // chunk, 1, bw), F32),
                               jax.ShapeDtypeStruct((m, bw), BF16), jax.ShapeDtypeStruct((m, bw), F32),
                               jax.ShapeDtypeStruct((m, bw), F32)],
        compiler_params=_params("arbitrary"),
        name="rwkv_prep",
    )(pa, pa, pa, pa, shift[layer], w0[layer], a0[layer], wup, aup, g_up, row(k_k[layer]), row(k_a[layer]),
      row(r_k[layer]), seg_ones)


def _pair_select(lo, z):
    half = z.shape[0] // 2
    return jnp.where(lo, z[:half], z[half:])


def _rwkv_scan_kernel(at_ref, rt_ref, bt_ref, kt_ref, bc_ref, kc_ref, gend_ref, v_ref, y_ref, s_ref, *, chunk):
    @pl.when(pl.program_id(2) == 0)
    def _():
        s_ref[...] = jnp.zeros_like(s_ref)

    c2 = 2 * chunk
    n_pairs = s_ref.shape[0]
    pw = 2 * RWKV_HEAD
    n_levels = chunk.bit_length() - 1
    order = lax.broadcasted_iota(jnp.int32, (chunk, chunk), 0) - lax.broadcasted_iota(jnp.int32, (chunk, chunk), 1)
    order = jnp.where(pl.program_id(0) == 1, -order, order)
    strict = order > 0
    incl = order >= 0
    lo = lax.broadcasted_iota(jnp.int32, (chunk, pw), 1) < RWKV_HEAD
    lo2 = lax.broadcasted_iota(jnp.int32, (c2, pw), 1) < RWKV_HEAD
    own = ((lax.broadcasted_iota(jnp.int32, (pw, pw), 0) < RWKV_HEAD)
           == (lax.broadcasted_iota(jnp.int32, (pw, pw), 1) < RWKV_HEAD))
    pairs = range(n_pairs)
    sl = [slice(j * pw, (j + 1) * pw) for j in pairs]

    def tri(mask, z):
        return jnp.where(mask, z, 0.0)

    a = [at_ref[:, s] for s in sl]
    r = [rt_ref[:, s] for s in sl]
    v = [v_ref[:, s] for s in sl]
    s0 = [s_ref[j] for j in pairs]
    ar = [jnp.concatenate([a[j], r[j]], axis=0) for j in pairs]
    ar_st = [jnp.concatenate([jnp.where(lo2, ar[j], 0), jnp.where(lo2, 0, ar[j])], axis=0) for j in pairs]
    p_b = [_dot_nt(ar_st[j], bt_ref[:, sl[j]]) for j in pairs]
    p_k = [_dot_nt(ar_st[j], kt_ref[:, sl[j]]) for j in pairs]
    lp = [[tri(strict, p_b[j][e * c2:e * c2 + chunk]) for e in range(2)] for j in pairs]
    l_ak = [jnp.concatenate([tri(strict, p_k[j][e * c2:e * c2 + chunk]) for e in range(2)], axis=0).astype(BF16)
            for j in pairs]
    m_rb = [jnp.concatenate([tri(incl, p_b[j][e * c2 + chunk:(e + 1) * c2]) for e in range(2)], axis=0).astype(BF16)
            for j in pairs]
    m_rk = [jnp.concatenate([tri(incl, p_k[j][e * c2 + chunk:(e + 1) * c2]) for e in range(2)], axis=0).astype(BF16)
            for j in pairs]
    x_w = [a[j].astype(F32) for j in pairs]
    x_u = [_pair_select(lo, _dot(l_ak[j], v[j])) for j in pairs]
    for lvl in range(n_levels):
        lpb = [[lp[j][e].astype(BF16) for e in range(2)] for j in pairs]
        lst = [jnp.concatenate(lpb[j], axis=0) for j in pairs]
        z = [_dot(lst[j], jnp.concatenate([x_w[j], x_u[j]], axis=1).astype(BF16)) for j in pairs]
        x_w = [x_w[j] + _pair_select(lo, z[j][:, :pw]) for j in pairs]
        x_u = [x_u[j] + _pair_select(lo, z[j][:, pw:]) for j in pairs]
        if lvl + 1 < n_levels:
            lp = [[_dot(lpb[j][e], lpb[j][e]) for e in range(2)] for j in pairs]
    s0b = [s0[j].astype(BF16) for j in pairs]
    u = [_dot_nt(x_w[j].astype(BF16), s0b[j]) + x_u[j] for j in pairs]
    ub = [u[j].astype(BF16) for j in pairs]
    y = [_dot_nt(r[j], s0b[j]) + _pair_select(lo, _dot(m_rb[j], ub[j])) + _pair_select(lo, _dot(m_rk[j], v[j]))
         for j in pairs]
    s_new = [s0[j] * gend_ref[:, sl[j]]
             + jnp.where(own, _dot_tn(ub[j], bc_ref[:, sl[j]]) + _dot_tn(v[j], kc_ref[:, sl[j]]), 0.0)
             for j in pairs]
    for j in pairs:
        y_ref[:, sl[j]] = y[j]
    for j in pairs:
        s_ref[j] = s_new[j]


def rwkv_scan(ops, v, rows, *, chunk=RWKV_CHUNK):
    at, rt, bt, kt, bc, kc, gend = ops
    _, m, bw = at.shape
    block, nc = rows.chunk_block(chunk)
    big = pl.BlockSpec((None, chunk, bw), lambda d, b, c: (d, block(d, b, c), 0))
    return pl.pallas_call(
        functools.partial(_rwkv_scan_kernel, chunk=chunk),
        grid=(2, rows.batch, nc),
        in_specs=[big] * 6 + [pl.BlockSpec((None, None, 1, bw), lambda d, b, c: (d, block(d, b, c), 0, 0)),
                              pl.BlockSpec((chunk, bw), lambda d, b, c: (block(d, b, c), 0))],
        out_specs=big,
        out_shape=jax.ShapeDtypeStruct((2, m, bw), F32),
        scratch_shapes=[pltpu.VMEM((bw // (2 * RWKV_HEAD), 2 * RWKV_HEAD, 2 * RWKV_HEAD), F32)],
        compiler_params=_params("arbitrary", "arbitrary", "arbitrary"),
        name="rwkv_scan",
    )(at, rt, bt, kt, bc, kc, gend, v)


def _rwkv_readout_kernel(y_ref, bonus_ref, gate_ref, e_ref, g_ref, b_ref, o_ref):
    y = y_ref[0] + y_ref[1]
    e = e_ref[...]
    mu = _dot_split_lhs(y, e) * (1.0 / RWKV_HEAD)
    yc = y - mu
    var = _dot_split_lhs(yc * yc, e) * (1.0 / RWKV_HEAD)
    yn = yc * lax.rsqrt(var + RWKV_LN_EPS) * g_ref[...] + b_ref[...]
    o_ref[...] = ((yn + bonus_ref[...]) * gate_ref[...]).astype(o_ref.dtype)


def rwkv_readout(y, bonus, gate, seg_ones, ln_g, ln_b, *, rows, tm=PREP_ROWS):
    bw = y.shape[-1]
    tm = math.gcd(tm, rows)
    one = pl.BlockSpec((tm, bw), lambda i: (i, 0))
    return pl.pallas_call(
        _rwkv_readout_kernel,
        grid=(rows // tm,),
        in_specs=[pl.BlockSpec((2, tm, bw), lambda i: (0, i, 0)), one, one, _full(seg_ones.shape),
                  _full((1, bw)), _full((1, bw))],
        out_specs=one,
        out_shape=jax.ShapeDtypeStruct((rows, bw), BF16),
        compiler_params=_params("arbitrary"),
        name="rwkv_readout",
    )(y, bonus, gate, seg_ones, ln_g.reshape(1, bw), ln_b.reshape(1, bw))


def _ssd_prep_kernel(x_ref, hp_ref, hn_ref, w_ref, b_ref, o_ref, *, rows):
    first, last = rows.seq_edges(pl.program_id(0), x_ref.shape[0])
    y = _conv3_tile(x_ref[...], hp_ref[SUBLANES - 1:SUBLANES, :], hn_ref[0:1, :], w_ref, first, last) + b_ref[...]
    o_ref[...] = y * jax.nn.sigmoid(y)


def ssd_prep(pc, rows, conv_w, conv_b, *, width, first_block, n_blocks, tm=PREP_ROWS):
    m = pc.shape[0]
    tm = min(tm, rows.n_ctx)
    assert m % tm == 0 and rows.n_ctx % tm == 0 and rows.n_lat % tm == 0
    per = tm // SUBLANES
    last = m // SUBLANES - 1
    return pl.pallas_call(
        functools.partial(_ssd_prep_kernel, rows=rows),
        grid=(m // tm, n_blocks),
        in_specs=[pl.BlockSpec((tm, width), lambda i, j: (i, first_block + j)),
                  pl.BlockSpec((SUBLANES, width), lambda i, j: (jnp.maximum(i * per - 1, 0), first_block + j)),
                  pl.BlockSpec((SUBLANES, width), lambda i, j: (jnp.minimum((i + 1) * per, last), first_block + j)),
                  pl.BlockSpec((3, width), lambda i, j: (0, j)),
                  pl.BlockSpec((1, width), lambda i, j: (0, j))],
        out_specs=pl.BlockSpec((tm, width), lambda i, j: (i, j)),
        out_shape=jax.ShapeDtypeStruct((m, n_blocks * width), F32),
        compiler_params=_params("arbitrary", "arbitrary"),
        name="ssd_prep",
    )(pc, pc, pc, conv_w, conv_b.reshape(1, n_blocks * width))


def _ssd_scan_kernel(x_ref, b_ref, c_ref, la_ref, lat_ref, dt_ref, y_ref, s_ref, *, chunk, heads, groups):
    @pl.when(pl.program_id(2) == 0)
    def _():
        s_ref[...] = jnp.zeros_like(s_ref)

    rev = pl.program_id(0) == 1
    pw = 2 * SSD_HEAD
    order = lax.broadcasted_iota(jnp.int32, (chunk, chunk), 0) - lax.broadcasted_iota(jnp.int32, (chunk, chunk), 1)
    order = jnp.where(rev, -order, order)
    upto = order >= 0
    lo = lax.broadcasted_iota(jnp.int32, (chunk, pw), 1) < SSD_HEAD
    acs_c_all = _dot_hi(upto.astype(F32), la_ref[...])
    acs_r_all = _dot_hi(lat_ref[...], (order <= 0).astype(F32))
    tot_c_all = jnp.where(rev, acs_c_all[0:1, :], acs_c_all[chunk - 1:chunk, :])
    dt_all = dt_ref[...]
    hpg = heads // groups
    for gi in range(groups):
        cg = c_ref[:, gi * SSD_STATE:(gi + 1) * SSD_STATE].astype(BF16)
        bg = b_ref[:, gi * SSD_STATE:(gi + 1) * SSD_STATE].astype(BF16)
        cb = _dot_nt(cg, bg)
        for qi in range(hpg // 2):
            h0 = gi * hpg + 2 * qi
            j = h0 // 2
            cols = slice(j * pw, (j + 1) * pw)
            w_st, acs_c, tot = [], [], []
            for e in range(2):
                h = h0 + e
                acs_c.append(acs_c_all[:, h:h + 1])
                tot.append(tot_c_all[:, h:h + 1])
                decay = jnp.exp(jnp.where(upto, acs_c[e] - acs_r_all[h:h + 1, :], -1e30))
                w_st.append((cb * decay).astype(BF16))
            xdt = x_ref[:, cols] * jnp.where(lo, dt_all[:, h0:h0 + 1], dt_all[:, h0 + 1:h0 + 2])
            st = s_ref[j]
            y = _pair_select(lo, _dot(jnp.concatenate(w_st, axis=0), xdt.astype(BF16)))
            y = y + _dot(cg, st.astype(BF16)) * jnp.where(lo, jnp.exp(acs_c[0]), jnp.exp(acs_c[1]))
            y_ref[:, cols] = y
            to_end = jnp.where(lo, jnp.exp(tot[0] - acs_c[0]), jnp.exp(tot[1] - acs_c[1]))
            s_ref[j] = (st * jnp.where(lo[0:1, :], jnp.exp(tot[0]), jnp.exp(tot[1]))
                        + _dot_tn(bg, (xdt * to_end).astype(BF16)))


def ssd_scan(xbc, la, dt, rows, *, heads, chunk=SSD_CHUNK):
    m = xbc.shape[0]
    bw = heads * SSD_HEAD
    gn = SSD_GROUPS * SSD_STATE
    assert bw % gn == 0
    block, nc = rows.chunk_block(chunk)
    lat = jnp.swapaxes(la, 1, 2)
    thin = pl.BlockSpec((None, chunk, heads), lambda d, b, c: (d, block(d, b, c), 0))
    return pl.pallas_call(
        functools.partial(_ssd_scan_kernel, chunk=chunk, heads=heads, groups=SSD_GROUPS),
        grid=(2, rows.batch, nc),
        in_specs=[pl.BlockSpec((chunk, bw), lambda d, b, c: (block(d, b, c), 0)),
                  pl.BlockSpec((chunk, gn), lambda d, b, c: (block(d, b, c), bw // gn)),
                  pl.BlockSpec((chunk, gn), lambda d, b, c: (block(d, b, c), bw // gn + 1)),
                  thin,
                  pl.BlockSpec((None, heads, chunk), lambda d, b, c: (d, 0, block(d, b, c))),
                  thin],
        out_specs=pl.BlockSpec((None, chunk, bw), lambda d, b, c: (d, block(d, b, c), 0)),
        out_shape=jax.ShapeDtypeStruct((2, m, bw), F32),
        scratch_shapes=[pltpu.VMEM((heads // 2, SSD_STATE, 2 * SSD_HEAD), F32)],
        compiler_params=_params("arbitrary", "arbitrary", "arbitrary"),
        name="ssd_scan",
    )(xbc, xbc, xbc, la, lat, dt)


def _ssd_readout_kernel(y_ref, x_ref, z_ref, d_ref, g_ref, o_ref):
    y = y_ref[0] + y_ref[1] + d_ref[...] * x_ref[...]
    z = z_ref[...]
    o_ref[...] = (_rms(y * (z * jax.nn.sigmoid(z))) * g_ref[...]).astype(o_ref.dtype)


def ssd_readout(y, xbc, pc, d_skip, norm_g, *, rows, tm=PREP_ROWS):
    bw = y.shape[-1]
    tm = math.gcd(tm, rows)
    one = pl.BlockSpec((tm, bw), lambda i: (i, 0))
    return pl.pallas_call(
        _ssd_readout_kernel,
        grid=(rows // tm,),
        in_specs=[pl.BlockSpec((2, tm, bw), lambda i: (0, i, 0)), one, one, _full((1, bw)), _full((1, bw))],
        out_specs=one,
        out_shape=jax.ShapeDtypeStruct((rows, bw), BF16),
        compiler_params=_params("arbitrary"),
        name="ssd_readout",
    )(y, xbc, pc, jnp.repeat(d_skip, SSD_HEAD).reshape(1, bw), norm_g.reshape(1, bw))


def _softmax_pv(scores, values):
    m = None
    for s in scores:
        ms = jnp.max(s, axis=-1, keepdims=True)
        m = ms if m is None else jnp.maximum(m, ms)
    l, o = None, None
    for s, v in zip(scores, values):
        p = jnp.exp(s - m)
        ls = jnp.sum(p, axis=-1, keepdims=True)
        os_ = _dot(p.astype(BF16), v)
        l = ls if l is None else l + ls
        o = os_ if o is None else o + os_
    return o / l


def _attn_kernel(lam_ref, q_ref, g_ref, *refs, n_seg, scale, diff, out_scale, row_parts):
    seg_refs = refs[:2 * n_seg]
    o_ref = refs[2 * n_seg]
    values = [seg_refs[2 * si + 1][...] for si in range(n_seg)]
    rows_per = q_ref.shape[0] // row_parts
    for r in range(row_parts):
        rs = slice(r * rows_per, (r + 1) * rows_per)
        q = q_ref[rs, :].astype(F32) * scale
        if not diff:
            qb = q.astype(BF16)
            scores = [_dot_nt(qb, seg_refs[2 * si][...]) for si in range(n_seg)]
            o_ref[rs, :] = _softmax_pv(scores, values).astype(o_ref.dtype)
            continue
        first = lax.broadcasted_iota(jnp.int32, q.shape, 1) < DIFF_HEAD
        q1 = jnp.where(first, q, 0.0).astype(BF16)
        q2 = jnp.where(first, 0.0, q).astype(BF16)
        o1 = _softmax_pv([_dot_nt(q1, seg_refs[2 * si][...]) for si in range(n_seg)], values)
        o2 = _softmax_pv([_dot_nt(q2, seg_refs[2 * si][...]) for si in range(n_seg)], values)
        o = o1 - lam_ref[0] * o2
        o_ref[rs, :] = (_rms(o) * g_ref[...] * out_scale).astype(o_ref.dtype)


def attention(q, k, v, *, heads, batch, q_rows, q_first, segments, scale, tq=256, row_parts=1,
              diff_lam=None, diff_gain=None, out_scale=1.0):
    dqk = q.shape[1] // heads
    dv = v.shape[1] // heads
    tq = min(tq, q_rows)
    nq = q_rows // tq
    if tq % (row_parts * 2 * SUBLANES):
        row_parts = 1
    diff = diff_lam is not None
    lam = (diff_lam if diff else jnp.zeros((), F32)).reshape(1).astype(F32)
    gain = (diff_gain if diff else jnp.ones((dv,), F32)).reshape(1, dv).astype(F32)
    seg_specs = []
    for seg_rows, first in segments:
        for w in (dqk, dv):
            seg_specs.append(pl.BlockSpec((seg_rows, w), functools.partial(lambda b, h, i, f: (f + b, h), f=first)))
    return pl.pallas_call(
        functools.partial(_attn_kernel, n_seg=len(segments), scale=scale, diff=diff, out_scale=out_scale,
                          row_parts=row_parts),
        grid=(batch, heads, nq),
        in_specs=[pl.BlockSpec(memory_space=pltpu.SMEM),
                  pl.BlockSpec((tq, dqk), lambda b, h, i: ((q_first + b) * nq + i, h)),
                  pl.BlockSpec((1, dv), lambda b, h, i: (0, 0))] + seg_specs,
        out_specs=pl.BlockSpec((tq, dv), lambda b, h, i: (b * nq + i, h)),
        out_shape=jax.ShapeDtypeStruct((batch * q_rows, heads * dv), BF16),
        compiler_params=_params("arbitrary", "arbitrary", "arbitrary"),
        name="diff_attention" if diff else "mla_attention",
    )(lam, q, gain, *([k, v] * len(segments)))


ROPE_GROUP = 64
ROPE_QUARTER = ROPE_GROUP // 4


def _axial_rope_tables(n_tok):
    t = jnp.arange(n_tok)
    row = (t // GRID_W).astype(F32)
    col = (t % GRID_W).astype(F32)
    axis_dim = ROPE_GROUP // 2
    inv = 1.0 / (ROPE_BASE ** (jnp.arange(0, axis_dim, 2, dtype=F32) / axis_dim))
    ar = row[:, None] * inv[None]
    ac = col[:, None] * inv[None]
    ang = jnp.concatenate([ar, ar, ac, ac], axis=-1)
    sign = jnp.where((jnp.arange(ROPE_GROUP) % (2 * ROPE_QUARTER)) < ROPE_QUARTER, -1.0, 1.0)
    return jnp.cos(ang), jnp.sin(ang) * sign


def _rope_lanes(x, cos, sin_signed):
    lane = lax.broadcasted_iota(jnp.int32, x.shape, 1)
    takes_next = (lane % (2 * ROPE_QUARTER)) < ROPE_QUARTER
    rot = jnp.where(takes_next, pltpu.roll(x, LANES - ROPE_QUARTER, 1), pltpu.roll(x, ROPE_QUARTER, 1))
    return x * cos + rot * sin_signed


def _rope_table_spec(rows, tm):
    per_seq = rows.n_lat // tm
    return pl.BlockSpec((tm, LANES), lambda i: (jnp.where(i * tm < rows.m_lat, i % per_seq, 0), 0))


def _tile_tables(i, tm, rows, cos_ref, sin_ref):
    is_lat = i * tm < rows.m_lat
    return jnp.where(is_lat, cos_ref[...], 1.0), jnp.where(is_lat, sin_ref[...], 0.0)


def _diff_prep_kernel(q_ref, k_ref, v_ref, cos_ref, sin_ref, qo_ref, ko_ref, vo_ref, *, rows):
    tm = q_ref.shape[0]
    cos, sin = _tile_tables(pl.program_id(0), tm, rows, cos_ref, sin_ref)
    for src, dst in ((q_ref, qo_ref), (k_ref, ko_ref)):
        for cb in range(src.shape[1] // LANES):
            cols = slice(cb * LANES, (cb + 1) * LANES)
            dst[:, cols] = _rope_lanes(src[:, cols], cos, sin).astype(dst.dtype)
    vo_ref[...] = v_ref[...].astype(vo_ref.dtype)


def diff_prep(pd, rows, cos2, sin2, *, tm=PREP_ROWS):
    m = pd.shape[0]
    bw = pd.shape[1] // 3
    tm = min(tm, rows.n_ctx)
    assert m % tm == 0 and rows.n_ctx % tm == 0 and rows.n_lat % tm == 0
    out = jax.ShapeDtypeStruct((m, bw), BF16)
    one = pl.BlockSpec((tm, bw), lambda i: (i, 0))
    tab = _rope_table_spec(rows, tm)
    return pl.pallas_call(
        functools.partial(_diff_prep_kernel, rows=rows),
        grid=(m // tm,),
        in_specs=[pl.BlockSpec((tm, bw), functools.partial(lambda i, j: (i, j), j=j)) for j in range(3)] + [tab, tab],
        out_specs=[one] * 3,
        out_shape=[out] * 3,
        compiler_params=_params("arbitrary"),
        name="diff_prep",
    )(pd, pd, pd, cos2, sin2)


def _mla_q_kernel(cq_ref, g_ref, w_ref, cos_ref, sin_ref, o_ref, *, rows):
    tm = cq_ref.shape[0]
    cos, sin = _tile_tables(pl.program_id(0), tm, rows, cos_ref, sin_ref)
    q = _dot((_rms(cq_ref[...]) * g_ref[...]).astype(BF16), w_ref[...])
    for h in range(MLA_HEADS):
        nope = slice(2 * h * LANES, (2 * h + 1) * LANES)
        rope = slice((2 * h + 1) * LANES, (2 * h + 2) * LANES)
        o_ref[:, nope] = q[:, nope].astype(o_ref.dtype)
        o_ref[:, rope] = _rope_lanes(q[:, rope], cos, sin).astype(o_ref.dtype)


def mla_q_prep(pb, rows, q_norm, w_uq, cos2, sin2, *, tm=PREP_ROWS):
    m = pb.shape[0]
    tm = min(tm, rows.n_ctx)
    w = w_uq.reshape(MLA_Q_RANK, MLA_HEADS, MLA_NOPE + MLA_ROPE)
    w = jnp.pad(w, ((0, 0), (0, 0), (0, 2 * LANES - MLA_NOPE - MLA_ROPE))).reshape(MLA_Q_RANK, -1).astype(BF16)
    tab = _rope_table_spec(rows, tm)
    return pl.pallas_call(
        functools.partial(_mla_q_kernel, rows=rows),
        grid=(m // tm,),
        in_specs=[pl.BlockSpec((tm, MLA_Q_RANK), lambda i: (i, 0)), _full((1, MLA_Q_RANK)), _full(w.shape), tab, tab],
        out_specs=pl.BlockSpec((tm, w.shape[1]), lambda i: (i, 0)),
        out_shape=jax.ShapeDtypeStruct((m, w.shape[1]), BF16),
        compiler_params=_params("arbitrary"),
        name="mla_q_prep",
    )(pb, q_norm.reshape(1, MLA_Q_RANK), w, cos2, sin2)


def _mla_kv_kernel(ckv_ref, kr_ref, g_ref, w_ref, cos_ref, sin_ref, k_ref, v_ref, *, rows):
    tm = ckv_ref.shape[0]
    cos, sin = _tile_tables(pl.program_id(0), tm, rows, cos_ref, sin_ref)
    kv = _dot((_rms(ckv_ref[...]) * g_ref[...]).astype(BF16), w_ref[...])
    lane = lax.broadcasted_iota(jnp.int32, (tm, LANES), 1)
    kr = jnp.where(lane < MLA_ROPE, _rope_lanes(kr_ref[...], cos, sin), 0.0).astype(k_ref.dtype)
    nope_w = MLA_HEADS * MLA_NOPE
    for h in range(MLA_HEADS):
        k_ref[:, 2 * h * LANES:(2 * h + 1) * LANES] = kv[:, h * MLA_NOPE:(h + 1) * MLA_NOPE].astype(k_ref.dtype)
        k_ref[:, (2 * h + 1) * LANES:(2 * h + 2) * LANES] = kr
    v_ref[...] = kv[:, nope_w:].astype(v_ref.dtype)


def mla_kv_prep(pb, rows, kv_norm, w_ukv, cos2, sin2, *, tm=PREP_ROWS):
    m = pb.shape[0]
    tm = min(tm, rows.n_ctx)
    assert MLA_NOPE == LANES and MLA_Q_RANK % MLA_KV_RANK == 0 and (MLA_Q_RANK + MLA_KV_RANK) % LANES == 0
    w = w_ukv.reshape(MLA_KV_RANK, MLA_HEADS, 2 * MLA_NOPE)
    w = jnp.concatenate([w[:, :, :MLA_NOPE].reshape(MLA_KV_RANK, -1), w[:, :, MLA_NOPE:].reshape(MLA_KV_RANK, -1)],
                        axis=1).astype(BF16)
    nope_w = MLA_HEADS * MLA_NOPE
    tab = _rope_table_spec(rows, tm)
    return pl.pallas_call(
        functools.partial(_mla_kv_kernel, rows=rows),
        grid=(m // tm,),
        in_specs=[pl.BlockSpec((tm, MLA_KV_RANK), lambda i: (i, MLA_Q_RANK // MLA_KV_RANK)),
                  pl.BlockSpec((tm, LANES), lambda i: (i, (MLA_Q_RANK + MLA_KV_RANK) // LANES)),
                  _full((1, MLA_KV_RANK)), _full(w.shape), tab, tab],
        out_specs=[pl.BlockSpec((tm, 2 * nope_w), lambda i: (i, 0)), pl.BlockSpec((tm, nope_w), lambda i: (i, 0))],
        out_shape=[jax.ShapeDtypeStruct((m, 2 * nope_w), BF16), jax.ShapeDtypeStruct((m, nope_w), BF16)],
        compiler_params=_params("arbitrary"),
        name="mla_kv_prep",
    )(pb, pb, kv_norm.reshape(1, MLA_KV_RANK), w, cos2, sin2)


def kernel(x, c, ctx, c_ctx, ada_w, ada_b, norm_mix_pre, norm_mix_post, norm_ffn_pre, norm_ffn_post, w_in, rwkv_shift, rwkv_w0, rwkv_w_up, rwkv_a0, rwkv_a_up, rwkv_g_up, rwkv_k_k, rwkv_k_a, rwkv_r_k, rwkv_ln_g, rwkv_ln_b, mla_q_norm, mla_w_uq, mla_kv_norm, mla_w_ukv, ssd_conv_w, ssd_conv_b, ssd_dt_bias, ssd_a_log, ssd_d, ssd_norm, diff_lq1, diff_lk1, diff_lq2, diff_lk2, diff_subln, w_branch, w_gate, w_out, w_ff1, w_ff2):
    batch, n_lat, d = x.shape
    n_ctx = ctx.shape[1]
    depth = ada_w.shape[0]
    bw = d // N_BRANCH
    rows = _Rows(batch, n_lat, n_ctx)
    m_lat, m_all = rows.m_lat, rows.m
    ssd_heads = bw // SSD_HEAD
    gn = SSD_GROUPS * SSD_STATE
    rwkv_in = 3 * bw + RWKV_DECAY_RANK + RWKV_AAA_RANK + RWKV_GATE_RANK
    mla_in = MLA_Q_RANK + MLA_KV_RANK + MLA_ROPE
    ssd_in = 2 * bw + 2 * gn + ssd_heads
    diff_in = 3 * bw
    o_mla, o_ssd, o_diff = rwkv_in, rwkv_in + mla_in, rwkv_in + mla_in + ssd_in
    assert w_in.shape[-1] == o_diff + diff_in
    assert n_lat % SSD_CHUNK == 0 and n_ctx % SSD_CHUNK == 0

    assert MLA_ROPE == ROPE_GROUP and DIFF_HEAD == ROPE_GROUP
    cos2, sin2 = (jnp.tile(t, (1, LANES // ROPE_GROUP)) for t in _axial_rope_tables(n_lat))
    hid = jnp.arange(bw) // RWKV_HEAD
    seg_ones = (hid[:, None] == hid[None, :]).astype(BF16)

    xs = jnp.concatenate([x.reshape(m_lat, d), ctx.reshape(batch * n_ctx, d)], axis=0)
    cvec = jnp.zeros((8, d), F32).at[:batch].set(jax.nn.silu(c)).at[batch].set(jax.nn.silu(c_ctx))
    groups = batch + 1

    mods = [(matmul(cvec, ada_w, layer=l, tn=512) + ada_b[l])[:groups].reshape(groups, 1, 6 * d)
            for l in range(depth)]
    h = norm_modulate(xs, norm_mix_pre[0], mods[0], 0, 1, rows=m_all, rows_per_group=n_lat)

    for l in range(depth):
        ctx_out = l < depth - 1
        m_out = m_all if ctx_out else m_lat
        mod = mods[l]

        wide = dict(tm=512, tn=1024) if l == depth - 1 else {}
        pa = matmul(h, w_in[l, :, :o_mla])
        pb = matmul(h, w_in[l, :, o_mla:o_ssd])
        pc = matmul(h, w_in[l, :, o_ssd:o_diff])
        pd = matmul(h, w_in[l, :, o_diff:])

        prep = rwkv_prep(pa, rows, l, rwkv_shift, rwkv_w0, rwkv_w_up, rwkv_a0, rwkv_a_up, rwkv_g_up,
                         rwkv_k_k, rwkv_k_a, rwkv_r_k.reshape(depth, bw), seg_ones)
        y_rwkv = rwkv_scan(prep[:7], prep[7], rows)
        ya = rwkv_readout(y_rwkv, prep[9], prep[8], seg_ones, rwkv_ln_g[l], rwkv_ln_b[l], rows=m_out)

        q_m = mla_q_prep(pb, rows, mla_q_norm[l], mla_w_uq[l], cos2, sin2)
        k_m, vv = mla_kv_prep(pb, rows, mla_kv_norm[l], mla_w_ukv[l], cos2, sin2)
        ctx_blk = m_lat // n_ctx
        segs_lat = [(n_lat, 0), (n_ctx, ctx_blk)]
        mla_scale = (MLA_NOPE + MLA_ROPE) ** -0.5
        yb = attention(q_m, k_m, vv, heads=MLA_HEADS, batch=batch, q_rows=n_lat, q_first=0, segments=segs_lat,
                       scale=mla_scale, tq=512, row_parts=2)
        if ctx_out:
            yb_c = attention(q_m, k_m, vv, heads=MLA_HEADS, batch=batch, q_rows=n_ctx, q_first=ctx_blk,
                             segments=[(n_ctx, ctx_blk)], scale=mla_scale)
            yb = jnp.concatenate([yb, yb_c], axis=0)

        assert (2 * gn) % bw == 0
        xbc = ssd_prep(pc, rows, ssd_conv_w[l], ssd_conv_b[l], width=bw, first_block=1, n_blocks=1 + 2 * gn // bw)
        dt_raw = pc[:, 2 * bw + 2 * gn:2 * bw + 2 * gn + ssd_heads]
        dts = jnp.stack([jax.nn.softplus(dt_raw + ssd_dt_bias[l, dr]) for dr in range(2)])
        las = dts * (-jnp.exp(ssd_a_log[l]))[:, None, :]
        y_ssd = ssd_scan(xbc, las, dts, rows, heads=ssd_heads)
        yc = ssd_readout(y_ssd, xbc, pc, ssd_d[l], ssd_norm[l], rows=m_out)

        q_d, k_d, v_d = diff_prep(pd, rows, cos2, sin2)
        lam_init = 0.8 - 0.6 * math.exp(-0.3 * l)
        lam = (jnp.exp(jnp.sum(diff_lq1[l] * diff_lk1[l])) - jnp.exp(jnp.sum(diff_lq2[l] * diff_lk2[l])) + lam_init)
        diff_kw = dict(heads=DIFF_HEADS, batch=batch, scale=DIFF_HEAD ** -0.5, diff_lam=lam,
                       diff_gain=diff_subln[l], out_scale=1.0 - lam_init)
        yd = attention(q_d, k_d, v_d, q_rows=n_lat, q_first=0, segments=segs_lat, tq=1024, row_parts=2, **diff_kw)
        if ctx_out:
            yd_c = attention(q_d, k_d, v_d, q_rows=n_ctx, q_first=ctx_blk, segments=[(n_ctx, ctx_blk)], **diff_kw)
            yd = jnp.concatenate([yd, yd_c], axis=0)

        gates = matmul(h, w_gate, layer=l, rows=m_out, act="sigmoid", out_dtype=BF16)
        merged = merge_branches(gates, [ya, yb, yc, yd], w_branch, l, rows=m_out)
        mix = matmul(merged, w_out, layer=l, **wide)
        xs, hf = norm_residual(xs, mix, norm_mix_post[l], mod, 2, rows=m_out, rows_per_group=n_lat,
                               next_norm=(norm_ffn_pre[l], mod, 3, 4))
        f1 = matmul(hf, w_ff1, layer=l, act="relu2", out_dtype=BF16, **wide)
        f2 = matmul(f1, w_ff2, layer=l, tm=512)
        if ctx_out:
            xs, h = norm_residual(xs, f2, norm_ffn_post[l], mod, 5, rows=m_out, rows_per_group=n_lat,
                                  next_norm=(norm_mix_pre[l + 1], mods[l + 1], 0, 1))
        else:
            xs = norm_residual(xs, f2, norm_ffn_post[l], mod, 5, rows=m_out, rows_per_group=n_lat)

    return xs[:m_lat].reshape(batch, n_lat, d)
```

```python
import functools
import math

import jax
import jax.numpy as jnp
from jax import lax
from jax.experimental import pallas as pl
from jax.experimental.pallas import tpu as pltpu

F32 = jnp.float32
BF16 = jnp.bfloat16
HIGHEST = lax.Precision.HIGHEST

NORM_EPS = 1e-6
ROPE_BASE = 10000.0
GRID_W = 64
N_BRANCH = 4
LANES = 128
SUBLANES = 8

RWKV_HEAD = 64
RWKV_DECAY_RANK = 64
RWKV_AAA_RANK = 64
RWKV_GATE_RANK = 128
RWKV_LN_EPS = 64e-5
RWKV_CHUNK = 64

MLA_HEADS = 8
MLA_NOPE = 128
MLA_ROPE = 64
MLA_Q_RANK = 768
MLA_KV_RANK = 256

SSD_HEAD = 64
SSD_GROUPS = 4
SSD_STATE = 128
SSD_CHUNK = 128

DIFF_HEADS = 8
DIFF_HEAD = 64

PREP_ROWS = 256

VMEM_LIMIT_BYTES = 56 * 1024 * 1024


def _params(*sem):
    return pltpu.CompilerParams(dimension_semantics=sem, vmem_limit_bytes=VMEM_LIMIT_BYTES)


def _dot(a, b):
    return jnp.dot(a, b, preferred_element_type=F32)


def _dot_nt(a, b):
    return lax.dot_general(a, b, (((1,), (1,)), ((), ())), preferred_element_type=F32)


def _dot_tn(a, b):
    return lax.dot_general(a, b, (((0,), (0,)), ((), ())), preferred_element_type=F32)


def _dot_hi(a, b):
    return jnp.dot(a, b, preferred_element_type=F32, precision=HIGHEST)


def _split_bf16(x, terms):
    parts = []
    for _ in range(terms):
        p = x.astype(BF16)
        parts.append(p)
        x = x - p.astype(F32)
    return parts


def _dot_split_lhs(x, e, terms=2):
    return sum(_dot(p, e) for p in _split_bf16(x, terms))


def _dot_split_rhs(e, x, terms=2):
    return sum(_dot(e, p) for p in _split_bf16(x, terms))


def _dot3(a, b):
    a_hi, a_lo = _split_bf16(a, 2)
    b_hi, b_lo = _split_bf16(b, 2)
    return _dot(a_hi, b_hi) + (_dot(a_lo, b_hi) + _dot(a_hi, b_lo))


def _full(shape):
    nd = len(shape)
    return pl.BlockSpec(shape, lambda *_: (0,) * nd)


def _apply_act(y, act):
    if act is None:
        return y
    if act == "sigmoid":
        return 0.5 * jnp.tanh(0.5 * y) + 0.5
    if act == "relu2":
        return jnp.square(jnp.maximum(y, 0.0))
    raise ValueError(act)


def _mm_kernel(x_ref, w_ref, o_ref, wbf_ref, *acc, nk, act):
    k = pl.program_id(1)
    i = pl.program_id(2)

    @pl.when(i == 0)
    def _():
        wbf_ref[...] = w_ref[...].astype(BF16)

    y = _dot(x_ref[...].astype(BF16), wbf_ref[...])
    if nk == 1:
        o_ref[...] = _apply_act(y, act).astype(o_ref.dtype)
        return
    (acc_ref,) = acc
    tm = x_ref.shape[0]
    rows = pl.ds(pl.multiple_of(i * tm, tm), tm)

    @pl.when(k == 0)
    def _():
        acc_ref[rows, :] = y

    @pl.when(jnp.logical_and(k > 0, k < nk - 1))
    def _():
        acc_ref[rows, :] += y

    @pl.when(k == nk - 1)
    def _():
        o_ref[...] = _apply_act(acc_ref[rows, :] + y, act).astype(o_ref.dtype)


def matmul(x, w, *, layer=None, rows=None, n=None, tm=512, tn=1024, tk=4096, act=None, out_dtype=F32):
    m = x.shape[0] if rows is None else rows
    kdim = x.shape[1]
    n = w.shape[-1] if n is None else n
    assert w.shape[-2] == kdim
    tm = math.gcd(tm, m)
    tk = min(tk, kdim)
    tn = min(tn, n)
    assert m % tm == 0 and kdim % tk == 0
    ni, nk, nj = m // tm, kdim // tk, pl.cdiv(n, tn)
    if w.ndim == 3:
        w_spec = pl.BlockSpec((None, tk, tn), lambda j, k, i: (layer, k, j))
    else:
        w_spec = pl.BlockSpec((tk, tn), lambda j, k, i: (k, j))
    if nk == 1:
        o_map = lambda j, k, i: (i, j)
    else:
        o_map = lambda j, k, i: (jnp.where(k == nk - 1, i, 0), j)
    scratch = [pltpu.VMEM((tk, tn), BF16)]
    if nk > 1:
        scratch.append(pltpu.VMEM((m, tn), F32))
    return pl.pallas_call(
        functools.partial(_mm_kernel, nk=nk, act=act),
        grid=(nj, nk, ni),
        in_specs=[pl.BlockSpec((tm, tk), lambda j, k, i: (i, k)), w_spec],
        out_specs=pl.BlockSpec((tm, tn), o_map),
        out_shape=jax.ShapeDtypeStruct((m, n), out_dtype),
        scratch_shapes=scratch,
        compiler_params=_params("arbitrary", "arbitrary", "arbitrary"),
        name="matmul_ws",
    )(x, w)


def _merge_kernel(g0, g1, g2, g3, y0, y1, y2, y3, w_ref, o_ref, wbf_ref):
    @pl.when(pl.program_id(1) == 0)
    def _():
        wbf_ref[...] = w_ref[...].astype(BF16)

    acc = None
    for b, (g, y) in enumerate(((g0, y0), (g1, y1), (g2, y2), (g3, y3))):
        t = g[...].astype(F32) * _dot(y[...].astype(BF16), wbf_ref[b])
        acc = t if acc is None else acc + t
    o_ref[...] = acc.astype(o_ref.dtype)


def merge_branches(gates, ys, w_branch, layer, *, rows, tm=1024, tn=512):
    d = w_branch.shape[-1]
    bw = w_branch.shape[-2]
    tm = math.gcd(tm, rows)
    assert rows % tm == 0 and d % tn == 0
    nj = d // tn
    g_specs = [pl.BlockSpec((tm, tn), functools.partial(lambda j, i, b: (i, b * nj + j), b=b))
               for b in range(N_BRANCH)]
    y_specs = [pl.BlockSpec((tm, bw), lambda j, i: (i, 0)) for _ in range(N_BRANCH)]
    return pl.pallas_call(
        _merge_kernel,
        grid=(nj, rows // tm),
        in_specs=g_specs + y_specs + [pl.BlockSpec((None, N_BRANCH, bw, tn), lambda j, i: (layer, 0, 0, j))],
        out_specs=pl.BlockSpec((tm, tn), lambda j, i: (i, j)),
        out_shape=jax.ShapeDtypeStruct((rows, d), BF16),
        scratch_shapes=[pltpu.VMEM((N_BRANCH, bw, tn), BF16)],
        compiler_params=_params("arbitrary", "arbitrary"),
        name="merge_branches",
    )(gates, gates, gates, gates, *ys, w_branch)


def _rms(x, eps=NORM_EPS):
    return x * lax.rsqrt(jnp.mean(x * x, axis=-1, keepdims=True) + eps)


def _norm_mod_kernel(x_ref, g_ref, shift_ref, scale_ref, o_ref):
    x = x_ref[...].astype(F32)
    y = _rms(x) * g_ref[...]
    o_ref[...] = (y * (1.0 + scale_ref[...]) + shift_ref[...]).astype(o_ref.dtype)


def _group_of_rows(tm, rows_per_group, n_groups):
    return lambda i: jnp.minimum((i * tm) // rows_per_group, n_groups - 1)


def norm_modulate(x, g, mod, shift_idx, scale_idx, *, rows, rows_per_group, tm=PREP_ROWS):
    d = x.shape[1]
    tm = min(tm, rows_per_group)
    assert rows % tm == 0 and rows_per_group % tm == 0
    grp = _group_of_rows(tm, rows_per_group, mod.shape[0])
    return pl.pallas_call(
        _norm_mod_kernel,
        grid=(rows // tm,),
        in_specs=[pl.BlockSpec((tm, d), lambda i: (i, 0)),
                  pl.BlockSpec((1, d), lambda i: (0, 0)),
                  pl.BlockSpec((None, 1, d), lambda i: (grp(i), 0, shift_idx)),
                  pl.BlockSpec((None, 1, d), lambda i: (grp(i), 0, scale_idx))],
        out_specs=pl.BlockSpec((tm, d), lambda i: (i, 0)),
        out_shape=jax.ShapeDtypeStruct((rows, d), BF16),
        compiler_params=_params("arbitrary"),
        name="norm_modulate",
    )(x, g.reshape(1, d), mod, mod)


def _norm_res_kernel(x_ref, y_ref, g_ref, m_ref, *rest):
    y = _rms(y_ref[...].astype(F32)) * g_ref[...]
    x = x_ref[...] + m_ref[...] * y
    if len(rest) == 1:
        rest[0][...] = x
        return
    g2_ref, shift_ref, scale_ref, o_ref, h_ref = rest
    o_ref[...] = x
    h_ref[...] = (_rms(x) * g2_ref[...] * (1.0 + scale_ref[...]) + shift_ref[...]).astype(h_ref.dtype)


def norm_residual(x, y, g, mod, gate_idx, *, rows, rows_per_group, tm=PREP_ROWS, next_norm=None):
    d = x.shape[1]
    tm = min(tm, rows_per_group)
    assert rows % tm == 0 and rows_per_group % tm == 0
    grp = _group_of_rows(tm, rows_per_group, mod.shape[0])
    tile = pl.BlockSpec((tm, d), lambda i: (i, 0))
    vec = pl.BlockSpec((1, d), lambda i: (0, 0))
    mod_spec = lambda idx: pl.BlockSpec((None, 1, d), lambda i: (grp(i), 0, idx))
    in_specs = [tile, tile, vec, mod_spec(gate_idx)]
    args = [x, y, g.reshape(1, d), mod]
    out_specs, out_shape = tile, jax.ShapeDtypeStruct((rows, d), F32)
    if next_norm is not None:
        g2, mod2, shift_idx, scale_idx = next_norm
        in_specs += [vec, mod_spec(shift_idx), mod_spec(scale_idx)]
        args += [g2.reshape(1, d), mod2, mod2]
        out_specs, out_shape = [tile, tile], [out_shape, jax.ShapeDtypeStruct((rows, d), BF16)]
    return pl.pallas_call(
        _norm_res_kernel,
        grid=(rows // tm,),
        in_specs=in_specs,
        out_specs=out_specs,
        out_shape=out_shape,
        compiler_params=_params("arbitrary"),
        name="norm_residual",
    )(*args)


class _Rows:
    def __init__(self, batch, n_lat, n_ctx):
        self.batch, self.n_lat, self.n_ctx = batch, n_lat, n_ctx
        self.m_lat = batch * n_lat
        self.m = batch * (n_lat + n_ctx)

    def split(self, t):
        c = t.shape[-1]
        return (t[:self.m_lat].reshape(self.batch, self.n_lat, c),
                t[self.m_lat:].reshape(self.batch, self.n_ctx, c))

    def join(self, lat, ctx):
        c = lat.shape[-1]
        return jnp.concatenate([lat.reshape(self.m_lat, c), ctx.reshape(self.batch * self.n_ctx, c)], axis=0)

    def seq_edges(self, i, tm):
        r0 = i * tm
        in_lat = r0 < self.m_lat
        pos = jnp.where(in_lat, r0 % self.n_lat, (r0 - self.m_lat) % self.n_ctx)
        length = jnp.where(in_lat, self.n_lat, self.n_ctx)
        return pos == 0, pos + tm == length

    def chunk_block(self, chunk):
        ncc, nlc = self.n_ctx // chunk, self.n_lat // chunk
        ctx0 = self.m_lat // chunk

        def block(d, b, c):
            in_ctx = c < ncc
            cc = jnp.where(in_ctx, c, c - ncc)
            n_seg = jnp.where(in_ctx, ncc, nlc)
            cc = jnp.where(d == 1, n_seg - 1 - cc, cc)
            return jnp.where(in_ctx, ctx0 + b * ncc, b * nlc) + cc
        return block, ncc + nlc


def _conv3_tile(x, prev_row, next_row, w_ref, first, last):
    tm = x.shape[0]
    rid = lax.broadcasted_iota(jnp.int32, x.shape, 0)
    prev_row = jnp.where(first, 0.0, prev_row)
    next_row = jnp.where(last, 0.0, next_row)
    prev = jnp.where(rid == 0, prev_row, pltpu.roll(x, 1, 0))
    nxt = jnp.where(rid == tm - 1, next_row, pltpu.roll(x, tm - 1, 0))
    return prev * w_ref[0:1, :] + x * w_ref[1:2, :] + nxt * w_ref[2:3, :]


def _halo_specs(tm, width, col_block, n_rows):
    per = tm // SUBLANES
    last = n_rows // SUBLANES - 1
    return [pl.BlockSpec((SUBLANES, width), lambda i: (jnp.maximum(i * per - 1, 0), col_block)),
            pl.BlockSpec((SUBLANES, width), lambda i: (jnp.minimum((i + 1) * per, last), col_block))]


def _chunk_cumsum_matrix(tm, chunk, reverse):
    r = lax.broadcasted_iota(jnp.int32, (tm, tm), 0)
    c = lax.broadcasted_iota(jnp.int32, (tm, tm), 1)
    same = (r // chunk) == (c // chunk)
    tri = (r <= c) if reverse else (r >= c)
    return jnp.logical_and(same, tri).astype(BF16)


assert RWKV_CHUNK * math.exp(-0.5) < 80.0

def _rwkv_prep_kernel(rkv_ref, hp_ref, hn_ref, low_ref, shift_ref, w0_ref, a0_ref, wup_ref, aup_ref, gup_ref,
                      kk_ref, ka_ref, rk_ref, e_ref,
                      at_ref, rt_ref, bt_ref, kt_ref, bc_ref, kc_ref, gend_ref, v_ref, gate_ref, bonus_ref,
                      *, rows, chunk):
    tm = rkv_ref.shape[0]
    bw = v_ref.shape[1]
    first, last = rows.seq_edges(pl.program_id(0), tm)
    rkv = _conv3_tile(rkv_ref[...], hp_ref[SUBLANES - 1:SUBLANES, :], hn_ref[0:1, :], shift_ref, first, last)
    r, k, v = rkv[:, :bw], rkv[:, bw:2 * bw], rkv[:, 2 * bw:]
    e = e_ref[...]
    kk = k * kk_ref[...]
    kk = kk * lax.rsqrt(_dot_split_lhs(kk * kk, e) + 1e-12)
    low = low_ref[...]
    wd = low[:, :RWKV_DECAY_RANK]
    ad = low[:, RWKV_DECAY_RANK:RWKV_DECAY_RANK + RWKV_AAA_RANK]
    gd = low[:, RWKV_DECAY_RANK + RWKV_AAA_RANK:]
    w_pre = _dot3(jnp.tanh(wd), wup_ref[...])
    a_pre = _dot3(ad, aup_ref[...])
    gate_ref[...] = _dot3(jax.nn.sigmoid(gd), gup_ref[...])
    v_ref[...] = v.astype(BF16)
    kt_sum = None
    for d in range(2):
        w_log = -jax.nn.softplus(-(w0_ref[d:d + 1, :] + w_pre[:, d * bw:(d + 1) * bw])) - 0.5
        lw = -jnp.exp(w_log)
        a = jax.nn.sigmoid(a0_ref[d:d + 1, :] + a_pre[:, d * bw:(d + 1) * bw])
        kt = k * (1.0 + (a - 1.0) * ka_ref[...])
        kt_sum = kt if kt_sum is None else kt_sum + kt
        b = kk * a
        lam = _dot_split_rhs(_chunk_cumsum_matrix(tm, chunk, d == 1), lw)
        g_inv = jnp.exp(-lam)
        at_ref[d] = (-kk * jnp.exp(lam - lw)).astype(BF16)
        rt_ref[d] = (r * jnp.exp(lam)).astype(BF16)
        bt_ref[d] = (b * g_inv).astype(BF16)
        kt_ref[d] = (kt * g_inv).astype(BF16)
        for q in range(tm // chunk):
            end = q * chunk if d == 1 else (q + 1) * chunk - 1
            lam_end = lam[end:end + 1, :]
            sl = slice(q * chunk, (q + 1) * chunk)
            to_end = jnp.exp(lam_end - lam[sl])
            bc_ref[d, sl, :] = (b[sl] * to_end).astype(BF16)
            kc_ref[d, sl, :] = (kt[sl] * to_end).astype(BF16)
            gend_ref[d, q] = jnp.exp(lam_end)
    bonus_ref[...] = _dot_split_lhs(r * kt_sum * rk_ref[...], e, terms=1) * v


def rwkv_prep(pa, rows, layer, shift, w0, w_up, a0, a_up, g_up, k_k, k_a, r_k, seg_ones, *, tm=PREP_ROWS,
              chunk=RWKV_CHUNK):
    m = pa.shape[0]
    bw = k_k.shape[-1]
    low_w = RWKV_DECAY_RANK + RWKV_AAA_RANK + RWKV_GATE_RANK
    tm = min(tm, rows.n_ctx)
    assert m % tm == 0 and rows.n_ctx % tm == 0 and rows.n_lat % tm == 0 and tm % chunk == 0
    assert (3 * bw) % low_w == 0
    wup = jnp.concatenate([w_up[layer, 0], w_up[layer, 1]], axis=1)
    aup = jnp.concatenate([a_up[layer, 0], a_up[layer, 1]], axis=1)
    row = lambda t: t.reshape(1, bw)
    big = jax.ShapeDtypeStruct((2, m, bw), BF16)
    big_spec = pl.BlockSpec((2, tm, bw), lambda i: (0, i, 0))
    one_spec = pl.BlockSpec((tm, bw), lambda i: (i, 0))
    cpt = tm // chunk
    return pl.pallas_call(
        functools.partial(_rwkv_prep_kernel, rows=rows, chunk=chunk),
        grid=(m // tm,),
        in_specs=[pl.BlockSpec((tm, 3 * bw), lambda i: (i, 0))] + _halo_specs(tm, 3 * bw, 0, m)
        + [pl.BlockSpec((tm, low_w), lambda i: (i, 3 * bw // low_w)),
           _full((3, 3 * bw)), _full((2, bw)), _full((2, bw)), _full(wup.shape), _full(aup.shape),
           pl.BlockSpec((None,) + g_up.shape[1:], lambda i: (layer, 0, 0)),
           _full((1, bw)), _full((1, bw)), _full((1, bw)), _full(seg_ones.shape)],
        out_specs=[big_spec] * 6 + [pl.BlockSpec((2, cpt, 1, bw), lambda i: (0, i, 0, 0)),
                                    one_spec, one_spec, one_spec],
        out_shape=[big] * 6 + [jax.ShapeDtypeStruct((2, m // chunk, 1, bw), F32),
                               jax.ShapeDtypeStruct((m, bw), BF16), jax.ShapeDtypeStruct((m, bw), F32),
                               jax.ShapeDtypeStruct((m, bw), F32)],
        compiler_params=_params("arbitrary"),
        name="rwkv_prep",
    )(pa, pa, pa, pa, shift[layer], w0[layer], a0[layer], wup, aup, g_up, row(k_k[layer]), row(k_a[layer]),
      row(r_k[layer]), seg_ones)


def _pair_select(lo, z):
    half = z.shape[0] // 2
    return jnp.where(lo, z[:half], z[half:])


def _rwkv_scan_kernel(at_ref, rt_ref, bt_ref, kt_ref, bc_ref, kc_ref, gend_ref, v_ref, y_ref, s_ref, *, chunk):
    @pl.when(pl.program_id(2) == 0)
    def _():
        s_ref[...] = jnp.zeros_like(s_ref)

    c2 = 2 * chunk
    n_pairs = s_ref.shape[0]
    pw = 2 * RWKV_HEAD
    n_levels = chunk.bit_length() - 1
    order = lax.broadcasted_iota(jnp.int32, (chunk, chunk), 0) - lax.broadcasted_iota(jnp.int32, (chunk, chunk), 1)
    order = jnp.where(pl.program_id(0) == 1, -order, order)
    strict = order > 0
    incl = order >= 0
    lo = lax.broadcasted_iota(jnp.int32, (chunk, pw), 1) < RWKV_HEAD
    lo2 = lax.broadcasted_iota(jnp.int32, (c2, pw), 1) < RWKV_HEAD
    own = ((lax.broadcasted_iota(jnp.int32, (pw, pw), 0) < RWKV_HEAD)
           == (lax.broadcasted_iota(jnp.int32, (pw, pw), 1) < RWKV_HEAD))
    pairs = range(n_pairs)
    sl = [slice(j * pw, (j + 1) * pw) for j in pairs]

    def tri(mask, z):
        return jnp.where(mask, z, 0.0)

    a = [at_ref[:, s] for s in sl]
    r = [rt_ref[:, s] for s in sl]
    v = [v_ref[:, s] for s in sl]
    s0 = [s_ref[j] for j in pairs]
    ar = [jnp.concatenate([a[j], r[j]], axis=0) for j in pairs]
    ar_st = [jnp.concatenate([jnp.where(lo2, ar[j], 0), jnp.where(lo2, 0, ar[j])], axis=0) for j in pairs]
    p_b = [_dot_nt(ar_st[j], bt_ref[:, sl[j]]) for j in pairs]
    p_k = [_dot_nt(ar_st[j], kt_ref[:, sl[j]]) for j in pairs]
    lp = [[tri(strict, p_b[j][e * c2:e * c2 + chunk]) for e in range(2)] for j in pairs]
    l_ak = [jnp.concatenate([tri(strict, p_k[j][e * c2:e * c2 + chunk]) for e in range(2)], axis=0).astype(BF16)
            for j in pairs]
    m_rb = [jnp.concatenate([tri(incl, p_b[j][e * c2 + chunk:(e + 1) * c2]) for e in range(2)], axis=0).astype(BF16)
            for j in pairs]
    m_rk = [jnp.concatenate([tri(incl, p_k[j][e * c2 + chunk:(e + 1) * c2]) for e in range(2)], axis=0).astype(BF16)
            for j in pairs]
    x_w = [a[j].astype(F32) for j in pairs]
    x_u = [_pair_select(lo, _dot(l_ak[j], v[j])) for j in pairs]
    for lvl in range(n_levels):
        lpb = [[lp[j][e].astype(BF16) for e in range(2)] for j in pairs]
        lst = [jnp.concatenate(lpb[j], axis=0) for j in pairs]
        z = [_dot(lst[j], jnp.concatenate([x_w[j], x_u[j]], axis=1).astype(BF16)) for j in pairs]
        x_w = [x_w[j] + _pair_select(lo, z[j][:, :pw]) for j in pairs]
        x_u = [x_u[j] + _pair_select(lo, z[j][:, pw:]) for j in pairs]
        if lvl + 1 < n_levels:
            lp = [[_dot(lpb[j][e], lpb[j][e]) for e in range(2)] for j in pairs]
    s0b = [s0[j].astype(BF16) for j in pairs]
    u = [_dot_nt(x_w[j].astype(BF16), s0b[j]) + x_u[j] for j in pairs]
    ub = [u[j].astype(BF16) for j in pairs]
    y = [_dot_nt(r[j], s0b[j]) + _pair_select(lo, _dot(m_rb[j], ub[j])) + _pair_select(lo, _dot(m_rk[j], v[j]))
         for j in pairs]
    s_new = [s0[j] * gend_ref[:, sl[j]]
             + jnp.where(own, _dot_tn(ub[j], bc_ref[:, sl[j]]) + _dot_tn(v[j], kc_ref[:, sl[j]]), 0.0)
             for j in pairs]
    for j in pairs:
        y_ref[:, sl[j]] = y[j]
    for j in pairs:
        s_ref[j] = s_new[j]


def rwkv_scan(ops, v, rows, *, chunk=RWKV_CHUNK):
    at, rt, bt, kt, bc, kc, gend = ops
    _, m, bw = at.shape
    block, nc = rows.chunk_block(chunk)
    big = pl.BlockSpec((None, chunk, bw), lambda d, b, c: (d, block(d, b, c), 0))
    return pl.pallas_call(
        functools.partial(_rwkv_scan_kernel, chunk=chunk),
        grid=(2, rows.batch, nc),
        in_specs=[big] * 6 + [pl.BlockSpec((None, None, 1, bw), lambda d, b, c: (d, block(d, b, c), 0, 0)),
                              pl.BlockSpec((chunk, bw), lambda d, b, c: (block(d, b, c), 0))],
        out_specs=big,
        out_shape=jax.ShapeDtypeStruct((2, m, bw), F32),
        scratch_shapes=[pltpu.VMEM((bw // (2 * RWKV_HEAD), 2 * RWKV_HEAD, 2 * RWKV_HEAD), F32)],
        compiler_params=_params("arbitrary", "arbitrary", "arbitrary"),
        name="rwkv_scan",
    )(at, rt, bt, kt, bc, kc, gend, v)


def _rwkv_readout_kernel(y_ref, bonus_ref, gate_ref, e_ref, g_ref, b_ref, o_ref):
    y = y_ref[0] + y_ref[1]
    e = e_ref[...]
    mu = _dot_split_lhs(y, e) * (1.0 / RWKV_HEAD)
    yc = y - mu
    var = _dot_split_lhs(yc * yc, e) * (1.0 / RWKV_HEAD)
    yn = yc * lax.rsqrt(var + RWKV_LN_EPS) * g_ref[...] + b_ref[...]
    o_ref[...] = ((yn + bonus_ref[...]) * gate_ref[...]).astype(o_ref.dtype)


def rwkv_readout(y, bonus, gate, seg_ones, ln_g, ln_b, *, rows, tm=PREP_ROWS):
    bw = y.shape[-1]
    tm = math.gcd(tm, rows)
    one = pl.BlockSpec((tm, bw), lambda i: (i, 0))
    return pl.pallas_call(
        _rwkv_readout_kernel,
        grid=(rows // tm,),
        in_specs=[pl.BlockSpec((2, tm, bw), lambda i: (0, i, 0)), one, one, _full(seg_ones.shape),
                  _full((1, bw)), _full((1, bw))],
        out_specs=one,
        out_shape=jax.ShapeDtypeStruct((rows, bw), BF16),
        compiler_params=_params("arbitrary"),
        name="rwkv_readout",
    )(y, bonus, gate, seg_ones, ln_g.reshape(1, bw), ln_b.reshape(1, bw))


def _ssd_prep_kernel(x_ref, hp_ref, hn_ref, w_ref, b_ref, o_ref, *, rows):
    first, last = rows.seq_edges(pl.program_id(0), x_ref.shape[0])
    y = _conv3_tile(x_ref[...], hp_ref[SUBLANES - 1:SUBLANES, :], hn_ref[0:1, :], w_ref, first, last) + b_ref[...]
    o_ref[...] = y * jax.nn.sigmoid(y)


def ssd_prep(pc, rows, conv_w, conv_b, *, width, first_block, n_blocks, tm=PREP_ROWS):
    m = pc.shape[0]
    tm = min(tm, rows.n_ctx)
    assert m % tm == 0 and rows.n_ctx % tm == 0 and rows.n_lat % tm == 0
    per = tm // SUBLANES
    last = m // SUBLANES - 1
    return pl.pallas_call(
        functools.partial(_ssd_prep_kernel, rows=rows),
        grid=(m // tm, n_blocks),
        in_specs=[pl.BlockSpec((tm, width), lambda i, j: (i, first_block + j)),
                  pl.BlockSpec((SUBLANES, width), lambda i, j: (jnp.maximum(i * per - 1, 0), first_block + j)),
                  pl.BlockSpec((SUBLANES, width), lambda i, j: (jnp.minimum((i + 1) * per, last), first_block + j)),
                  pl.BlockSpec((3, width), lambda i, j: (0, j)),
                  pl.BlockSpec((1, width), lambda i, j: (0, j))],
        out_specs=pl.BlockSpec((tm, width), lambda i, j: (i, j)),
        out_shape=jax.ShapeDtypeStruct((m, n_blocks * width), F32),
        compiler_params=_params("arbitrary", "arbitrary"),
        name="ssd_prep",
    )(pc, pc, pc, conv_w, conv_b.reshape(1, n_blocks * width))


def _ssd_scan_kernel(x_ref, b_ref, c_ref, la_ref, lat_ref, dt_ref, y_ref, s_ref, *, chunk, heads, groups):
    @pl.when(pl.program_id(2) == 0)
    def _():
        s_ref[...] = jnp.zeros_like(s_ref)

    rev = pl.program_id(0) == 1
    pw = 2 * SSD_HEAD
    order = lax.broadcasted_iota(jnp.int32, (chunk, chunk), 0) - lax.broadcasted_iota(jnp.int32, (chunk, chunk), 1)
    order = jnp.where(rev, -order, order)
    upto = order >= 0
    lo = lax.broadcasted_iota(jnp.int32, (chunk, pw), 1) < SSD_HEAD
    acs_c_all = _dot_hi(upto.astype(F32), la_ref[...])
    acs_r_all = _dot_hi(lat_ref[...], (order <= 0).astype(F32))
    tot_c_all = jnp.where(rev, acs_c_all[0:1, :], acs_c_all[chunk - 1:chunk, :])
    dt_all = dt_ref[...]
    hpg = heads // groups
    for gi in range(groups):
        cg = c_ref[:, gi * SSD_STATE:(gi + 1) * SSD_STATE].astype(BF16)
        bg = b_ref[:, gi * SSD_STATE:(gi + 1) * SSD_STATE].astype(BF16)
        cb = _dot_nt(cg, bg)
        for qi in range(hpg // 2):
            h0 = gi * hpg + 2 * qi
            j = h0 // 2
            cols = slice(j * pw, (j + 1) * pw)
            w_st, acs_c, tot = [], [], []
            for e in range(2):
                h = h0 + e
                acs_c.append(acs_c_all[:, h:h + 1])
                tot.append(tot_c_all[:, h:h + 1])
                decay = jnp.exp(jnp.where(upto, acs_c[e] - acs_r_all[h:h + 1, :], -1e30))
                w_st.append((cb * decay).astype(BF16))
            xdt = x_ref[:, cols] * jnp.where(lo, dt_all[:, h0:h0 + 1], dt_all[:, h0 + 1:h0 + 2])
            st = s_ref[j]
            y = _pair_select(lo, _dot(jnp.concatenate(w_st, axis=0), xdt.astype(BF16)))
            y = y + _dot(cg, st.astype(BF16)) * jnp.where(lo, jnp.exp(acs_c[0]), jnp.exp(acs_c[1]))
            y_ref[:, cols] = y
            to_end = jnp.where(lo, jnp.exp(tot[0] - acs_c[0]), jnp.exp(tot[1] - acs_c[1]))
            s_ref[j] = (st * jnp.where(lo[0:1, :], jnp.exp(tot[0]), jnp.exp(tot[1]))
                        + _dot_tn(bg, (xdt * to_end).astype(BF16)))


def ssd_scan(xbc, la, dt, rows, *, heads, chunk=SSD_CHUNK):
    m = xbc.shape[0]
    bw = heads * SSD_HEAD
    gn = SSD_GROUPS * SSD_STATE
    assert bw % gn == 0
    block, nc = rows.chunk_block(chunk)
    lat = jnp.swapaxes(la, 1, 2)
    thin = pl.BlockSpec((None, chunk, heads), lambda d, b, c: (d, block(d, b, c), 0))
    return pl.pallas_call(
        functools.partial(_ssd_scan_kernel, chunk=chunk, heads=heads, groups=SSD_GROUPS),
        grid=(2, rows.batch, nc),
        in_specs=[pl.BlockSpec((chunk, bw), lambda d, b, c: (block(d, b, c), 0)),
                  pl.BlockSpec((chunk, gn), lambda d, b, c: (block(d, b, c), bw // gn)),
                  pl.BlockSpec((chunk, gn), lambda d, b, c: (block(d, b, c), bw // gn + 1)),
                  thin,
                  pl.BlockSpec((None, heads, chunk), lambda d, b, c: (d, 0, block(d, b, c))),
                  thin],
        out_specs=pl.BlockSpec((None, chunk, bw), lambda d, b, c: (d, block(d, b, c), 0)),
        out_shape=jax.ShapeDtypeStruct((2, m, bw), F32),
        scratch_shapes=[pltpu.VMEM((heads // 2, SSD_STATE, 2 * SSD_HEAD), F32)],
        compiler_params=_params("arbitrary", "arbitrary", "arbitrary"),
        name="ssd_scan",
    )(xbc, xbc, xbc, la, lat, dt)


def _ssd_readout_kernel(y_ref, x_ref, z_ref, d_ref, g_ref, o_ref):
    y = y_ref[0] + y_ref[1] + d_ref[...] * x_ref[...]
    z = z_ref[...]
    o_ref[...] = (_rms(y * (z * jax.nn.sigmoid(z))) * g_ref[...]).astype(o_ref.dtype)


def ssd_readout(y, xbc, pc, d_skip, norm_g, *, rows, tm=PREP_ROWS):
    bw = y.shape[-1]
    tm = math.gcd(tm, rows)
    one = pl.BlockSpec((tm, bw), lambda i: (i, 0))
    return pl.pallas_call(
        _ssd_readout_kernel,
        grid=(rows // tm,),
        in_specs=[pl.BlockSpec((2, tm, bw), lambda i: (0, i, 0)), one, one, _full((1, bw)), _full((1, bw))],
        out_specs=one,
        out_shape=jax.ShapeDtypeStruct((rows, bw), BF16),
        compiler_params=_params("arbitrary"),
        name="ssd_readout",
    )(y, xbc, pc, jnp.repeat(d_skip, SSD_HEAD).reshape(1, bw), norm_g.reshape(1, bw))


def _softmax_pv(scores, values):
    m = None
    for s in scores:
        ms = jnp.max(s, axis=-1, keepdims=True)
        m = ms if m is None else jnp.maximum(m, ms)
    l, o = None, None
    for s, v in zip(scores, values):
        p = jnp.exp(s - m)
        ls = jnp.sum(p, axis=-1, keepdims=True)
        os_ = _dot(p.astype(BF16), v)
        l = ls if l is None else l + ls
        o = os_ if o is None else o + os_
    return o / l


def _attn_kernel(lam_ref, q_ref, g_ref, *refs, n_seg, scale, diff, out_scale, row_parts):
    seg_refs = refs[:2 * n_seg]
    o_ref = refs[2 * n_seg]
    values = [seg_refs[2 * si + 1][...] for si in range(n_seg)]
    rows_per = q_ref.shape[0] // row_parts
    for r in range(row_parts):
        rs = slice(r * rows_per, (r + 1) * rows_per)
        q = q_ref[rs, :].astype(F32) * scale
        if not diff:
            qb = q.astype(BF16)
            scores = [_dot_nt(qb, seg_refs[2 * si][...]) for si in range(n_seg)]
            o_ref[rs, :] = _softmax_pv(scores, values).astype(o_ref.dtype)
            continue
        first = lax.broadcasted_iota(jnp.int32, q.shape, 1) < DIFF_HEAD
        q1 = jnp.where(first, q, 0.0).astype(BF16)
        q2 = jnp.where(first, 0.0, q).astype(BF16)
        o1 = _softmax_pv([_dot_nt(q1, seg_refs[2 * si][...]) for si in range(n_seg)], values)
        o2 = _softmax_pv([_dot_nt(q2, seg_refs[2 * si][...]) for si in range(n_seg)], values)
        o = o1 - lam_ref[0] * o2
        o_ref[rs, :] = (_rms(o) * g_ref[...] * out_scale).astype(o_ref.dtype)


def attention(q, k, v, *, heads, batch, q_rows, q_first, segments, scale, tq=256, row_parts=1,
              diff_lam=None, diff_gain=None, out_scale=1.0):
    dqk = q.shape[1] // heads
    dv = v.shape[1] // heads
    tq = min(tq, q_rows)
    nq = q_rows // tq
    if tq % (row_parts * 2 * SUBLANES):
        row_parts = 1
    diff = diff_lam is not None
    lam = (diff_lam if diff else jnp.zeros((), F32)).reshape(1).astype(F32)
    gain = (diff_gain if diff else jnp.ones((dv,), F32)).reshape(1, dv).astype(F32)
    seg_specs = []
    for seg_rows, first in segments:
        for w in (dqk, dv):
            seg_specs.append(pl.BlockSpec((seg_rows, w), functools.partial(lambda b, h, i, f: (f + b, h), f=first)))
    return pl.pallas_call(
        functools.partial(_attn_kernel, n_seg=len(segments), scale=scale, diff=diff, out_scale=out_scale,
                          row_parts=row_parts),
        grid=(batch, heads, nq),
        in_specs=[pl.BlockSpec(memory_space=pltpu.SMEM),
                  pl.BlockSpec((tq, dqk), lambda b, h, i: ((q_first + b) * nq + i, h)),
                  pl.BlockSpec((1, dv), lambda b, h, i: (0, 0))] + seg_specs,
        out_specs=pl.BlockSpec((tq, dv), lambda b, h, i: (b * nq + i, h)),
        out_shape=jax.ShapeDtypeStruct((batch * q_rows, heads * dv), BF16),
        compiler_params=_params("arbitrary", "arbitrary", "arbitrary"),
        name="diff_attention" if diff else "mla_attention",
    )(lam, q, gain, *([k, v] * len(segments)))


ROPE_GROUP = 64
ROPE_QUARTER = ROPE_GROUP // 4


def _axial_rope_tables(n_tok):
    t = jnp.arange(n_tok)
    row = (t // GRID_W).astype(F32)
    col = (t % GRID_W).astype(F32)
    axis_dim = ROPE_GROUP // 2
    inv = 1.0 / (ROPE_BASE ** (jnp.arange(0, axis_dim, 2, dtype=F32) / axis_dim))
    ar = row[:, None] * inv[None]
    ac = col[:, None] * inv[None]
    ang = jnp.concatenate([ar, ar, ac, ac], axis=-1)
    sign = jnp.where((jnp.arange(ROPE_GROUP) % (2 * ROPE_QUARTER)) < ROPE_QUARTER, -1.0, 1.0)
    return jnp.cos(ang), jnp.sin(ang) * sign


def _rope_lanes(x, cos, sin_signed):
    lane = lax.broadcasted_iota(jnp.int32, x.shape, 1)
    takes_next = (lane % (2 * ROPE_QUARTER)) < ROPE_QUARTER
    rot = jnp.where(takes_next, pltpu.roll(x, LANES - ROPE_QUARTER, 1), pltpu.roll(x, ROPE_QUARTER, 1))
    return x * cos + rot * sin_signed


def _rope_table_spec(rows, tm):
    per_seq = rows.n_lat // tm
    return pl.BlockSpec((tm, LANES), lambda i: (jnp.where(i * tm < rows.m_lat, i % per_seq, 0), 0))


def _tile_tables(i, tm, rows, cos_ref, sin_ref):
    is_lat = i * tm < rows.m_lat
    return jnp.where(is_lat, cos_ref[...], 1.0), jnp.where(is_lat, sin_ref[...], 0.0)


def _diff_prep_kernel(q_ref, k_ref, v_ref, cos_ref, sin_ref, qo_ref, ko_ref, vo_ref, *, rows):
    tm = q_ref.shape[0]
    cos, sin = _tile_tables(pl.program_id(0), tm, rows, cos_ref, sin_ref)
    for src, dst in ((q_ref, qo_ref), (k_ref, ko_ref)):
        for cb in range(src.shape[1] // LANES):
            cols = slice(cb * LANES, (cb + 1) * LANES)
            dst[:, cols] = _rope_lanes(src[:, cols], cos, sin).astype(dst.dtype)
    vo_ref[...] = v_ref[...].astype(vo_ref.dtype)


def diff_prep(pd, rows, cos2, sin2, *, tm=PREP_ROWS):
    m = pd.shape[0]
    bw = pd.shape[1] // 3
    tm = min(tm, rows.n_ctx)
    assert m % tm == 0 and rows.n_ctx % tm == 0 and rows.n_lat % tm == 0
    out = jax.ShapeDtypeStruct((m, bw), BF16)
    one = pl.BlockSpec((tm, bw), lambda i: (i, 0))
    tab = _rope_table_spec(rows, tm)
    return pl.pallas_call(
        functools.partial(_diff_prep_kernel, rows=rows),
        grid=(m // tm,),
        in_specs=[pl.BlockSpec((tm, bw), functools.partial(lambda i, j: (i, j), j=j)) for j in range(3)] + [tab, tab],
        out_specs=[one] * 3,
        out_shape=[out] * 3,
        compiler_params=_params("arbitrary"),
        name="diff_prep",
    )(pd, pd, pd, cos2, sin2)


def _mla_q_kernel(cq_ref, g_ref, w_ref, cos_ref, sin_ref, o_ref, *, rows):
    tm = cq_ref.shape[0]
    cos, sin = _tile_tables(pl.program_id(0), tm, rows, cos_ref, sin_ref)
    q = _dot((_rms(cq_ref[...]) * g_ref[...]).astype(BF16), w_ref[...])
    for h in range(MLA_HEADS):
        nope = slice(2 * h * LANES, (2 * h + 1) * LANES)
        rope = slice((2 * h + 1) * LANES, (2 * h + 2) * LANES)
        o_ref[:, nope] = q[:, nope].astype(o_ref.dtype)
        o_ref[:, rope] = _rope_lanes(q[:, rope], cos, sin).astype(o_ref.dtype)


def mla_q_prep(pb, rows, q_norm, w_uq, cos2, sin2, *, tm=PREP_ROWS):
    m = pb.shape[0]
    tm = min(tm, rows.n_ctx)
    w = w_uq.reshape(MLA_Q_RANK, MLA_HEADS, MLA_NOPE + MLA_ROPE)
    w = jnp.pad(w, ((0, 0), (0, 0), (0, 2 * LANES - MLA_NOPE - MLA_ROPE))).reshape(MLA_Q_RANK, -1).astype(BF16)
    tab = _rope_table_spec(rows, tm)
    return pl.pallas_call(
        functools.partial(_mla_q_kernel, rows=rows),
        grid=(m // tm,),
        in_specs=[pl.BlockSpec((tm, MLA_Q_RANK), lambda i: (i, 0)), _full((1, MLA_Q_RANK)), _full(w.shape), tab, tab],
        out_specs=pl.BlockSpec((tm, w.shape[1]), lambda i: (i, 0)),
        out_shape=jax.ShapeDtypeStruct((m, w.shape[1]), BF16),
        compiler_params=_params("arbitrary"),
        name="mla_q_prep",
    )(pb, q_norm.reshape(1, MLA_Q_RANK), w, cos2, sin2)


def _mla_kv_kernel(ckv_ref, kr_ref, g_ref, w_ref, cos_ref, sin_ref, k_ref, v_ref, *, rows):
    tm = ckv_ref.shape[0]
    cos, sin = _tile_tables(pl.program_id(0), tm, rows, cos_ref, sin_ref)
    kv = _dot((_rms(ckv_ref[...]) * g_ref[...]).astype(BF16), w_ref[...])
    lane = lax.broadcasted_iota(jnp.int32, (tm, LANES), 1)
    kr = jnp.where(lane < MLA_ROPE, _rope_lanes(kr_ref[...], cos, sin), 0.0).astype(k_ref.dtype)
    nope_w = MLA_HEADS * MLA_NOPE
    for h in range(MLA_HEADS):
        k_ref[:, 2 * h * LANES:(2 * h + 1) * LANES] = kv[:, h * MLA_NOPE:(h + 1) * MLA_NOPE].astype(k_ref.dtype)
        k_ref[:, (2 * h + 1) * LANES:(2 * h + 2) * LANES] = kr
    v_ref[...] = kv[:, nope_w:].astype(v_ref.dtype)


def mla_kv_prep(pb, rows, kv_norm, w_ukv, cos2, sin2, *, tm=PREP_ROWS):
    m = pb.shape[0]
    tm = min(tm, rows.n_ctx)
    assert MLA_NOPE == LANES and MLA_Q_RANK % MLA_KV_RANK == 0 and (MLA_Q_RANK + MLA_KV_RANK) % LANES == 0
    w = w_ukv.reshape(MLA_KV_RANK, MLA_HEADS, 2 * MLA_NOPE)
    w = jnp.concatenate([w[:, :, :MLA_NOPE].reshape(MLA_KV_RANK, -1), w[:, :, MLA_NOPE:].reshape(MLA_KV_RANK, -1)],
                        axis=1).astype(BF16)
    nope_w = MLA_HEADS * MLA_NOPE
    tab = _rope_table_spec(rows, tm)
    return pl.pallas_call(
        functools.partial(_mla_kv_kernel, rows=rows),
        grid=(m // tm,),
        in_specs=[pl.BlockSpec((tm, MLA_KV_RANK), lambda i: (i, MLA_Q_RANK // MLA_KV_RANK)),
                  pl.BlockSpec((tm, LANES), lambda i: (i, (MLA_Q_RANK + MLA_KV_RANK) // LANES)),
                  _full((1, MLA_KV_RANK)), _full(w.shape), tab, tab],
        out_specs=[pl.BlockSpec((tm, 2 * nope_w), lambda i: (i, 0)), pl.BlockSpec((tm, nope_w), lambda i: (i, 0))],
        out_shape=[jax.ShapeDtypeStruct((m, 2 * nope_w), BF16), jax.ShapeDtypeStruct((m, nope_w), BF16)],
        compiler_params=_params("arbitrary"),
        name="mla_kv_prep",
    )(pb, pb, kv_norm.reshape(1, MLA_KV_RANK), w, cos2, sin2)


def kernel(x, c, ctx, c_ctx, ada_w, ada_b, norm_mix_pre, norm_mix_post, norm_ffn_pre, norm_ffn_post, w_in, rwkv_shift, rwkv_w0, rwkv_w_up, rwkv_a0, rwkv_a_up, rwkv_g_up, rwkv_k_k, rwkv_k_a, rwkv_r_k, rwkv_ln_g, rwkv_ln_b, mla_q_norm, mla_w_uq, mla_kv_norm, mla_w_ukv, ssd_conv_w, ssd_conv_b, ssd_dt_bias, ssd_a_log, ssd_d, ssd_norm, diff_lq1, diff_lk1, diff_lq2, diff_lk2, diff_subln, w_branch, w_gate, w_out, w_ff1, w_ff2):
    batch, n_lat, d = x.shape
    n_ctx = ctx.shape[1]
    depth = ada_w.shape[0]
    bw = d // N_BRANCH
    rows = _Rows(batch, n_lat, n_ctx)
    m_lat, m_all = rows.m_lat, rows.m
    ssd_heads = bw // SSD_HEAD
    gn = SSD_GROUPS * SSD_STATE
    rwkv_in = 3 * bw + RWKV_DECAY_RANK + RWKV_AAA_RANK + RWKV_GATE_RANK
    mla_in = MLA_Q_RANK + MLA_KV_RANK + MLA_ROPE
    ssd_in = 2 * bw + 2 * gn + ssd_heads
    diff_in = 3 * bw
    o_mla, o_ssd, o_diff = rwkv_in, rwkv_in + mla_in, rwkv_in + mla_in + ssd_in
    assert w_in.shape[-1] == o_diff + diff_in
    assert n_lat % SSD_CHUNK == 0 and n_ctx % SSD_CHUNK == 0

    assert MLA_ROPE == ROPE_GROUP and DIFF_HEAD == ROPE_GROUP
    cos2, sin2 = (jnp.tile(t, (1, LANES // ROPE_GROUP)) for t in _axial_rope_tables(n_lat))
    hid = jnp.arange(bw) // RWKV_HEAD
    seg_ones = (hid[:, None] == hid[None, :]).astype(BF16)

    xs = jnp.concatenate([x.reshape(m_lat, d), ctx.reshape(batch * n_ctx, d)], axis=0)
    cvec = jnp.zeros((8, d), F32).at[:batch].set(jax.nn.silu(c)).at[batch].set(jax.nn.silu(c_ctx))
    groups = batch + 1

    mods = [(matmul(cvec, ada_w, layer=l, tn=512) + ada_b[l])[:groups].reshape(groups, 1, 6 * d)
            for l in range(depth)]
    h = norm_modulate(xs, norm_mix_pre[0], mods[0], 0, 1, rows=m_all, rows_per_group=n_lat)

    for l in range(depth):
        ctx_out = l < depth - 1
        m_out = m_all if ctx_out else m_lat
        mod = mods[l]

        o_dt = o_diff - ssd_heads
        pa = matmul(h, w_in, layer=l, n=o_mla, tm=1024, tn=512)
        pb = matmul(h, jnp.concatenate([w_in[l, :, o_mla:o_ssd], w_in[l, :, o_dt:o_diff]], axis=1), tm=1024, tn=384)
        pc = matmul(h, w_in[l, :, o_ssd:o_dt])
        pd = matmul(h, w_in[l, :, o_diff:])

        prep = rwkv_prep(pa, rows, l, rwkv_shift, rwkv_w0, rwkv_w_up, rwkv_a0, rwkv_a_up, rwkv_g_up,
                         rwkv_k_k, rwkv_k_a, rwkv_r_k.reshape(depth, bw), seg_ones)
        y_rwkv = rwkv_scan(prep[:7], prep[7], rows)
        ya = rwkv_readout(y_rwkv, prep[9], prep[8], seg_ones, rwkv_ln_g[l], rwkv_ln_b[l], rows=m_out)

        q_m = mla_q_prep(pb, rows, mla_q_norm[l], mla_w_uq[l], cos2, sin2)
        k_m, vv = mla_kv_prep(pb, rows, mla_kv_norm[l], mla_w_ukv[l], cos2, sin2)
        ctx_blk = m_lat // n_ctx
        segs_lat = [(n_lat, 0), (n_ctx, ctx_blk)]
        mla_scale = (MLA_NOPE + MLA_ROPE) ** -0.5
        yb = attention(q_m, k_m, vv, heads=MLA_HEADS, batch=batch, q_rows=n_lat, q_first=0, segments=segs_lat,
                       scale=mla_scale, tq=512, row_parts=2)
        if ctx_out:
            yb_c = attention(q_m, k_m, vv, heads=MLA_HEADS, batch=batch, q_rows=n_ctx, q_first=ctx_blk,
                             segments=[(n_ctx, ctx_blk)], scale=mla_scale)
            yb = jnp.concatenate([yb, yb_c], axis=0)

        assert (2 * gn) % bw == 0
        xbc = ssd_prep(pc, rows, ssd_conv_w[l], ssd_conv_b[l], width=bw, first_block=1, n_blocks=1 + 2 * gn // bw)
        dt_raw = pb[:, mla_in:mla_in + ssd_heads]
        dts = jnp.stack([jax.nn.softplus(dt_raw + ssd_dt_bias[l, dr]) for dr in range(2)])
        las = dts * (-jnp.exp(ssd_a_log[l]))[:, None, :]
        y_ssd = ssd_scan(xbc, las, dts, rows, heads=ssd_heads)
        yc = ssd_readout(y_ssd, xbc, pc, ssd_d[l], ssd_norm[l], rows=m_out)

        q_d, k_d, v_d = diff_prep(pd, rows, cos2, sin2)
        lam_init = 0.8 - 0.6 * math.exp(-0.3 * l)
        lam = (jnp.exp(jnp.sum(diff_lq1[l] * diff_lk1[l])) - jnp.exp(jnp.sum(diff_lq2[l] * diff_lk2[l])) + lam_init)
        diff_kw = dict(heads=DIFF_HEADS, batch=batch, scale=DIFF_HEAD ** -0.5, diff_lam=lam,
                       diff_gain=diff_subln[l], out_scale=1.0 - lam_init)
        yd = attention(q_d, k_d, v_d, q_rows=n_lat, q_first=0, segments=segs_lat, tq=1024, row_parts=2, **diff_kw)
        if ctx_out:
            yd_c = attention(q_d, k_d, v_d, q_rows=n_ctx, q_first=ctx_blk, segments=[(n_ctx, ctx_blk)], **diff_kw)
            yd = jnp.concatenate([yd, yd_c], axis=0)

        gates = matmul(h, w_gate, layer=l, rows=m_out, act="sigmoid", out_dtype=BF16)
        merged = merge_branches(gates, [ya, yb, yc, yd], w_branch, l, rows=m_out)
        mix = matmul(merged, w_out, layer=l)
        xs, hf = norm_residual(xs, mix, norm_mix_post[l], mod, 2, rows=m_out, rows_per_group=n_lat,
                               next_norm=(norm_ffn_pre[l], mod, 3, 4))
        f1 = matmul(hf, w_ff1, layer=l, act="relu2", out_dtype=BF16)
        f2 = matmul(f1, w_ff2, layer=l, tn=512)
        if ctx_out:
            xs, h = norm_residual(xs, f2, norm_ffn_post[l], mod, 5, rows=m_out, rows_per_group=n_lat,
                                  next_norm=(norm_mix_pre[l + 1], mods[l + 1], 0, 1))
        else:
            xs = norm_residual(xs, f2, norm_ffn_post[l], mod, 5, rows=m_out, rows_per_group=n_lat)

    return xs[:m_lat].reshape(batch, n_lat, d)
```

```python
import functools
import math

import jax
import jax.numpy as jnp
from jax import lax
from jax.experimental import pallas as pl
from jax.experimental.pallas import tpu as pltpu

F32 = jnp.float32
BF16 = jnp.bfloat16
HIGHEST = lax.Precision.HIGHEST

NORM_EPS = 1e-6
ROPE_BASE = 10000.0
GRID_W = 64
N_BRANCH = 4
LANES = 128
SUBLANES = 8

RWKV_HEAD = 64
RWKV_DECAY_RANK = 64
RWKV_AAA_RANK = 64
RWKV_GATE_RANK = 128
RWKV_LN_EPS = 64e-5
RWKV_CHUNK = 64

MLA_HEADS = 8
MLA_NOPE = 128
MLA_ROPE = 64
MLA_Q_RANK = 768
MLA_KV_RANK = 256

SSD_HEAD = 64
SSD_GROUPS = 4
SSD_STATE = 128
SSD_CHUNK = 128

DIFF_HEADS = 8
DIFF_HEAD = 64

PREP_ROWS = 256

VMEM_LIMIT_BYTES = 56 * 1024 * 1024


def _params(*sem):
    return pltpu.CompilerParams(dimension_semantics=sem, vmem_limit_bytes=VMEM_LIMIT_BYTES)


def _dot(a, b):
    return jnp.dot(a, b, preferred_element_type=F32)


def _dot_nt(a, b):
    return lax.dot_general(a, b, (((1,), (1,)), ((), ())), preferred_element_type=F32)


def _dot_tn(a, b):
    return lax.dot_general(a, b, (((0,), (0,)), ((), ())), preferred_element_type=F32)


def _dot_hi(a, b):
    return jnp.dot(a, b, preferred_element_type=F32, precision=HIGHEST)


def _split_bf16(x, terms):
    parts = []
    for _ in range(terms):
        p = x.astype(BF16)
        parts.append(p)
        x = x - p.astype(F32)
    return parts


def _dot_split_lhs(x, e, terms=2):
    return sum(_dot(p, e) for p in _split_bf16(x, terms))


def _dot_split_rhs(e, x, terms=2):
    return sum(_dot(e, p) for p in _split_bf16(x, terms))


def _dot3(a, b):
    a_hi, a_lo = _split_bf16(a, 2)
    b_hi, b_lo = _split_bf16(b, 2)
    return _dot(a_hi, b_hi) + (_dot(a_lo, b_hi) + _dot(a_hi, b_lo))


def _full(shape):
    nd = len(shape)
    return pl.BlockSpec(shape, lambda *_: (0,) * nd)


def _apply_act(y, act):
    if act is None:
        return y
    if act == "sigmoid":
        return 0.5 * jnp.tanh((0.5 * y).astype(BF16)) + 0.5
    if act == "relu2":
        return jnp.square(jnp.maximum(y, 0.0))
    raise ValueError(act)


def _mm_kernel(x_ref, w_ref, o_ref, wbf_ref, *acc, nk, act):
    k = pl.program_id(1)
    i = pl.program_id(2)

    @pl.when(i == 0)
    def _():
        wbf_ref[...] = w_ref[...].astype(BF16)

    y = _dot(x_ref[...].astype(BF16), wbf_ref[...])
    if nk == 1:
        o_ref[...] = _apply_act(y, act).astype(o_ref.dtype)
        return
    (acc_ref,) = acc
    tm = x_ref.shape[0]
    rows = pl.ds(pl.multiple_of(i * tm, tm), tm)

    @pl.when(k == 0)
    def _():
        acc_ref[rows, :] = y

    @pl.when(jnp.logical_and(k > 0, k < nk - 1))
    def _():
        acc_ref[rows, :] += y

    @pl.when(k == nk - 1)
    def _():
        o_ref[...] = _apply_act(acc_ref[rows, :] + y, act).astype(o_ref.dtype)


def matmul(x, w, *, layer=None, rows=None, tm=512, tn=1024, tk=4096, act=None, out_dtype=F32):
    m = x.shape[0] if rows is None else rows
    kdim = x.shape[1]
    n = w.shape[-1]
    assert w.shape[-2] == kdim
    tm = math.gcd(tm, m)
    tk = min(tk, kdim)
    tn = min(tn, n)
    assert m % tm == 0 and kdim % tk == 0
    ni, nk, nj = m // tm, kdim // tk, pl.cdiv(n, tn)
    if w.ndim == 3:
        w_spec = pl.BlockSpec((None, tk, tn), lambda j, k, i: (layer, k, j))
    else:
        w_spec = pl.BlockSpec((tk, tn), lambda j, k, i: (k, j))
    if nk == 1:
        o_map = lambda j, k, i: (i, j)
    else:
        o_map = lambda j, k, i: (jnp.where(k == nk - 1, i, 0), j)
    scratch = [pltpu.VMEM((tk, tn), BF16)]
    if nk > 1:
        scratch.append(pltpu.VMEM((m, tn), F32))
    return pl.pallas_call(
        functools.partial(_mm_kernel, nk=nk, act=act),
        grid=(nj, nk, ni),
        in_specs=[pl.BlockSpec((tm, tk), lambda j, k, i: (i, k)), w_spec],
        out_specs=pl.BlockSpec((tm, tn), o_map),
        out_shape=jax.ShapeDtypeStruct((m, n), out_dtype),
        scratch_shapes=scratch,
        compiler_params=_params("arbitrary", "arbitrary", "arbitrary"),
        name="matmul_ws",
    )(x, w)


def _merge_kernel(g0, g1, g2, g3, y0, y1, y2, y3, w_ref, o_ref, wbf_ref):
    @pl.when(pl.program_id(1) == 0)
    def _():
        wbf_ref[...] = w_ref[...].astype(BF16)

    acc = None
    for b, (g, y) in enumerate(((g0, y0), (g1, y1), (g2, y2), (g3, y3))):
        t = g[...].astype(F32) * _dot(y[...].astype(BF16), wbf_ref[b])
        acc = t if acc is None else acc + t
    o_ref[...] = acc.astype(o_ref.dtype)


def merge_branches(gates, ys, w_branch, layer, *, rows, tm=1024, tn=512):
    d = w_branch.shape[-1]
    bw = w_branch.shape[-2]
    tm = math.gcd(tm, rows)
    assert rows % tm == 0 and d % tn == 0
    nj = d // tn
    g_specs = [pl.BlockSpec((tm, tn), functools.partial(lambda j, i, b: (i, b * nj + j), b=b))
               for b in range(N_BRANCH)]
    y_specs = [pl.BlockSpec((tm, bw), lambda j, i: (i, 0)) for _ in range(N_BRANCH)]
    return pl.pallas_call(
        _merge_kernel,
        grid=(nj, rows // tm),
        in_specs=g_specs + y_specs + [pl.BlockSpec((None, N_BRANCH, bw, tn), lambda j, i: (layer, 0, 0, j))],
        out_specs=pl.BlockSpec((tm, tn), lambda j, i: (i, j)),
        out_shape=jax.ShapeDtypeStruct((rows, d), BF16),
        scratch_shapes=[pltpu.VMEM((N_BRANCH, bw, tn), BF16)],
        compiler_params=_params("arbitrary", "arbitrary"),
        name="merge_branches",
    )(gates, gates, gates, gates, *ys, w_branch)


def _rms(x, eps=NORM_EPS):
    return x * lax.rsqrt(jnp.mean(x * x, axis=-1, keepdims=True) + eps)


def _norm_mod_kernel(x_ref, g_ref, shift_ref, scale_ref, o_ref):
    x = x_ref[...].astype(F32)
    y = _rms(x) * g_ref[...]
    o_ref[...] = (y * (1.0 + scale_ref[...]) + shift_ref[...]).astype(o_ref.dtype)


def _group_of_rows(tm, rows_per_group, n_groups):
    return lambda i: jnp.minimum((i * tm) // rows_per_group, n_groups - 1)


def norm_modulate(x, g, mod, shift_idx, scale_idx, *, rows, rows_per_group, tm=PREP_ROWS):
    d = x.shape[1]
    tm = min(tm, rows_per_group)
    assert rows % tm == 0 and rows_per_group % tm == 0
    grp = _group_of_rows(tm, rows_per_group, mod.shape[0])
    return pl.pallas_call(
        _norm_mod_kernel,
        grid=(rows // tm,),
        in_specs=[pl.BlockSpec((tm, d), lambda i: (i, 0)),
                  pl.BlockSpec((1, d), lambda i: (0, 0)),
                  pl.BlockSpec((None, 1, d), lambda i: (grp(i), 0, shift_idx)),
                  pl.BlockSpec((None, 1, d), lambda i: (grp(i), 0, scale_idx))],
        out_specs=pl.BlockSpec((tm, d), lambda i: (i, 0)),
        out_shape=jax.ShapeDtypeStruct((rows, d), BF16),
        compiler_params=_params("arbitrary"),
        name="norm_modulate",
    )(x, g.reshape(1, d), mod, mod)


def _norm_res_kernel(x_ref, y_ref, g_ref, m_ref, *rest):
    y = _rms(y_ref[...].astype(F32)) * g_ref[...]
    x = x_ref[...] + m_ref[...] * y
    if len(rest) == 1:
        rest[0][...] = x
        return
    g2_ref, shift_ref, scale_ref, o_ref, h_ref = rest
    o_ref[...] = x
    h_ref[...] = (_rms(x) * g2_ref[...] * (1.0 + scale_ref[...]) + shift_ref[...]).astype(h_ref.dtype)


def norm_residual(x, y, g, mod, gate_idx, *, rows, rows_per_group, tm=PREP_ROWS, next_norm=None):
    d = x.shape[1]
    tm = min(tm, rows_per_group)
    assert rows % tm == 0 and rows_per_group % tm == 0
    grp = _group_of_rows(tm, rows_per_group, mod.shape[0])
    tile = pl.BlockSpec((tm, d), lambda i: (i, 0))
    vec = pl.BlockSpec((1, d), lambda i: (0, 0))
    mod_spec = lambda idx: pl.BlockSpec((None, 1, d), lambda i: (grp(i), 0, idx))
    in_specs = [tile, tile, vec, mod_spec(gate_idx)]
    args = [x, y, g.reshape(1, d), mod]
    out_specs, out_shape = tile, jax.ShapeDtypeStruct((rows, d), F32)
    if next_norm is not None:
        g2, mod2, shift_idx, scale_idx = next_norm
        in_specs += [vec, mod_spec(shift_idx), mod_spec(scale_idx)]
        args += [g2.reshape(1, d), mod2, mod2]
        out_specs, out_shape = [tile, tile], [out_shape, jax.ShapeDtypeStruct((rows, d), BF16)]
    return pl.pallas_call(
        _norm_res_kernel,
        grid=(rows // tm,),
        in_specs=in_specs,
        out_specs=out_specs,
        out_shape=out_shape,
        compiler_params=_params("arbitrary"),
        name="norm_residual",
    )(*args)


class _Rows:
    def __init__(self, batch, n_lat, n_ctx):
        self.batch, self.n_lat, self.n_ctx = batch, n_lat, n_ctx
        self.m_lat = batch * n_lat
        self.m = batch * (n_lat + n_ctx)

    def split(self, t):
        c = t.shape[-1]
        return (t[:self.m_lat].reshape(self.batch, self.n_lat, c),
                t[self.m_lat:].reshape(self.batch, self.n_ctx, c))

    def join(self, lat, ctx):
        c = lat.shape[-1]
        return jnp.concatenate([lat.reshape(self.m_lat, c), ctx.reshape(self.batch * self.n_ctx, c)], axis=0)

    def seq_edges(self, i, tm):
        r0 = i * tm
        in_lat = r0 < self.m_lat
        pos = jnp.where(in_lat, r0 % self.n_lat, (r0 - self.m_lat) % self.n_ctx)
        length = jnp.where(in_lat, self.n_lat, self.n_ctx)
        return pos == 0, pos + tm == length

    def chunk_block(self, chunk):
        ncc, nlc = self.n_ctx // chunk, self.n_lat // chunk
        ctx0 = self.m_lat // chunk

        def block(d, b, c):
            in_ctx = c < ncc
            cc = jnp.where(in_ctx, c, c - ncc)
            n_seg = jnp.where(in_ctx, ncc, nlc)
            cc = jnp.where(d == 1, n_seg - 1 - cc, cc)
            return jnp.where(in_ctx, ctx0 + b * ncc, b * nlc) + cc
        return block, ncc + nlc


def _conv3_tile(x, prev_row, next_row, w_ref, first, last):
    tm = x.shape[0]
    rid = lax.broadcasted_iota(jnp.int32, x.shape, 0)
    prev_row = jnp.where(first, 0.0, prev_row)
    next_row = jnp.where(last, 0.0, next_row)
    prev = jnp.where(rid == 0, prev_row, pltpu.roll(x, 1, 0))
    nxt = jnp.where(rid == tm - 1, next_row, pltpu.roll(x, tm - 1, 0))
    return prev * w_ref[0:1, :] + x * w_ref[1:2, :] + nxt * w_ref[2:3, :]


def _halo_specs(tm, width, col_block, n_rows):
    per = tm // SUBLANES
    last = n_rows // SUBLANES - 1
    return [pl.BlockSpec((SUBLANES, width), lambda i: (jnp.maximum(i * per - 1, 0), col_block)),
            pl.BlockSpec((SUBLANES, width), lambda i: (jnp.minimum((i + 1) * per, last), col_block))]


def _chunk_cumsum_matrix(tm, chunk, reverse):
    r = lax.broadcasted_iota(jnp.int32, (tm, tm), 0)
    c = lax.broadcasted_iota(jnp.int32, (tm, tm), 1)
    same = (r // chunk) == (c // chunk)
    tri = (r <= c) if reverse else (r >= c)
    return jnp.logical_and(same, tri).astype(BF16)


assert RWKV_CHUNK * math.exp(-0.5) < 80.0

def _rwkv_prep_kernel(rkv_ref, hp_ref, hn_ref, low_ref, shift_ref, w0_ref, a0_ref, wup_ref, aup_ref, gup_ref,
                      kk_ref, ka_ref, rk_ref, e_ref,
                      at_ref, rt_ref, bt_ref, kt_ref, bc_ref, kc_ref, gend_ref, v_ref, gate_ref, bonus_ref,
                      *, rows, chunk):
    tm = rkv_ref.shape[0]
    bw = v_ref.shape[1]
    first, last = rows.seq_edges(pl.program_id(0), tm)
    rkv = _conv3_tile(rkv_ref[...], hp_ref[SUBLANES - 1:SUBLANES, :], hn_ref[0:1, :], shift_ref, first, last)
    r, k, v = rkv[:, :bw], rkv[:, bw:2 * bw], rkv[:, 2 * bw:]
    e = e_ref[...]
    kk = k * kk_ref[...]
    kk = kk * lax.rsqrt(_dot_split_lhs(kk * kk, e) + 1e-12)
    low = low_ref[...]
    wd = low[:, :RWKV_DECAY_RANK]
    ad = low[:, RWKV_DECAY_RANK:RWKV_DECAY_RANK + RWKV_AAA_RANK]
    gd = low[:, RWKV_DECAY_RANK + RWKV_AAA_RANK:]
    w_pre = _dot3(jnp.tanh(wd), wup_ref[...])
    a_pre = _dot3(ad, aup_ref[...])
    gate_ref[...] = _dot3(jax.nn.sigmoid(gd), gup_ref[...])
    v_ref[...] = v.astype(BF16)
    kt_sum = None
    for d in range(2):
        w_log = -jax.nn.softplus(-(w0_ref[d:d + 1, :] + w_pre[:, d * bw:(d + 1) * bw])) - 0.5
        lw = -jnp.exp(w_log)
        a = jax.nn.sigmoid(a0_ref[d:d + 1, :] + a_pre[:, d * bw:(d + 1) * bw])
        kt = k * (1.0 + (a - 1.0) * ka_ref[...])
        kt_sum = kt if kt_sum is None else kt_sum + kt
        b = kk * a
        lam = _dot_split_rhs(_chunk_cumsum_matrix(tm, chunk, d == 1), lw)
        g_inv = jnp.exp(-lam)
        at_ref[d] = (-kk * jnp.exp(lam - lw)).astype(BF16)
        rt_ref[d] = (r * jnp.exp(lam)).astype(BF16)
        bt_ref[d] = (b * g_inv).astype(BF16)
        kt_ref[d] = (kt * g_inv).astype(BF16)
        for q in range(tm // chunk):
            end = q * chunk if d == 1 else (q + 1) * chunk - 1
            lam_end = lam[end:end + 1, :]
            sl = slice(q * chunk, (q + 1) * chunk)
            to_end = jnp.exp(lam_end - lam[sl])
            bc_ref[d, sl, :] = (b[sl] * to_end).astype(BF16)
            kc_ref[d, sl, :] = (kt[sl] * to_end).astype(BF16)
            gend_ref[d, q] = jnp.exp(lam_end)
    bonus_ref[...] = _dot_split_lhs(r * kt_sum * rk_ref[...], e, terms=1) * v


def rwkv_prep(pa, rows, layer, shift, w0, w_up, a0, a_up, g_up, k_k, k_a, r_k, seg_ones, *, tm=PREP_ROWS,
              chunk=RWKV_CHUNK):
    m = pa.shape[0]
    bw = k_k.shape[-1]
    low_w = RWKV_DECAY_RANK + RWKV_AAA_RANK + RWKV_GATE_RANK
    tm = min(tm, rows.n_ctx)
    assert m % tm == 0 and rows.n_ctx % tm == 0 and rows.n_lat % tm == 0 and tm % chunk == 0
    assert (3 * bw) % low_w == 0
    wup = jnp.concatenate([w_up[layer, 0], w_up[layer, 1]], axis=1)
    aup = jnp.concatenate([a_up[layer, 0], a_up[layer, 1]], axis=1)
    row = lambda t: t.reshape(1, bw)
    big = jax.ShapeDtypeStruct((2, m, bw), BF16)
    big_spec = pl.BlockSpec((2, tm, bw), lambda i: (0, i, 0))
    one_spec = pl.BlockSpec((tm, bw), lambda i: (i, 0))
    cpt = tm // chunk
    return pl.pallas_call(
        functools.partial(_rwkv_prep_kernel, rows=rows, chunk=chunk),
        grid=(m // tm,),
        in_specs=[pl.BlockSpec((tm, 3 * bw), lambda i: (i, 0))] + _halo_specs(tm, 3 * bw, 0, m)
        + [pl.BlockSpec((tm, low_w), lambda i: (i, 3 * bw // low_w)),
           _full((3, 3 * bw)), _full((2, bw)), _full((2, bw)), _full(wup.shape), _full(aup.shape),
           pl.BlockSpec((None,) + g_up.shape[1:], lambda i: (layer, 0, 0)),
           _full((1, bw)), _full((1, bw)), _full((1, bw)), _full(seg_ones.shape)],
        out_specs=[big_spec] * 6 + [pl.BlockSpec((2, cpt, 1, bw), lambda i: (0, i, 0, 0)),
                                    one_spec, one_spec, one_spec],
        out_shape=[big] * 6 + [jax.ShapeDtypeStruct((2, m // chunk, 1, bw), F32),
                               jax.ShapeDtypeStruct((m, bw), BF16), jax.ShapeDtypeStruct((m, bw), F32),
                               jax.ShapeDtypeStruct((m, bw), F32)],
        compiler_params=_params("arbitrary"),
        name="rwkv_prep",
    )(pa, pa, pa, pa, shift[layer], w0[layer], a0[layer], wup, aup, g_up, row(k_k[layer]), row(k_a[layer]),
      row(r_k[layer]), seg_ones)


def _pair_select(lo, z):
    half = z.shape[0] // 2
    return jnp.where(lo, z[:half], z[half:])


def _rwkv_scan_kernel(at_ref, rt_ref, bt_ref, kt_ref, bc_ref, kc_ref, gend_ref, v_ref, y_ref, s_ref, *, chunk):
    @pl.when(pl.program_id(2) == 0)
    def _():
        s_ref[...] = jnp.zeros_like(s_ref)

    c2 = 2 * chunk
    n_pairs = s_ref.shape[0]
    pw = 2 * RWKV_HEAD
    n_levels = chunk.bit_length() - 1
    order = lax.broadcasted_iota(jnp.int32, (chunk, chunk), 0) - lax.broadcasted_iota(jnp.int32, (chunk, chunk), 1)
    order = jnp.where(pl.program_id(0) == 1, -order, order)
    strict = order > 0
    incl = order >= 0
    lo = lax.broadcasted_iota(jnp.int32, (chunk, pw), 1) < RWKV_HEAD
    lo2 = lax.broadcasted_iota(jnp.int32, (c2, pw), 1) < RWKV_HEAD
    own = ((lax.broadcasted_iota(jnp.int32, (pw, pw), 0) < RWKV_HEAD)
           == (lax.broadcasted_iota(jnp.int32, (pw, pw), 1) < RWKV_HEAD))
    pairs = range(n_pairs)
    sl = [slice(j * pw, (j + 1) * pw) for j in pairs]

    def tri(mask, z):
        return jnp.where(mask, z, 0.0)

    a = [at_ref[:, s] for s in sl]
    r = [rt_ref[:, s] for s in sl]
    v = [v_ref[:, s] for s in sl]
    s0 = [s_ref[j] for j in pairs]
    ar = [jnp.concatenate([a[j], r[j]], axis=0) for j in pairs]
    ar_st = [jnp.concatenate([jnp.where(lo2, ar[j], 0), jnp.where(lo2, 0, ar[j])], axis=0) for j in pairs]
    p_b = [_dot_nt(ar_st[j], bt_ref[:, sl[j]]) for j in pairs]
    p_k = [_dot_nt(ar_st[j], kt_ref[:, sl[j]]) for j in pairs]
    lp = [[tri(strict, p_b[j][e * c2:e * c2 + chunk]) for e in range(2)] for j in pairs]
    l_ak = [jnp.concatenate([tri(strict, p_k[j][e * c2:e * c2 + chunk]) for e in range(2)], axis=0).astype(BF16)
            for j in pairs]
    m_rb = [jnp.concatenate([tri(incl, p_b[j][e * c2 + chunk:(e + 1) * c2]) for e in range(2)], axis=0).astype(BF16)
            for j in pairs]
    m_rk = [jnp.concatenate([tri(incl, p_k[j][e * c2 + chunk:(e + 1) * c2]) for e in range(2)], axis=0).astype(BF16)
            for j in pairs]
    x_w = [a[j].astype(F32) for j in pairs]
    x_u = [_pair_select(lo, _dot(l_ak[j], v[j])) for j in pairs]
    for lvl in range(n_levels):
        lpb = [[lp[j][e].astype(BF16) for e in range(2)] for j in pairs]
        lst = [jnp.concatenate(lpb[j], axis=0) for j in pairs]
        z = [_dot(lst[j], jnp.concatenate([x_w[j], x_u[j]], axis=1).astype(BF16)) for j in pairs]
        x_w = [x_w[j] + _pair_select(lo, z[j][:, :pw]) for j in pairs]
        x_u = [x_u[j] + _pair_select(lo, z[j][:, pw:]) for j in pairs]
        if lvl + 1 < n_levels:
            lp = [[_dot(lpb[j][e], lpb[j][e]) for e in range(2)] for j in pairs]
    s0b = [s0[j].astype(BF16) for j in pairs]
    u = [_dot_nt(x_w[j].astype(BF16), s0b[j]) + x_u[j] for j in pairs]
    ub = [u[j].astype(BF16) for j in pairs]
    y = [_dot_nt(r[j], s0b[j]) + _pair_select(lo, _dot(m_rb[j], ub[j])) + _pair_select(lo, _dot(m_rk[j], v[j]))
         for j in pairs]
    s_new = [s0[j] * gend_ref[:, sl[j]]
             + jnp.where(own, _dot_tn(ub[j], bc_ref[:, sl[j]]) + _dot_tn(v[j], kc_ref[:, sl[j]]), 0.0)
             for j in pairs]
    for j in pairs:
        y_ref[:, sl[j]] = y[j]
    for j in pairs:
        s_ref[j] = s_new[j]


def rwkv_scan(ops, v, rows, *, chunk=RWKV_CHUNK):
    at, rt, bt, kt, bc, kc, gend = ops
    _, m, bw = at.shape
    block, nc = rows.chunk_block(chunk)
    big = pl.BlockSpec((None, chunk, bw), lambda d, b, c: (d, block(d, b, c), 0))
    return pl.pallas_call(
        functools.partial(_rwkv_scan_kernel, chunk=chunk),
        grid=(2, rows.batch, nc),
        in_specs=[big] * 6 + [pl.BlockSpec((None, None, 1, bw), lambda d, b, c: (d, block(d, b, c), 0, 0)),
                              pl.BlockSpec((chunk, bw), lambda d, b, c: (block(d, b, c), 0))],
        out_specs=big,
        out_shape=jax.ShapeDtypeStruct((2, m, bw), F32),
        scratch_shapes=[pltpu.VMEM((bw // (2 * RWKV_HEAD), 2 * RWKV_HEAD, 2 * RWKV_HEAD), F32)],
        compiler_params=_params("arbitrary", "arbitrary", "arbitrary"),
        name="rwkv_scan",
    )(at, rt, bt, kt, bc, kc, gend, v)


def _rwkv_readout_kernel(y_ref, bonus_ref, gate_ref, e_ref, g_ref, b_ref, o_ref):
    y = y_ref[0] + y_ref[1]
    e = e_ref[...]
    mu = _dot_split_lhs(y, e) * (1.0 / RWKV_HEAD)
    yc = y - mu
    var = _dot_split_lhs(yc * yc, e) * (1.0 / RWKV_HEAD)
    yn = yc * lax.rsqrt(var + RWKV_LN_EPS) * g_ref[...] + b_ref[...]
    o_ref[...] = ((yn + bonus_ref[...]) * gate_ref[...]).astype(o_ref.dtype)


def rwkv_readout(y, bonus, gate, seg_ones, ln_g, ln_b, *, rows, tm=PREP_ROWS):
    bw = y.shape[-1]
    tm = math.gcd(tm, rows)
    one = pl.BlockSpec((tm, bw), lambda i: (i, 0))
    return pl.pallas_call(
        _rwkv_readout_kernel,
        grid=(rows // tm,),
        in_specs=[pl.BlockSpec((2, tm, bw), lambda i: (0, i, 0)), one, one, _full(seg_ones.shape),
                  _full((1, bw)), _full((1, bw))],
        out_specs=one,
        out_shape=jax.ShapeDtypeStruct((rows, bw), BF16),
        compiler_params=_params("arbitrary"),
        name="rwkv_readout",
    )(y, bonus, gate, seg_ones, ln_g.reshape(1, bw), ln_b.reshape(1, bw))


def _ssd_prep_kernel(x_ref, hp_ref, hn_ref, w_ref, b_ref, o_ref, *, rows):
    first, last = rows.seq_edges(pl.program_id(0), x_ref.shape[0])
    y = _conv3_tile(x_ref[...], hp_ref[SUBLANES - 1:SUBLANES, :], hn_ref[0:1, :], w_ref, first, last) + b_ref[...]
    o_ref[...] = y * jax.nn.sigmoid(y)


def ssd_prep(pc, rows, conv_w, conv_b, *, width, first_block, n_blocks, tm=PREP_ROWS):
    m = pc.shape[0]
    tm = min(tm, rows.n_ctx)
    assert m % tm == 0 and rows.n_ctx % tm == 0 and rows.n_lat % tm == 0
    per = tm // SUBLANES
    last = m // SUBLANES - 1
    return pl.pallas_call(
        functools.partial(_ssd_prep_kernel, rows=rows),
        grid=(m // tm, n_blocks),
        in_specs=[pl.BlockSpec((tm, width), lambda i, j: (i, first_block + j)),
                  pl.BlockSpec((SUBLANES, width), lambda i, j: (jnp.maximum(i * per - 1, 0), first_block + j)),
                  pl.BlockSpec((SUBLANES, width), lambda i, j: (jnp.minimum((i + 1) * per, last), first_block + j)),
                  pl.BlockSpec((3, width), lambda i, j: (0, j)),
                  pl.BlockSpec((1, width), lambda i, j: (0, j))],
        out_specs=pl.BlockSpec((tm, width), lambda i, j: (i, j)),
        out_shape=jax.ShapeDtypeStruct((m, n_blocks * width), F32),
        compiler_params=_params("arbitrary", "arbitrary"),
        name="ssd_prep",
    )(pc, pc, pc, conv_w, conv_b.reshape(1, n_blocks * width))


def _ssd_scan_kernel(x_ref, b_ref, c_ref, la_ref, lat_ref, dt_ref, y_ref, s_ref, *, chunk, heads, groups):
    @pl.when(pl.program_id(2) == 0)
    def _():
        s_ref[...] = jnp.zeros_like(s_ref)

    rev = pl.program_id(0) == 1
    pw = 2 * SSD_HEAD
    order = lax.broadcasted_iota(jnp.int32, (chunk, chunk), 0) - lax.broadcasted_iota(jnp.int32, (chunk, chunk), 1)
    order = jnp.where(rev, -order, order)
    upto = order >= 0
    lo = lax.broadcasted_iota(jnp.int32, (chunk, pw), 1) < SSD_HEAD
    acs_c_all = _dot_hi(upto.astype(F32), la_ref[...])
    acs_r_all = _dot_hi(lat_ref[...], (order <= 0).astype(F32))
    tot_c_all = jnp.where(rev, acs_c_all[0:1, :], acs_c_all[chunk - 1:chunk, :])
    dt_all = dt_ref[...]
    hpg = heads // groups
    for gi in range(groups):
        cg = c_ref[:, gi * SSD_STATE:(gi + 1) * SSD_STATE].astype(BF16)
        bg = b_ref[:, gi * SSD_STATE:(gi + 1) * SSD_STATE].astype(BF16)
        cb = _dot_nt(cg, bg)
        for qi in range(hpg // 2):
            h0 = gi * hpg + 2 * qi
            j = h0 // 2
            cols = slice(j * pw, (j + 1) * pw)
            w_st, acs_c, tot = [], [], []
            for e in range(2):
                h = h0 + e
                acs_c.append(acs_c_all[:, h:h + 1])
                tot.append(tot_c_all[:, h:h + 1])
                decay = jnp.exp(jnp.where(upto, acs_c[e] - acs_r_all[h:h + 1, :], -1e30))
                w_st.append((cb * decay).astype(BF16))
            xdt = x_ref[:, cols] * jnp.where(lo, dt_all[:, h0:h0 + 1], dt_all[:, h0 + 1:h0 + 2])
            st = s_ref[j]
            y = _pair_select(lo, _dot(jnp.concatenate(w_st, axis=0), xdt.astype(BF16)))
            y = y + _dot(cg, st.astype(BF16)) * jnp.where(lo, jnp.exp(acs_c[0]), jnp.exp(acs_c[1]))
            y_ref[:, cols] = y
            to_end = jnp.where(lo, jnp.exp(tot[0] - acs_c[0]), jnp.exp(tot[1] - acs_c[1]))
            s_ref[j] = (st * jnp.where(lo[0:1, :], jnp.exp(tot[0]), jnp.exp(tot[1]))
                        + _dot_tn(bg, (xdt * to_end).astype(BF16)))


def ssd_scan(xbc, la, dt, rows, *, heads, chunk=SSD_CHUNK):
    m = xbc.shape[0]
    bw = heads * SSD_HEAD
    gn = SSD_GROUPS * SSD_STATE
    assert bw % gn == 0
    block, nc = rows.chunk_block(chunk)
    lat = jnp.swapaxes(la, 1, 2)
    thin = pl.BlockSpec((None, chunk, heads), lambda d, b, c: (d, block(d, b, c), 0))
    return pl.pallas_call(
        functools.partial(_ssd_scan_kernel, chunk=chunk, heads=heads, groups=SSD_GROUPS),
        grid=(2, rows.batch, nc),
        in_specs=[pl.BlockSpec((chunk, bw), lambda d, b, c: (block(d, b, c), 0)),
                  pl.BlockSpec((chunk, gn), lambda d, b, c: (block(d, b, c), bw // gn)),
                  pl.BlockSpec((chunk, gn), lambda d, b, c: (block(d, b, c), bw // gn + 1)),
                  thin,
                  pl.BlockSpec((None, heads, chunk), lambda d, b, c: (d, 0, block(d, b, c))),
                  thin],
        out_specs=pl.BlockSpec((None, chunk, bw), lambda d, b, c: (d, block(d, b, c), 0)),
        out_shape=jax.ShapeDtypeStruct((2, m, bw), F32),
        scratch_shapes=[pltpu.VMEM((heads // 2, SSD_STATE, 2 * SSD_HEAD), F32)],
        compiler_params=_params("arbitrary", "arbitrary", "arbitrary"),
        name="ssd_scan",
    )(xbc, xbc, xbc, la, lat, dt)


def _ssd_readout_kernel(y_ref, x_ref, z_ref, d_ref, g_ref, o_ref):
    y = y_ref[0] + y_ref[1] + d_ref[...] * x_ref[...]
    z = z_ref[...]
    o_ref[...] = (_rms(y * (z * jax.nn.sigmoid(z))) * g_ref[...]).astype(o_ref.dtype)


def ssd_readout(y, xbc, pc, d_skip, norm_g, *, rows, tm=PREP_ROWS):
    bw = y.shape[-1]
    tm = math.gcd(tm, rows)
    one = pl.BlockSpec((tm, bw), lambda i: (i, 0))
    return pl.pallas_call(
        _ssd_readout_kernel,
        grid=(rows // tm,),
        in_specs=[pl.BlockSpec((2, tm, bw), lambda i: (0, i, 0)), one, one, _full((1, bw)), _full((1, bw))],
        out_specs=one,
        out_shape=jax.ShapeDtypeStruct((rows, bw), BF16),
        compiler_params=_params("arbitrary"),
        name="ssd_readout",
    )(y, xbc, pc, jnp.repeat(d_skip, SSD_HEAD).reshape(1, bw), norm_g.reshape(1, bw))


def _softmax_pv(scores, values):
    m = None
    for s in scores:
        ms = jnp.max(s, axis=-1, keepdims=True)
        m = ms if m is None else jnp.maximum(m, ms)
    l, o = None, None
    for s, v in zip(scores, values):
        p = jnp.exp(s - m)
        ls = jnp.sum(p, axis=-1, keepdims=True)
        os_ = _dot(p.astype(BF16), v)
        l = ls if l is None else l + ls
        o = os_ if o is None else o + os_
    return o / l


def _attn_kernel(lam_ref, q_ref, g_ref, *refs, n_seg, scale, diff, out_scale, row_parts):
    seg_refs = refs[:2 * n_seg]
    o_ref = refs[2 * n_seg]
    values = [seg_refs[2 * si + 1][...] for si in range(n_seg)]
    rows_per = q_ref.shape[0] // row_parts
    for r in range(row_parts):
        rs = slice(r * rows_per, (r + 1) * rows_per)
        q = q_ref[rs, :].astype(F32) * scale
        if not diff:
            qb = q.astype(BF16)
            scores = [_dot_nt(qb, seg_refs[2 * si][...]) for si in range(n_seg)]
            o_ref[rs, :] = _softmax_pv(scores, values).astype(o_ref.dtype)
            continue
        first = lax.broadcasted_iota(jnp.int32, q.shape, 1) < DIFF_HEAD
        q1 = jnp.where(first, q, 0.0).astype(BF16)
        q2 = jnp.where(first, 0.0, q).astype(BF16)
        o1 = _softmax_pv([_dot_nt(q1, seg_refs[2 * si][...]) for si in range(n_seg)], values)
        o2 = _softmax_pv([_dot_nt(q2, seg_refs[2 * si][...]) for si in range(n_seg)], values)
        o = o1 - lam_ref[0] * o2
        o_ref[rs, :] = (_rms(o) * g_ref[...] * out_scale).astype(o_ref.dtype)


def attention(q, k, v, *, heads, batch, q_rows, q_first, segments, scale, tq=256, row_parts=1,
              diff_lam=None, diff_gain=None, out_scale=1.0):
    dqk = q.shape[1] // heads
    dv = v.shape[1] // heads
    tq = min(tq, q_rows)
    nq = q_rows // tq
    if tq % (row_parts * 2 * SUBLANES):
        row_parts = 1
    diff = diff_lam is not None
    lam = (diff_lam if diff else jnp.zeros((), F32)).reshape(1).astype(F32)
    gain = (diff_gain if diff else jnp.ones((dv,), F32)).reshape(1, dv).astype(F32)
    seg_specs = []
    for seg_rows, first in segments:
        for w in (dqk, dv):
            seg_specs.append(pl.BlockSpec((seg_rows, w), functools.partial(lambda b, h, i, f: (f + b, h), f=first)))
    return pl.pallas_call(
        functools.partial(_attn_kernel, n_seg=len(segments), scale=scale, diff=diff, out_scale=out_scale,
                          row_parts=row_parts),
        grid=(batch, heads, nq),
        in_specs=[pl.BlockSpec(memory_space=pltpu.SMEM),
                  pl.BlockSpec((tq, dqk), lambda b, h, i: ((q_first + b) * nq + i, h)),
                  pl.BlockSpec((1, dv), lambda b, h, i: (0, 0))] + seg_specs,
        out_specs=pl.BlockSpec((tq, dv), lambda b, h, i: (b * nq + i, h)),
        out_shape=jax.ShapeDtypeStruct((batch * q_rows, heads * dv), BF16),
        compiler_params=_params("arbitrary", "arbitrary", "arbitrary"),
        name="diff_attention" if diff else "mla_attention",
    )(lam, q, gain, *([k, v] * len(segments)))


ROPE_GROUP = 64
ROPE_QUARTER = ROPE_GROUP // 4


def _axial_rope_tables(n_tok):
    t = jnp.arange(n_tok)
    row = (t // GRID_W).astype(F32)
    col = (t % GRID_W).astype(F32)
    axis_dim = ROPE_GROUP // 2
    inv = 1.0 / (ROPE_BASE ** (jnp.arange(0, axis_dim, 2, dtype=F32) / axis_dim))
    ar = row[:, None] * inv[None]
    ac = col[:, None] * inv[None]
    ang = jnp.concatenate([ar, ar, ac, ac], axis=-1)
    sign = jnp.where((jnp.arange(ROPE_GROUP) % (2 * ROPE_QUARTER)) < ROPE_QUARTER, -1.0, 1.0)
    return jnp.cos(ang), jnp.sin(ang) * sign


def _rope_lanes(x, cos, sin_signed):
    lane = lax.broadcasted_iota(jnp.int32, x.shape, 1)
    takes_next = (lane % (2 * ROPE_QUARTER)) < ROPE_QUARTER
    rot = jnp.where(takes_next, pltpu.roll(x, LANES - ROPE_QUARTER, 1), pltpu.roll(x, ROPE_QUARTER, 1))
    return x * cos + rot * sin_signed


def _rope_table_spec(rows, tm):
    per_seq = rows.n_lat // tm
    return pl.BlockSpec((tm, LANES), lambda i: (jnp.where(i * tm < rows.m_lat, i % per_seq, 0), 0))


def _tile_tables(i, tm, rows, cos_ref, sin_ref):
    is_lat = i * tm < rows.m_lat
    return jnp.where(is_lat, cos_ref[...], 1.0), jnp.where(is_lat, sin_ref[...], 0.0)


def _diff_prep_kernel(q_ref, k_ref, v_ref, cos_ref, sin_ref, qo_ref, ko_ref, vo_ref, *, rows):
    tm = q_ref.shape[0]
    cos, sin = _tile_tables(pl.program_id(0), tm, rows, cos_ref, sin_ref)
    for src, dst in ((q_ref, qo_ref), (k_ref, ko_ref)):
        for cb in range(src.shape[1] // LANES):
            cols = slice(cb * LANES, (cb + 1) * LANES)
            dst[:, cols] = _rope_lanes(src[:, cols], cos, sin).astype(dst.dtype)
    vo_ref[...] = v_ref[...].astype(vo_ref.dtype)


def diff_prep(pd, rows, cos2, sin2, *, tm=PREP_ROWS):
    m = pd.shape[0]
    bw = pd.shape[1] // 3
    tm = min(tm, rows.n_ctx)
    assert m % tm == 0 and rows.n_ctx % tm == 0 and rows.n_lat % tm == 0
    out = jax.ShapeDtypeStruct((m, bw), BF16)
    one = pl.BlockSpec((tm, bw), lambda i: (i, 0))
    tab = _rope_table_spec(rows, tm)
    return pl.pallas_call(
        functools.partial(_diff_prep_kernel, rows=rows),
        grid=(m // tm,),
        in_specs=[pl.BlockSpec((tm, bw), functools.partial(lambda i, j: (i, j), j=j)) for j in range(3)] + [tab, tab],
        out_specs=[one] * 3,
        out_shape=[out] * 3,
        compiler_params=_params("arbitrary"),
        name="diff_prep",
    )(pd, pd, pd, cos2, sin2)


def _mla_q_kernel(cq_ref, g_ref, w_ref, cos_ref, sin_ref, o_ref, *, rows):
    tm = cq_ref.shape[0]
    cos, sin = _tile_tables(pl.program_id(0), tm, rows, cos_ref, sin_ref)
    q = _dot((_rms(cq_ref[...]) * g_ref[...]).astype(BF16), w_ref[...])
    for h in range(MLA_HEADS):
        nope = slice(2 * h * LANES, (2 * h + 1) * LANES)
        rope = slice((2 * h + 1) * LANES, (2 * h + 2) * LANES)
        o_ref[:, nope] = q[:, nope].astype(o_ref.dtype)
        o_ref[:, rope] = _rope_lanes(q[:, rope], cos, sin).astype(o_ref.dtype)


def mla_q_prep(pb, rows, q_norm, w_uq, cos2, sin2, *, tm=PREP_ROWS):
    m = pb.shape[0]
    tm = min(tm, rows.n_ctx)
    w = w_uq.reshape(MLA_Q_RANK, MLA_HEADS, MLA_NOPE + MLA_ROPE)
    w = jnp.pad(w, ((0, 0), (0, 0), (0, 2 * LANES - MLA_NOPE - MLA_ROPE))).reshape(MLA_Q_RANK, -1).astype(BF16)
    tab = _rope_table_spec(rows, tm)
    return pl.pallas_call(
        functools.partial(_mla_q_kernel, rows=rows),
        grid=(m // tm,),
        in_specs=[pl.BlockSpec((tm, MLA_Q_RANK), lambda i: (i, 0)), _full((1, MLA_Q_RANK)), _full(w.shape), tab, tab],
        out_specs=pl.BlockSpec((tm, w.shape[1]), lambda i: (i, 0)),
        out_shape=jax.ShapeDtypeStruct((m, w.shape[1]), BF16),
        compiler_params=_params("arbitrary"),
        name="mla_q_prep",
    )(pb, q_norm.reshape(1, MLA_Q_RANK), w, cos2, sin2)


def _mla_kv_kernel(ckv_ref, kr_ref, g_ref, w_ref, cos_ref, sin_ref, k_ref, v_ref, *, rows):
    tm = ckv_ref.shape[0]
    cos, sin = _tile_tables(pl.program_id(0), tm, rows, cos_ref, sin_ref)
    kv = _dot((_rms(ckv_ref[...]) * g_ref[...]).astype(BF16), w_ref[...])
    lane = lax.broadcasted_iota(jnp.int32, (tm, LANES), 1)
    kr = jnp.where(lane < MLA_ROPE, _rope_lanes(kr_ref[...], cos, sin), 0.0).astype(k_ref.dtype)
    nope_w = MLA_HEADS * MLA_NOPE
    for h in range(MLA_HEADS):
        k_ref[:, 2 * h * LANES:(2 * h + 1) * LANES] = kv[:, h * MLA_NOPE:(h + 1) * MLA_NOPE].astype(k_ref.dtype)
        k_ref[:, (2 * h + 1) * LANES:(2 * h + 2) * LANES] = kr
    v_ref[...] = kv[:, nope_w:].astype(v_ref.dtype)


def mla_kv_prep(pb, rows, kv_norm, w_ukv, cos2, sin2, *, tm=PREP_ROWS):
    m = pb.shape[0]
    tm = min(tm, rows.n_ctx)
    assert MLA_NOPE == LANES and MLA_Q_RANK % MLA_KV_RANK == 0 and (MLA_Q_RANK + MLA_KV_RANK) % LANES == 0
    w = w_ukv.reshape(MLA_KV_RANK, MLA_HEADS, 2 * MLA_NOPE)
    w = jnp.concatenate([w[:, :, :MLA_NOPE].reshape(MLA_KV_RANK, -1), w[:, :, MLA_NOPE:].reshape(MLA_KV_RANK, -1)],
                        axis=1).astype(BF16)
    nope_w = MLA_HEADS * MLA_NOPE
    tab = _rope_table_spec(rows, tm)
    return pl.pallas_call(
        functools.partial(_mla_kv_kernel, rows=rows),
        grid=(m // tm,),
        in_specs=[pl.BlockSpec((tm, MLA_KV_RANK), lambda i: (i, MLA_Q_RANK // MLA_KV_RANK)),
                  pl.BlockSpec((tm, LANES), lambda i: (i, (MLA_Q_RANK + MLA_KV_RANK) // LANES)),
                  _full((1, MLA_KV_RANK)), _full(w.shape), tab, tab],
        out_specs=[pl.BlockSpec((tm, 2 * nope_w), lambda i: (i, 0)), pl.BlockSpec((tm, nope_w), lambda i: (i, 0))],
        out_shape=[jax.ShapeDtypeStruct((m, 2 * nope_w), BF16), jax.ShapeDtypeStruct((m, nope_w), BF16)],
        compiler_params=_params("arbitrary"),
        name="mla_kv_prep",
    )(pb, pb, kv_norm.reshape(1, MLA_KV_RANK), w, cos2, sin2)


def kernel(x, c, ctx, c_ctx, ada_w, ada_b, norm_mix_pre, norm_mix_post, norm_ffn_pre, norm_ffn_post, w_in, rwkv_shift, rwkv_w0, rwkv_w_up, rwkv_a0, rwkv_a_up, rwkv_g_up, rwkv_k_k, rwkv_k_a, rwkv_r_k, rwkv_ln_g, rwkv_ln_b, mla_q_norm, mla_w_uq, mla_kv_norm, mla_w_ukv, ssd_conv_w, ssd_conv_b, ssd_dt_bias, ssd_a_log, ssd_d, ssd_norm, diff_lq1, diff_lk1, diff_lq2, diff_lk2, diff_subln, w_branch, w_gate, w_out, w_ff1, w_ff2):
    batch, n_lat, d = x.shape
    n_ctx = ctx.shape[1]
    depth = ada_w.shape[0]
    bw = d // N_BRANCH
    rows = _Rows(batch, n_lat, n_ctx)
    m_lat, m_all = rows.m_lat, rows.m
    ssd_heads = bw // SSD_HEAD
    gn = SSD_GROUPS * SSD_STATE
    rwkv_in = 3 * bw + RWKV_DECAY_RANK + RWKV_AAA_RANK + RWKV_GATE_RANK
    mla_in = MLA_Q_RANK + MLA_KV_RANK + MLA_ROPE
    ssd_in = 2 * bw + 2 * gn + ssd_heads
    diff_in = 3 * bw
    o_mla, o_ssd, o_diff = rwkv_in, rwkv_in + mla_in, rwkv_in + mla_in + ssd_in
    assert w_in.shape[-1] == o_diff + diff_in
    assert n_lat % SSD_CHUNK == 0 and n_ctx % SSD_CHUNK == 0

    assert MLA_ROPE == ROPE_GROUP and DIFF_HEAD == ROPE_GROUP
    cos2, sin2 = (jnp.tile(t, (1, LANES // ROPE_GROUP)) for t in _axial_rope_tables(n_lat))
    hid = jnp.arange(bw) // RWKV_HEAD
    seg_ones = (hid[:, None] == hid[None, :]).astype(BF16)

    xs = jnp.concatenate([x.reshape(m_lat, d), ctx.reshape(batch * n_ctx, d)], axis=0)
    cvec = jnp.zeros((8, d), F32).at[:batch].set(jax.nn.silu(c)).at[batch].set(jax.nn.silu(c_ctx))
    groups = batch + 1

    mods = [(matmul(cvec, ada_w, layer=l, tn=512) + ada_b[l])[:groups].reshape(groups, 1, 6 * d)
            for l in range(depth)]
    h = norm_modulate(xs, norm_mix_pre[0], mods[0], 0, 1, rows=m_all, rows_per_group=n_lat)

    for l in range(depth):
        ctx_out = l < depth - 1
        m_out = m_all if ctx_out else m_lat
        mod = mods[l]

        o_dt = o_diff - ssd_heads
        pa = matmul(h, w_in[l, :, :o_mla], tm=1024, tn=512)
        pb = matmul(h, jnp.concatenate([w_in[l, :, o_mla:o_ssd], w_in[l, :, o_dt:o_diff]], axis=1), tm=1024, tn=384)
        pc = matmul(h, w_in[l, :, o_ssd:o_dt])
        pd = matmul(h, w_in[l, :, o_diff:])

        prep = rwkv_prep(pa, rows, l, rwkv_shift, rwkv_w0, rwkv_w_up, rwkv_a0, rwkv_a_up, rwkv_g_up,
                         rwkv_k_k, rwkv_k_a, rwkv_r_k.reshape(depth, bw), seg_ones)
        y_rwkv = rwkv_scan(prep[:7], prep[7], rows)
        ya = rwkv_readout(y_rwkv, prep[9], prep[8], seg_ones, rwkv_ln_g[l], rwkv_ln_b[l], rows=m_out)

        q_m = mla_q_prep(pb, rows, mla_q_norm[l], mla_w_uq[l], cos2, sin2)
        k_m, vv = mla_kv_prep(pb, rows, mla_kv_norm[l], mla_w_ukv[l], cos2, sin2)
        ctx_blk = m_lat // n_ctx
        segs_lat = [(n_lat, 0), (n_ctx, ctx_blk)]
        mla_scale = (MLA_NOPE + MLA_ROPE) ** -0.5
        yb = attention(q_m, k_m, vv, heads=MLA_HEADS, batch=batch, q_rows=n_lat, q_first=0, segments=segs_lat,
                       scale=mla_scale, tq=512, row_parts=2)
        if ctx_out:
            yb_c = attention(q_m, k_m, vv, heads=MLA_HEADS, batch=batch, q_rows=n_ctx, q_first=ctx_blk,
                             segments=[(n_ctx, ctx_blk)], scale=mla_scale)
            yb = jnp.concatenate([yb, yb_c], axis=0)

        assert (2 * gn) % bw == 0
        xbc = ssd_prep(pc, rows, ssd_conv_w[l], ssd_conv_b[l], width=bw, first_block=1, n_blocks=1 + 2 * gn // bw)
        dt_raw = pb[:, mla_in:mla_in + ssd_heads]
        dts = jnp.stack([jax.nn.softplus(dt_raw + ssd_dt_bias[l, dr]) for dr in range(2)])
        las = dts * (-jnp.exp(ssd_a_log[l]))[:, None, :]
        y_ssd = ssd_scan(xbc, las, dts, rows, heads=ssd_heads)
        yc = ssd_readout(y_ssd, xbc, pc, ssd_d[l], ssd_norm[l], rows=m_out)

        q_d, k_d, v_d = diff_prep(pd, rows, cos2, sin2)
        lam_init = 0.8 - 0.6 * math.exp(-0.3 * l)
        lam = (jnp.exp(jnp.sum(diff_lq1[l] * diff_lk1[l])) - jnp.exp(jnp.sum(diff_lq2[l] * diff_lk2[l])) + lam_init)
        diff_kw = dict(heads=DIFF_HEADS, batch=batch, scale=DIFF_HEAD ** -0.5, diff_lam=lam,
                       diff_gain=diff_subln[l], out_scale=1.0 - lam_init)
        yd = attention(q_d, k_d, v_d, q_rows=n_lat, q_first=0, segments=segs_lat, tq=1024, row_parts=2, **diff_kw)
        if ctx_out:
            yd_c = attention(q_d, k_d, v_d, q_rows=n_ctx, q_first=ctx_blk, segments=[(n_ctx, ctx_blk)], **diff_kw)
            yd = jnp.concatenate([yd, yd_c], axis=0)

        gates = matmul(h, w_gate, layer=l, rows=m_out, act="sigmoid", out_dtype=BF16)
        merged = merge_branches(gates, [ya, yb, yc, yd], w_branch, l, rows=m_out)
        mix = matmul(merged, w_out, layer=l)
        xs, hf = norm_residual(xs, mix, norm_mix_post[l], mod, 2, rows=m_out, rows_per_group=n_lat,
                               next_norm=(norm_ffn_pre[l], mod, 3, 4))
        f1 = matmul(hf, w_ff1, layer=l, act="relu2", out_dtype=BF16)
        f2 = matmul(f1, w_ff2, layer=l, tn=512)
        if ctx_out:
            xs, h = norm_residual(xs, f2, norm_ffn_post[l], mod, 5, rows=m_out, rows_per_group=n_lat,
                                  next_norm=(norm_mix_pre[l + 1], mods[l + 1], 0, 1))
        else:
            xs = norm_residual(xs, f2, norm_ffn_post[l], mod, 5, rows=m_out, rows_per_group=n_lat)

    return xs[:m_lat].reshape(batch, n_lat, d)
```

```python
import functools
import math

import jax
import jax.numpy as jnp
from jax import lax
from jax.experimental import pallas as pl
from jax.experimental.pallas import tpu as pltpu

F32 = jnp.float32
BF16 = jnp.bfloat16
HIGHEST = lax.Precision.HIGHEST

NORM_EPS = 1e-6
ROPE_BASE = 10000.0
GRID_W = 64
N_BRANCH = 4
LANES = 128
SUBLANES = 8

RWKV_HEAD = 64
RWKV_DECAY_RANK = 64
RWKV_AAA_RANK = 64
RWKV_GATE_RANK = 128
RWKV_LN_EPS = 64e-5
RWKV_CHUNK = 64

MLA_HEADS = 8
MLA_NOPE = 128
MLA_ROPE = 64
MLA_Q_RANK = 768
MLA_KV_RANK = 256

SSD_HEAD = 64
SSD_GROUPS = 4
SSD_STATE = 128
SSD_CHUNK = 128

DIFF_HEADS = 8
DIFF_HEAD = 64

PREP_ROWS = 256

VMEM_LIMIT_BYTES = 56 * 1024 * 1024


def _params(*sem):
    return pltpu.CompilerParams(dimension_semantics=sem, vmem_limit_bytes=VMEM_LIMIT_BYTES)


def _dot(a, b):
    return jnp.dot(a, b, preferred_element_type=F32)


def _dot_nt(a, b):
    return lax.dot_general(a, b, (((1,), (1,)), ((), ())), preferred_element_type=F32)


def _dot_tn(a, b):
    return lax.dot_general(a, b, (((0,), (0,)), ((), ())), preferred_element_type=F32)


def _dot_hi(a, b):
    return jnp.dot(a, b, preferred_element_type=F32, precision=HIGHEST)


def _split_bf16(x, terms):
    parts = []
    for _ in range(terms):
        p = x.astype(BF16)
        parts.append(p)
        x = x - p.astype(F32)
    return parts


def _dot_split_lhs(x, e, terms=2):
    return sum(_dot(p, e) for p in _split_bf16(x, terms))


def _dot_split_rhs(e, x, terms=2):
    return sum(_dot(e, p) for p in _split_bf16(x, terms))


def _dot3(a, b):
    a_hi, a_lo = _split_bf16(a, 2)
    b_hi, b_lo = _split_bf16(b, 2)
    return _dot(a_hi, b_hi) + (_dot(a_lo, b_hi) + _dot(a_hi, b_lo))


def _full(shape):
    nd = len(shape)
    return pl.BlockSpec(shape, lambda *_: (0,) * nd)


def _apply_act(y, act):
    if act is None:
        return y
    if act == "sigmoid":
        return 0.5 * jnp.tanh((0.5 * y).astype(BF16)) + 0.5
    if act == "relu2":
        return jnp.square(jnp.maximum(y, 0.0))
    raise ValueError(act)


def _mm_kernel(x_ref, w_ref, o_ref, wbf_ref, *acc, nk, act):
    k = pl.program_id(1)
    i = pl.program_id(2)

    @pl.when(i == 0)
    def _():
        wbf_ref[...] = w_ref[...].astype(BF16)

    y = _dot(x_ref[...].astype(BF16), wbf_ref[...])
    if nk == 1:
        o_ref[...] = _apply_act(y, act).astype(o_ref.dtype)
        return
    (acc_ref,) = acc
    tm = x_ref.shape[0]
    rows = pl.ds(pl.multiple_of(i * tm, tm), tm)

    @pl.when(k == 0)
    def _():
        acc_ref[rows, :] = y

    @pl.when(jnp.logical_and(k > 0, k < nk - 1))
    def _():
        acc_ref[rows, :] += y

    @pl.when(k == nk - 1)
    def _():
        o_ref[...] = _apply_act(acc_ref[rows, :] + y, act).astype(o_ref.dtype)


def matmul(x, w, *, layer=None, rows=None, tm=512, tn=1024, tk=4096, act=None, out_dtype=F32, out_slab=None):
    x_slabs = x.ndim == 3
    m = x.shape[-2] if rows is None else rows
    kdim = x.shape[0] * x.shape[2] if x_slabs else x.shape[1]
    n = w.shape[-1]
    assert w.shape[-2] == kdim
    tm = math.gcd(tm, m)
    tk = x.shape[2] if x_slabs else min(tk, kdim)
    tn = min(tn, n)
    assert m % tm == 0 and kdim % tk == 0
    ni, nk, nj = m // tm, kdim // tk, pl.cdiv(n, tn)
    if w.ndim == 3:
        w_spec = pl.BlockSpec((None, tk, tn), lambda j, k, i: (layer, k, j))
    else:
        w_spec = pl.BlockSpec((tk, tn), lambda j, k, i: (k, j))
    if nk == 1:
        o_map = lambda j, k, i: (i, j)
    else:
        o_map = lambda j, k, i: (jnp.where(k == nk - 1, i, 0), j)
    scratch = [pltpu.VMEM((tk, tn), BF16)]
    if nk > 1:
        scratch.append(pltpu.VMEM((m, tn), F32))
    if x_slabs:
        x_spec = pl.BlockSpec((None, tm, tk), lambda j, k, i: (k, i, 0))
    else:
        x_spec = pl.BlockSpec((tm, tk), lambda j, k, i: (i, k))
    out_spec = pl.BlockSpec((tm, tn), o_map)
    out_shape = jax.ShapeDtypeStruct((m, n), out_dtype)
    if out_slab is not None:
        assert nk == 1 and out_slab % tn == 0 and n % out_slab == 0
        per = out_slab // tn
        out_spec = pl.BlockSpec((None, tm, tn), lambda j, k, i: (j // per, i, j % per))
        out_shape = jax.ShapeDtypeStruct((n // out_slab, m, out_slab), out_dtype)
    return pl.pallas_call(
        functools.partial(_mm_kernel, nk=nk, act=act),
        grid=(nj, nk, ni),
        in_specs=[x_spec, w_spec],
        out_specs=out_spec,
        out_shape=out_shape,
        scratch_shapes=scratch,
        compiler_params=_params("arbitrary", "arbitrary", "arbitrary"),
        name="matmul_ws",
    )(x, w)


def _merge_kernel(g0, g1, g2, g3, y0, y1, y2, y3, w_ref, o_ref, wbf_ref):
    @pl.when(pl.program_id(1) == 0)
    def _():
        wbf_ref[...] = w_ref[...].astype(BF16)

    acc = None
    for b, (g, y) in enumerate(((g0, y0), (g1, y1), (g2, y2), (g3, y3))):
        t = g[...].astype(F32) * _dot(y[...].astype(BF16), wbf_ref[b])
        acc = t if acc is None else acc + t
    o_ref[...] = acc.astype(o_ref.dtype)


def merge_branches(gates, ys, w_branch, layer, *, rows, tm=1024):
    d = w_branch.shape[-1]
    bw = w_branch.shape[-2]
    tn = gates.shape[2]
    tm = math.gcd(tm, rows)
    assert rows % tm == 0 and d % tn == 0 and gates.shape[0] * tn == N_BRANCH * d
    nj = d // tn
    g_specs = [pl.BlockSpec((None, tm, tn), functools.partial(lambda j, i, b: (b * nj + j, i, 0), b=b))
               for b in range(N_BRANCH)]
    y_specs = [pl.BlockSpec((tm, bw), lambda j, i: (i, 0)) for _ in range(N_BRANCH)]
    return pl.pallas_call(
        _merge_kernel,
        grid=(nj, rows // tm),
        in_specs=g_specs + y_specs + [pl.BlockSpec((None, N_BRANCH, bw, tn), lambda j, i: (layer, 0, 0, j))],
        out_specs=pl.BlockSpec((tm, tn), lambda j, i: (i, j)),
        out_shape=jax.ShapeDtypeStruct((rows, d), BF16),
        scratch_shapes=[pltpu.VMEM((N_BRANCH, bw, tn), BF16)],
        compiler_params=_params("arbitrary", "arbitrary"),
        name="merge_branches",
    )(gates, gates, gates, gates, *ys, w_branch)


def _rms(x, eps=NORM_EPS):
    return x * lax.rsqrt(jnp.mean(x * x, axis=-1, keepdims=True) + eps)


def _norm_mod_kernel(x_ref, g_ref, shift_ref, scale_ref, o_ref):
    x = x_ref[...].astype(F32)
    y = _rms(x) * g_ref[...]
    o_ref[...] = (y * (1.0 + scale_ref[...]) + shift_ref[...]).astype(o_ref.dtype)


def _group_of_rows(tm, rows_per_group, n_groups):
    return lambda i: jnp.minimum((i * tm) // rows_per_group, n_groups - 1)


def norm_modulate(x, g, mod, shift_idx, scale_idx, *, rows, rows_per_group, tm=PREP_ROWS):
    d = x.shape[1]
    tm = min(tm, rows_per_group)
    assert rows % tm == 0 and rows_per_group % tm == 0
    grp = _group_of_rows(tm, rows_per_group, mod.shape[0])
    return pl.pallas_call(
        _norm_mod_kernel,
        grid=(rows // tm,),
        in_specs=[pl.BlockSpec((tm, d), lambda i: (i, 0)),
                  pl.BlockSpec((1, d), lambda i: (0, 0)),
                  pl.BlockSpec((None, 1, d), lambda i: (grp(i), 0, shift_idx)),
                  pl.BlockSpec((None, 1, d), lambda i: (grp(i), 0, scale_idx))],
        out_specs=pl.BlockSpec((tm, d), lambda i: (i, 0)),
        out_shape=jax.ShapeDtypeStruct((rows, d), BF16),
        compiler_params=_params("arbitrary"),
        name="norm_modulate",
    )(x, g.reshape(1, d), mod, mod)


def _norm_res_kernel(x_ref, y_ref, g_ref, m_ref, *rest):
    y = _rms(y_ref[...].astype(F32)) * g_ref[...]
    x = x_ref[...] + m_ref[...] * y
    if len(rest) == 1:
        rest[0][...] = x
        return
    g2_ref, shift_ref, scale_ref, o_ref, h_ref = rest
    o_ref[...] = x
    h_ref[...] = (_rms(x) * g2_ref[...] * (1.0 + scale_ref[...]) + shift_ref[...]).astype(h_ref.dtype)


def norm_residual(x, y, g, mod, gate_idx, *, rows, rows_per_group, tm=PREP_ROWS, next_norm=None):
    d = x.shape[1]
    tm = min(tm, rows_per_group)
    assert rows % tm == 0 and rows_per_group % tm == 0
    grp = _group_of_rows(tm, rows_per_group, mod.shape[0])
    tile = pl.BlockSpec((tm, d), lambda i: (i, 0))
    vec = pl.BlockSpec((1, d), lambda i: (0, 0))
    mod_spec = lambda idx: pl.BlockSpec((None, 1, d), lambda i: (grp(i), 0, idx))
    in_specs = [tile, tile, vec, mod_spec(gate_idx)]
    args = [x, y, g.reshape(1, d), mod]
    out_specs, out_shape = tile, jax.ShapeDtypeStruct((rows, d), F32)
    if next_norm is not None:
        g2, mod2, shift_idx, scale_idx = next_norm
        in_specs += [vec, mod_spec(shift_idx), mod_spec(scale_idx)]
        args += [g2.reshape(1, d), mod2, mod2]
        out_specs, out_shape = [tile, tile], [out_shape, jax.ShapeDtypeStruct((rows, d), BF16)]
    return pl.pallas_call(
        _norm_res_kernel,
        grid=(rows // tm,),
        in_specs=in_specs,
        out_specs=out_specs,
        out_shape=out_shape,
        compiler_params=_params("arbitrary"),
        name="norm_residual",
    )(*args)


class _Rows:
    def __init__(self, batch, n_lat, n_ctx):
        self.batch, self.n_lat, self.n_ctx = batch, n_lat, n_ctx
        self.m_lat = batch * n_lat
        self.m = batch * (n_lat + n_ctx)

    def split(self, t):
        c = t.shape[-1]
        return (t[:self.m_lat].reshape(self.batch, self.n_lat, c),
                t[self.m_lat:].reshape(self.batch, self.n_ctx, c))

    def join(self, lat, ctx):
        c = lat.shape[-1]
        return jnp.concatenate([lat.reshape(self.m_lat, c), ctx.reshape(self.batch * self.n_ctx, c)], axis=0)

    def seq_edges(self, i, tm):
        r0 = i * tm
        in_lat = r0 < self.m_lat
        pos = jnp.where(in_lat, r0 % self.n_lat, (r0 - self.m_lat) % self.n_ctx)
        length = jnp.where(in_lat, self.n_lat, self.n_ctx)
        return pos == 0, pos + tm == length

    def chunk_block(self, chunk):
        ncc, nlc = self.n_ctx // chunk, self.n_lat // chunk
        ctx0 = self.m_lat // chunk

        def block(d, b, c):
            in_ctx = c < ncc
            cc = jnp.where(in_ctx, c, c - ncc)
            n_seg = jnp.where(in_ctx, ncc, nlc)
            cc = jnp.where(d == 1, n_seg - 1 - cc, cc)
            return jnp.where(in_ctx, ctx0 + b * ncc, b * nlc) + cc
        return block, ncc + nlc


def _conv3_tile(x, prev_row, next_row, w_ref, first, last):
    tm = x.shape[0]
    rid = lax.broadcasted_iota(jnp.int32, x.shape, 0)
    prev_row = jnp.where(first, 0.0, prev_row)
    next_row = jnp.where(last, 0.0, next_row)
    prev = jnp.where(rid == 0, prev_row, pltpu.roll(x, 1, 0))
    nxt = jnp.where(rid == tm - 1, next_row, pltpu.roll(x, tm - 1, 0))
    return prev * w_ref[0:1, :] + x * w_ref[1:2, :] + nxt * w_ref[2:3, :]


def _halo_specs(tm, width, col_block, n_rows):
    per = tm // SUBLANES
    last = n_rows // SUBLANES - 1
    return [pl.BlockSpec((SUBLANES, width), lambda i: (jnp.maximum(i * per - 1, 0), col_block)),
            pl.BlockSpec((SUBLANES, width), lambda i: (jnp.minimum((i + 1) * per, last), col_block))]


def _chunk_cumsum_matrix(tm, chunk, reverse):
    r = lax.broadcasted_iota(jnp.int32, (tm, tm), 0)
    c = lax.broadcasted_iota(jnp.int32, (tm, tm), 1)
    same = (r // chunk) == (c // chunk)
    tri = (r <= c) if reverse else (r >= c)
    return jnp.logical_and(same, tri).astype(BF16)


assert RWKV_CHUNK * math.exp(-0.5) < 80.0

def _rwkv_prep_kernel(rkv_ref, hp_ref, hn_ref, low_ref, shift_ref, w0_ref, a0_ref, wup_ref, aup_ref, gup_ref,
                      kk_ref, ka_ref, rk_ref, e_ref,
                      at_ref, rt_ref, bt_ref, kt_ref, bc_ref, kc_ref, gend_ref, v_ref, gate_ref, bonus_ref,
                      *, rows, chunk):
    tm = rkv_ref.shape[0]
    bw = v_ref.shape[1]
    first, last = rows.seq_edges(pl.program_id(0), tm)
    rkv = _conv3_tile(rkv_ref[...], hp_ref[SUBLANES - 1:SUBLANES, :], hn_ref[0:1, :], shift_ref, first, last)
    r, k, v = rkv[:, :bw], rkv[:, bw:2 * bw], rkv[:, 2 * bw:]
    e = e_ref[...]
    kk = k * kk_ref[...]
    kk = kk * lax.rsqrt(_dot_split_lhs(kk * kk, e) + 1e-12)
    low = low_ref[...]
    wd = low[:, :RWKV_DECAY_RANK]
    ad = low[:, RWKV_DECAY_RANK:RWKV_DECAY_RANK + RWKV_AAA_RANK]
    gd = low[:, RWKV_DECAY_RANK + RWKV_AAA_RANK:]
    w_pre = _dot3(jnp.tanh(wd), wup_ref[...])
    a_pre = _dot3(ad, aup_ref[...])
    gate_ref[...] = _dot3(jax.nn.sigmoid(gd), gup_ref[...])
    v_ref[...] = v.astype(BF16)
    kt_sum = None
    for d in range(2):
        w_log = -jax.nn.softplus(-(w0_ref[d:d + 1, :] + w_pre[:, d * bw:(d + 1) * bw])) - 0.5
        lw = -jnp.exp(w_log)
        a = jax.nn.sigmoid(a0_ref[d:d + 1, :] + a_pre[:, d * bw:(d + 1) * bw])
        kt = k * (1.0 + (a - 1.0) * ka_ref[...])
        kt_sum = kt if kt_sum is None else kt_sum + kt
        b = kk * a
        lam = _dot_split_rhs(_chunk_cumsum_matrix(tm, chunk, d == 1), lw)
        g_inv = jnp.exp(-lam)
        at_ref[d] = (-kk * jnp.exp(lam - lw)).astype(BF16)
        rt_ref[d] = (r * jnp.exp(lam)).astype(BF16)
        bt_ref[d] = (b * g_inv).astype(BF16)
        kt_ref[d] = (kt * g_inv).astype(BF16)
        for q in range(tm // chunk):
            end = q * chunk if d == 1 else (q + 1) * chunk - 1
            lam_end = lam[end:end + 1, :]
            sl = slice(q * chunk, (q + 1) * chunk)
            to_end = jnp.exp(lam_end - lam[sl])
            bc_ref[d, sl, :] = (b[sl] * to_end).astype(BF16)
            kc_ref[d, sl, :] = (kt[sl] * to_end).astype(BF16)
            gend_ref[d, q] = jnp.exp(lam_end)
    bonus_ref[...] = _dot_split_lhs(r * kt_sum * rk_ref[...], e, terms=1) * v


def rwkv_prep(pa, rows, layer, shift, w0, w_up, a0, a_up, g_up, k_k, k_a, r_k, seg_ones, *, tm=PREP_ROWS,
              chunk=RWKV_CHUNK):
    m = pa.shape[0]
    bw = k_k.shape[-1]
    low_w = RWKV_DECAY_RANK + RWKV_AAA_RANK + RWKV_GATE_RANK
    tm = min(tm, rows.n_ctx)
    assert m % tm == 0 and rows.n_ctx % tm == 0 and rows.n_lat % tm == 0 and tm % chunk == 0
    assert (3 * bw) % low_w == 0
    wup = jnp.concatenate([w_up[layer, 0], w_up[layer, 1]], axis=1)
    aup = jnp.concatenate([a_up[layer, 0], a_up[layer, 1]], axis=1)
    row = lambda t: t.reshape(1, bw)
    big = jax.ShapeDtypeStruct((2, m, bw), BF16)
    big_spec = pl.BlockSpec((2, tm, bw), lambda i: (0, i, 0))
    one_spec = pl.BlockSpec((tm, bw), lambda i: (i, 0))
    cpt = tm // chunk
    return pl.pallas_call(
        functools.partial(_rwkv_prep_kernel, rows=rows, chunk=chunk),
        grid=(m // tm,),
        in_specs=[pl.BlockSpec((tm, 3 * bw), lambda i: (i, 0))] + _halo_specs(tm, 3 * bw, 0, m)
        + [pl.BlockSpec((tm, low_w), lambda i: (i, 3 * bw // low_w)),
           _full((3, 3 * bw)), _full((2, bw)), _full((2, bw)), _full(wup.shape), _full(aup.shape),
           pl.BlockSpec((None,) + g_up.shape[1:], lambda i: (layer, 0, 0)),
           _full((1, bw)), _full((1, bw)), _full((1, bw)), _full(seg_ones.shape)],
        out_specs=[big_spec] * 6 + [pl.BlockSpec((2, cpt, 1, bw), lambda i: (0, i, 0, 0)),
                                    one_spec, one_spec, one_spec],
        out_shape=[big] * 6 + [jax.ShapeDtypeStruct((2, m // chunk, 1, bw), F32),
                               jax.ShapeDtypeStruct((m, bw), BF16), jax.ShapeDtypeStruct((m, bw), F32),
                               jax.ShapeDtypeStruct((m, bw), F32)],
        compiler_params=_params("arbitrary"),
        name="rwkv_prep",
    )(pa, pa, pa, pa, shift[layer], w0[layer], a0[layer], wup, aup, g_up, row(k_k[layer]), row(k_a[layer]),
      row(r_k[layer]), seg_ones)


def _pair_select(lo, z):
    half = z.shape[0] // 2
    return jnp.where(lo, z[:half], z[half:])


def _rwkv_scan_kernel(at_ref, rt_ref, bt_ref, kt_ref, bc_ref, kc_ref, gend_ref, v_ref, y_ref, s_ref, *, chunk):
    @pl.when(pl.program_id(2) == 0)
    def _():
        s_ref[...] = jnp.zeros_like(s_ref)

    c2 = 2 * chunk
    n_pairs = s_ref.shape[0]
    pw = 2 * RWKV_HEAD
    n_levels = chunk.bit_length() - 1
    order = lax.broadcasted_iota(jnp.int32, (chunk, chunk), 0) - lax.broadcasted_iota(jnp.int32, (chunk, chunk), 1)
    order = jnp.where(pl.program_id(0) == 1, -order, order)
    strict = order > 0
    incl = order >= 0
    lo = lax.broadcasted_iota(jnp.int32, (chunk, pw), 1) < RWKV_HEAD
    lo2 = lax.broadcasted_iota(jnp.int32, (c2, pw), 1) < RWKV_HEAD
    own = ((lax.broadcasted_iota(jnp.int32, (pw, pw), 0) < RWKV_HEAD)
           == (lax.broadcasted_iota(jnp.int32, (pw, pw), 1) < RWKV_HEAD))
    pairs = range(n_pairs)
    sl = [slice(j * pw, (j + 1) * pw) for j in pairs]

    def tri(mask, z):
        return jnp.where(mask, z, 0.0)

    a = [at_ref[:, s] for s in sl]
    r = [rt_ref[:, s] for s in sl]
    v = [v_ref[:, s] for s in sl]
    s0 = [s_ref[j] for j in pairs]
    ar = [jnp.concatenate([a[j], r[j]], axis=0) for j in pairs]
    ar_st = [jnp.concatenate([jnp.where(lo2, ar[j], 0), jnp.where(lo2, 0, ar[j])], axis=0) for j in pairs]
    p_b = [_dot_nt(ar_st[j], bt_ref[:, sl[j]]) for j in pairs]
    p_k = [_dot_nt(ar_st[j], kt_ref[:, sl[j]]) for j in pairs]
    lp = [[tri(strict, p_b[j][e * c2:e * c2 + chunk]) for e in range(2)] for j in pairs]
    l_ak = [jnp.concatenate([tri(strict, p_k[j][e * c2:e * c2 + chunk]) for e in range(2)], axis=0).astype(BF16)
            for j in pairs]
    m_rb = [jnp.concatenate([tri(incl, p_b[j][e * c2 + chunk:(e + 1) * c2]) for e in range(2)], axis=0).astype(BF16)
            for j in pairs]
    m_rk = [jnp.concatenate([tri(incl, p_k[j][e * c2 + chunk:(e + 1) * c2]) for e in range(2)], axis=0).astype(BF16)
            for j in pairs]
    x_w = [a[j].astype(F32) for j in pairs]
    x_u = [_pair_select(lo, _dot(l_ak[j], v[j])) for j in pairs]
    for lvl in range(n_levels):
        lpb = [[lp[j][e].astype(BF16) for e in range(2)] for j in pairs]
        lst = [jnp.concatenate(lpb[j], axis=0) for j in pairs]
        z = [_dot(lst[j], jnp.concatenate([x_w[j], x_u[j]], axis=1).astype(BF16)) for j in pairs]
        x_w = [x_w[j] + _pair_select(lo, z[j][:, :pw]) for j in pairs]
        x_u = [x_u[j] + _pair_select(lo, z[j][:, pw:]) for j in pairs]
        if lvl + 1 < n_levels:
            lp = [[_dot(lpb[j][e], lpb[j][e]) for e in range(2)] for j in pairs]
    s0b = [s0[j].astype(BF16) for j in pairs]
    u = [_dot_nt(x_w[j].astype(BF16), s0b[j]) + x_u[j] for j in pairs]
    ub = [u[j].astype(BF16) for j in pairs]
    y = [_dot_nt(r[j], s0b[j]) + _pair_select(lo, _dot(m_rb[j], ub[j])) + _pair_select(lo, _dot(m_rk[j], v[j]))
         for j in pairs]
    s_new = [s0[j] * gend_ref[:, sl[j]]
             + jnp.where(own, _dot_tn(ub[j], bc_ref[:, sl[j]]) + _dot_tn(v[j], kc_ref[:, sl[j]]), 0.0)
             for j in pairs]
    for j in pairs:
        y_ref[:, sl[j]] = y[j]
    for j in pairs:
        s_ref[j] = s_new[j]


def rwkv_scan(ops, v, rows, *, chunk=RWKV_CHUNK):
    at, rt, bt, kt, bc, kc, gend = ops
    _, m, bw = at.shape
    block, nc = rows.chunk_block(chunk)
    big = pl.BlockSpec((None, chunk, bw), lambda d, b, c: (d, block(d, b, c), 0))
    return pl.pallas_call(
        functools.partial(_rwkv_scan_kernel, chunk=chunk),
        grid=(2, rows.batch, nc),
        in_specs=[big] * 6 + [pl.BlockSpec((None, None, 1, bw), lambda d, b, c: (d, block(d, b, c), 0, 0)),
                              pl.BlockSpec((chunk, bw), lambda d, b, c: (block(d, b, c), 0))],
        out_specs=big,
        out_shape=jax.ShapeDtypeStruct((2, m, bw), F32),
        scratch_shapes=[pltpu.VMEM((bw // (2 * RWKV_HEAD), 2 * RWKV_HEAD, 2 * RWKV_HEAD), F32)],
        compiler_params=_params("arbitrary", "arbitrary", "arbitrary"),
        name="rwkv_scan",
    )(at, rt, bt, kt, bc, kc, gend, v)


def _rwkv_readout_kernel(y_ref, bonus_ref, gate_ref, e_ref, g_ref, b_ref, o_ref):
    y = y_ref[0] + y_ref[1]
    e = e_ref[...]
    mu = _dot_split_lhs(y, e) * (1.0 / RWKV_HEAD)
    yc = y - mu
    var = _dot_split_lhs(yc * yc, e) * (1.0 / RWKV_HEAD)
    yn = yc * lax.rsqrt(var + RWKV_LN_EPS) * g_ref[...] + b_ref[...]
    o_ref[...] = ((yn + bonus_ref[...]) * gate_ref[...]).astype(o_ref.dtype)


def rwkv_readout(y, bonus, gate, seg_ones, ln_g, ln_b, *, rows, tm=PREP_ROWS):
    bw = y.shape[-1]
    tm = math.gcd(tm, rows)
    one = pl.BlockSpec((tm, bw), lambda i: (i, 0))
    return pl.pallas_call(
        _rwkv_readout_kernel,
        grid=(rows // tm,),
        in_specs=[pl.BlockSpec((2, tm, bw), lambda i: (0, i, 0)), one, one, _full(seg_ones.shape),
                  _full((1, bw)), _full((1, bw))],
        out_specs=one,
        out_shape=jax.ShapeDtypeStruct((rows, bw), BF16),
        compiler_params=_params("arbitrary"),
        name="rwkv_readout",
    )(y, bonus, gate, seg_ones, ln_g.reshape(1, bw), ln_b.reshape(1, bw))


def _ssd_prep_kernel(x_ref, hp_ref, hn_ref, w_ref, b_ref, o_ref, *, rows):
    first, last = rows.seq_edges(pl.program_id(0), x_ref.shape[0])
    y = _conv3_tile(x_ref[...], hp_ref[SUBLANES - 1:SUBLANES, :], hn_ref[0:1, :], w_ref, first, last) + b_ref[...]
    o_ref[...] = y * jax.nn.sigmoid(y)


def ssd_prep(pc, rows, conv_w, conv_b, *, width, first_block, n_blocks, tm=PREP_ROWS):
    m = pc.shape[0]
    tm = min(tm, rows.n_ctx)
    assert m % tm == 0 and rows.n_ctx % tm == 0 and rows.n_lat % tm == 0
    per = tm // SUBLANES
    last = m // SUBLANES - 1
    return pl.pallas_call(
        functools.partial(_ssd_prep_kernel, rows=rows),
        grid=(m // tm, n_blocks),
        in_specs=[pl.BlockSpec((tm, width), lambda i, j: (i, first_block + j)),
                  pl.BlockSpec((SUBLANES, width), lambda i, j: (jnp.maximum(i * per - 1, 0), first_block + j)),
                  pl.BlockSpec((SUBLANES, width), lambda i, j: (jnp.minimum((i + 1) * per, last), first_block + j)),
                  pl.BlockSpec((3, width), lambda i, j: (0, j)),
                  pl.BlockSpec((1, width), lambda i, j: (0, j))],
        out_specs=pl.BlockSpec((tm, width), lambda i, j: (i, j)),
        out_shape=jax.ShapeDtypeStruct((m, n_blocks * width), F32),
        compiler_params=_params("arbitrary", "arbitrary"),
        name="ssd_prep",
    )(pc, pc, pc, conv_w, conv_b.reshape(1, n_blocks * width))


def _ssd_scan_kernel(x_ref, b_ref, c_ref, la_ref, lat_ref, dt_ref, y_ref, s_ref, *, chunk, heads, groups):
    @pl.when(pl.program_id(2) == 0)
    def _():
        s_ref[...] = jnp.zeros_like(s_ref)

    rev = pl.program_id(0) == 1
    pw = 2 * SSD_HEAD
    order = lax.broadcasted_iota(jnp.int32, (chunk, chunk), 0) - lax.broadcasted_iota(jnp.int32, (chunk, chunk), 1)
    order = jnp.where(rev, -order, order)
    upto = order >= 0
    lo = lax.broadcasted_iota(jnp.int32, (chunk, pw), 1) < SSD_HEAD
    acs_c_all = _dot_hi(upto.astype(F32), la_ref[...])
    acs_r_all = _dot_hi(lat_ref[...], (order <= 0).astype(F32))
    tot_c_all = jnp.where(rev, acs_c_all[0:1, :], acs_c_all[chunk - 1:chunk, :])
    dt_all = dt_ref[...]
    hpg = heads // groups
    for gi in range(groups):
        cg = c_ref[:, gi * SSD_STATE:(gi + 1) * SSD_STATE].astype(BF16)
        bg = b_ref[:, gi * SSD_STATE:(gi + 1) * SSD_STATE].astype(BF16)
        cb = _dot_nt(cg, bg)
        for qi in range(hpg // 2):
            h0 = gi * hpg + 2 * qi
            j = h0 // 2
            cols = slice(j * pw, (j + 1) * pw)
            w_st, acs_c, tot = [], [], []
            for e in range(2):
                h = h0 + e
                acs_c.append(acs_c_all[:, h:h + 1])
                tot.append(tot_c_all[:, h:h + 1])
                decay = jnp.exp(jnp.where(upto, acs_c[e] - acs_r_all[h:h + 1, :], -1e30))
                w_st.append((cb * decay).astype(BF16))
            xdt = x_ref[:, cols] * jnp.where(lo, dt_all[:, h0:h0 + 1], dt_all[:, h0 + 1:h0 + 2])
            st = s_ref[j]
            y = _pair_select(lo, _dot(jnp.concatenate(w_st, axis=0), xdt.astype(BF16)))
            y = y + _dot(cg, st.astype(BF16)) * jnp.where(lo, jnp.exp(acs_c[0]), jnp.exp(acs_c[1]))
            y_ref[:, cols] = y
            to_end = jnp.where(lo, jnp.exp(tot[0] - acs_c[0]), jnp.exp(tot[1] - acs_c[1]))
            s_ref[j] = (st * jnp.where(lo[0:1, :], jnp.exp(tot[0]), jnp.exp(tot[1]))
                        + _dot_tn(bg, (xdt * to_end).astype(BF16)))


def ssd_scan(xbc, la, dt, rows, *, heads, chunk=SSD_CHUNK):
    m = xbc.shape[0]
    bw = heads * SSD_HEAD
    gn = SSD_GROUPS * SSD_STATE
    assert bw % gn == 0
    block, nc = rows.chunk_block(chunk)
    lat = jnp.swapaxes(la, 1, 2)
    thin = pl.BlockSpec((None, chunk, heads), lambda d, b, c: (d, block(d, b, c), 0))
    return pl.pallas_call(
        functools.partial(_ssd_scan_kernel, chunk=chunk, heads=heads, groups=SSD_GROUPS),
        grid=(2, rows.batch, nc),
        in_specs=[pl.BlockSpec((chunk, bw), lambda d, b, c: (block(d, b, c), 0)),
                  pl.BlockSpec((chunk, gn), lambda d, b, c: (block(d, b, c), bw // gn)),
                  pl.BlockSpec((chunk, gn), lambda d, b, c: (block(d, b, c), bw // gn + 1)),
                  thin,
                  pl.BlockSpec((None, heads, chunk), lambda d, b, c: (d, 0, block(d, b, c))),
                  thin],
        out_specs=pl.BlockSpec((None, chunk, bw), lambda d, b, c: (d, block(d, b, c), 0)),
        out_shape=jax.ShapeDtypeStruct((2, m, bw), F32),
        scratch_shapes=[pltpu.VMEM((heads // 2, SSD_STATE, 2 * SSD_HEAD), F32)],
        compiler_params=_params("arbitrary", "arbitrary", "arbitrary"),
        name="ssd_scan",
    )(xbc, xbc, xbc, la, lat, dt)


def _ssd_readout_kernel(y_ref, x_ref, z_ref, d_ref, g_ref, o_ref):
    y = y_ref[0] + y_ref[1] + d_ref[...] * x_ref[...]
    z = z_ref[...]
    o_ref[...] = (_rms(y * (z * jax.nn.sigmoid(z))) * g_ref[...]).astype(o_ref.dtype)


def ssd_readout(y, xbc, pc, d_skip, norm_g, *, rows, tm=PREP_ROWS):
    bw = y.shape[-1]
    tm = math.gcd(tm, rows)
    one = pl.BlockSpec((tm, bw), lambda i: (i, 0))
    return pl.pallas_call(
        _ssd_readout_kernel,
        grid=(rows // tm,),
        in_specs=[pl.BlockSpec((2, tm, bw), lambda i: (0, i, 0)), one, one, _full((1, bw)), _full((1, bw))],
        out_specs=one,
        out_shape=jax.ShapeDtypeStruct((rows, bw), BF16),
        compiler_params=_params("arbitrary"),
        name="ssd_readout",
    )(y, xbc, pc, jnp.repeat(d_skip, SSD_HEAD).reshape(1, bw), norm_g.reshape(1, bw))


def _softmax_pv(scores, values):
    m = None
    for s in scores:
        ms = jnp.max(s, axis=-1, keepdims=True)
        m = ms if m is None else jnp.maximum(m, ms)
    l, o = None, None
    for s, v in zip(scores, values):
        p = jnp.exp(s - m)
        ls = jnp.sum(p, axis=-1, keepdims=True)
        os_ = _dot(p.astype(BF16), v)
        l = ls if l is None else l + ls
        o = os_ if o is None else o + os_
    return o / l


def _attn_kernel(lam_ref, q_ref, g_ref, *refs, n_seg, scale, diff, out_scale, row_parts):
    seg_refs = refs[:2 * n_seg]
    o_ref = refs[2 * n_seg]
    values = [seg_refs[2 * si + 1][...] for si in range(n_seg)]
    rows_per = q_ref.shape[0] // row_parts
    for r in range(row_parts):
        rs = slice(r * rows_per, (r + 1) * rows_per)
        q = q_ref[rs, :].astype(F32) * scale
        if not diff:
            qb = q.astype(BF16)
            scores = [_dot_nt(qb, seg_refs[2 * si][...]) for si in range(n_seg)]
            o_ref[rs, :] = _softmax_pv(scores, values).astype(o_ref.dtype)
            continue
        first = lax.broadcasted_iota(jnp.int32, q.shape, 1) < DIFF_HEAD
        q1 = jnp.where(first, q, 0.0).astype(BF16)
        q2 = jnp.where(first, 0.0, q).astype(BF16)
        o1 = _softmax_pv([_dot_nt(q1, seg_refs[2 * si][...]) for si in range(n_seg)], values)
        o2 = _softmax_pv([_dot_nt(q2, seg_refs[2 * si][...]) for si in range(n_seg)], values)
        o = o1 - lam_ref[0] * o2
        o_ref[rs, :] = (_rms(o) * g_ref[...] * out_scale).astype(o_ref.dtype)


def attention(q, k, v, *, heads, batch, q_rows, q_first, segments, scale, tq=256, row_parts=1,
              diff_lam=None, diff_gain=None, out_scale=1.0):
    dqk = q.shape[1] // heads
    dv = v.shape[1] // heads
    tq = min(tq, q_rows)
    nq = q_rows // tq
    if tq % (row_parts * 2 * SUBLANES):
        row_parts = 1
    diff = diff_lam is not None
    lam = (diff_lam if diff else jnp.zeros((), F32)).reshape(1).astype(F32)
    gain = (diff_gain if diff else jnp.ones((dv,), F32)).reshape(1, dv).astype(F32)
    seg_specs = []
    for seg_rows, first in segments:
        for w in (dqk, dv):
            seg_specs.append(pl.BlockSpec((seg_rows, w), functools.partial(lambda b, h, i, f: (f + b, h), f=first)))
    return pl.pallas_call(
        functools.partial(_attn_kernel, n_seg=len(segments), scale=scale, diff=diff, out_scale=out_scale,
                          row_parts=row_parts),
        grid=(batch, heads, nq),
        in_specs=[pl.BlockSpec(memory_space=pltpu.SMEM),
                  pl.BlockSpec((tq, dqk), lambda b, h, i: ((q_first + b) * nq + i, h)),
                  pl.BlockSpec((1, dv), lambda b, h, i: (0, 0))] + seg_specs,
        out_specs=pl.BlockSpec((tq, dv), lambda b, h, i: (b * nq + i, h)),
        out_shape=jax.ShapeDtypeStruct((batch * q_rows, heads * dv), BF16),
        compiler_params=_params("arbitrary", "arbitrary", "arbitrary"),
        name="diff_attention" if diff else "mla_attention",
    )(lam, q, gain, *([k, v] * len(segments)))


ROPE_GROUP = 64
ROPE_QUARTER = ROPE_GROUP // 4


def _axial_rope_tables(n_tok):
    t = jnp.arange(n_tok)
    row = (t // GRID_W).astype(F32)
    col = (t % GRID_W).astype(F32)
    axis_dim = ROPE_GROUP // 2
    inv = 1.0 / (ROPE_BASE ** (jnp.arange(0, axis_dim, 2, dtype=F32) / axis_dim))
    ar = row[:, None] * inv[None]
    ac = col[:, None] * inv[None]
    ang = jnp.concatenate([ar, ar, ac, ac], axis=-1)
    sign = jnp.where((jnp.arange(ROPE_GROUP) % (2 * ROPE_QUARTER)) < ROPE_QUARTER, -1.0, 1.0)
    return jnp.cos(ang), jnp.sin(ang) * sign


def _rope_lanes(x, cos, sin_signed):
    lane = lax.broadcasted_iota(jnp.int32, x.shape, 1)
    takes_next = (lane % (2 * ROPE_QUARTER)) < ROPE_QUARTER
    rot = jnp.where(takes_next, pltpu.roll(x, LANES - ROPE_QUARTER, 1), pltpu.roll(x, ROPE_QUARTER, 1))
    return x * cos + rot * sin_signed


def _rope_table_spec(rows, tm):
    per_seq = rows.n_lat // tm
    return pl.BlockSpec((tm, LANES), lambda i: (jnp.where(i * tm < rows.m_lat, i % per_seq, 0), 0))


def _tile_tables(i, tm, rows, cos_ref, sin_ref):
    is_lat = i * tm < rows.m_lat
    return jnp.where(is_lat, cos_ref[...], 1.0), jnp.where(is_lat, sin_ref[...], 0.0)


def _diff_prep_kernel(q_ref, k_ref, v_ref, cos_ref, sin_ref, qo_ref, ko_ref, vo_ref, *, rows):
    tm = q_ref.shape[0]
    cos, sin = _tile_tables(pl.program_id(0), tm, rows, cos_ref, sin_ref)
    for src, dst in ((q_ref, qo_ref), (k_ref, ko_ref)):
        for cb in range(src.shape[1] // LANES):
            cols = slice(cb * LANES, (cb + 1) * LANES)
            dst[:, cols] = _rope_lanes(src[:, cols], cos, sin).astype(dst.dtype)
    vo_ref[...] = v_ref[...].astype(vo_ref.dtype)


def diff_prep(pd, rows, cos2, sin2, *, tm=PREP_ROWS):
    m = pd.shape[0]
    bw = pd.shape[1] // 3
    tm = min(tm, rows.n_ctx)
    assert m % tm == 0 and rows.n_ctx % tm == 0 and rows.n_lat % tm == 0
    out = jax.ShapeDtypeStruct((m, bw), BF16)
    one = pl.BlockSpec((tm, bw), lambda i: (i, 0))
    tab = _rope_table_spec(rows, tm)
    return pl.pallas_call(
        functools.partial(_diff_prep_kernel, rows=rows),
        grid=(m // tm,),
        in_specs=[pl.BlockSpec((tm, bw), functools.partial(lambda i, j: (i, j), j=j)) for j in range(3)] + [tab, tab],
        out_specs=[one] * 3,
        out_shape=[out] * 3,
        compiler_params=_params("arbitrary"),
        name="diff_prep",
    )(pd, pd, pd, cos2, sin2)


def _mla_q_kernel(cq_ref, g_ref, w_ref, cos_ref, sin_ref, o_ref, *, rows):
    tm = cq_ref.shape[0]
    cos, sin = _tile_tables(pl.program_id(0), tm, rows, cos_ref, sin_ref)
    q = _dot((_rms(cq_ref[...]) * g_ref[...]).astype(BF16), w_ref[...])
    for h in range(MLA_HEADS):
        nope = slice(2 * h * LANES, (2 * h + 1) * LANES)
        rope = slice((2 * h + 1) * LANES, (2 * h + 2) * LANES)
        o_ref[:, nope] = q[:, nope].astype(o_ref.dtype)
        o_ref[:, rope] = _rope_lanes(q[:, rope], cos, sin).astype(o_ref.dtype)


def mla_q_prep(pb, rows, q_norm, w_uq, cos2, sin2, *, tm=PREP_ROWS):
    m = pb.shape[0]
    tm = min(tm, rows.n_ctx)
    w = w_uq.reshape(MLA_Q_RANK, MLA_HEADS, MLA_NOPE + MLA_ROPE)
    w = jnp.pad(w, ((0, 0), (0, 0), (0, 2 * LANES - MLA_NOPE - MLA_ROPE))).reshape(MLA_Q_RANK, -1).astype(BF16)
    tab = _rope_table_spec(rows, tm)
    return pl.pallas_call(
        functools.partial(_mla_q_kernel, rows=rows),
        grid=(m // tm,),
        in_specs=[pl.BlockSpec((tm, MLA_Q_RANK), lambda i: (i, 0)), _full((1, MLA_Q_RANK)), _full(w.shape), tab, tab],
        out_specs=pl.BlockSpec((tm, w.shape[1]), lambda i: (i, 0)),
        out_shape=jax.ShapeDtypeStruct((m, w.shape[1]), BF16),
        compiler_params=_params("arbitrary"),
        name="mla_q_prep",
    )(pb, q_norm.reshape(1, MLA_Q_RANK), w, cos2, sin2)


def _mla_kv_kernel(ckv_ref, kr_ref, g_ref, w_ref, cos_ref, sin_ref, k_ref, v_ref, *, rows):
    tm = ckv_ref.shape[0]
    cos, sin = _tile_tables(pl.program_id(0), tm, rows, cos_ref, sin_ref)
    kv = _dot((_rms(ckv_ref[...]) * g_ref[...]).astype(BF16), w_ref[...])
    lane = lax.broadcasted_iota(jnp.int32, (tm, LANES), 1)
    kr = jnp.where(lane < MLA_ROPE, _rope_lanes(kr_ref[...], cos, sin), 0.0).astype(k_ref.dtype)
    nope_w = MLA_HEADS * MLA_NOPE
    for h in range(MLA_HEADS):
        k_ref[:, 2 * h * LANES:(2 * h + 1) * LANES] = kv[:, h * MLA_NOPE:(h + 1) * MLA_NOPE].astype(k_ref.dtype)
        k_ref[:, (2 * h + 1) * LANES:(2 * h + 2) * LANES] = kr
    v_ref[...] = kv[:, nope_w:].astype(v_ref.dtype)


def mla_kv_prep(pb, rows, kv_norm, w_ukv, cos2, sin2, *, tm=PREP_ROWS):
    m = pb.shape[0]
    tm = min(tm, rows.n_ctx)
    assert MLA_NOPE == LANES and MLA_Q_RANK % MLA_KV_RANK == 0 and (MLA_Q_RANK + MLA_KV_RANK) % LANES == 0
    w = w_ukv.reshape(MLA_KV_RANK, MLA_HEADS, 2 * MLA_NOPE)
    w = jnp.concatenate([w[:, :, :MLA_NOPE].reshape(MLA_KV_RANK, -1), w[:, :, MLA_NOPE:].reshape(MLA_KV_RANK, -1)],
                        axis=1).astype(BF16)
    nope_w = MLA_HEADS * MLA_NOPE
    tab = _rope_table_spec(rows, tm)
    return pl.pallas_call(
        functools.partial(_mla_kv_kernel, rows=rows),
        grid=(m // tm,),
        in_specs=[pl.BlockSpec((tm, MLA_KV_RANK), lambda i: (i, MLA_Q_RANK // MLA_KV_RANK)),
                  pl.BlockSpec((tm, LANES), lambda i: (i, (MLA_Q_RANK + MLA_KV_RANK) // LANES)),
                  _full((1, MLA_KV_RANK)), _full(w.shape), tab, tab],
        out_specs=[pl.BlockSpec((tm, 2 * nope_w), lambda i: (i, 0)), pl.BlockSpec((tm, nope_w), lambda i: (i, 0))],
        out_shape=[jax.ShapeDtypeStruct((m, 2 * nope_w), BF16), jax.ShapeDtypeStruct((m, nope_w), BF16)],
        compiler_params=_params("arbitrary"),
        name="mla_kv_prep",
    )(pb, pb, kv_norm.reshape(1, MLA_KV_RANK), w, cos2, sin2)


def kernel(x, c, ctx, c_ctx, ada_w, ada_b, norm_mix_pre, norm_mix_post, norm_ffn_pre, norm_ffn_post, w_in, rwkv_shift, rwkv_w0, rwkv_w_up, rwkv_a0, rwkv_a_up, rwkv_g_up, rwkv_k_k, rwkv_k_a, rwkv_r_k, rwkv_ln_g, rwkv_ln_b, mla_q_norm, mla_w_uq, mla_kv_norm, mla_w_ukv, ssd_conv_w, ssd_conv_b, ssd_dt_bias, ssd_a_log, ssd_d, ssd_norm, diff_lq1, diff_lk1, diff_lq2, diff_lk2, diff_subln, w_branch, w_gate, w_out, w_ff1, w_ff2):
    batch, n_lat, d = x.shape
    n_ctx = ctx.shape[1]
    depth = ada_w.shape[0]
    bw = d // N_BRANCH
    rows = _Rows(batch, n_lat, n_ctx)
    m_lat, m_all = rows.m_lat, rows.m
    ssd_heads = bw // SSD_HEAD
    gn = SSD_GROUPS * SSD_STATE
    rwkv_in = 3 * bw + RWKV_DECAY_RANK + RWKV_AAA_RANK + RWKV_GATE_RANK
    mla_in = MLA_Q_RANK + MLA_KV_RANK + MLA_ROPE
    ssd_in = 2 * bw + 2 * gn + ssd_heads
    diff_in = 3 * bw
    o_mla, o_ssd, o_diff = rwkv_in, rwkv_in + mla_in, rwkv_in + mla_in + ssd_in
    assert w_in.shape[-1] == o_diff + diff_in
    assert n_lat % SSD_CHUNK == 0 and n_ctx % SSD_CHUNK == 0

    assert MLA_ROPE == ROPE_GROUP and DIFF_HEAD == ROPE_GROUP
    cos2, sin2 = (jnp.tile(t, (1, LANES // ROPE_GROUP)) for t in _axial_rope_tables(n_lat))
    hid = jnp.arange(bw) // RWKV_HEAD
    seg_ones = (hid[:, None] == hid[None, :]).astype(BF16)

    xs = jnp.concatenate([x.reshape(m_lat, d), ctx.reshape(batch * n_ctx, d)], axis=0)
    cvec = jnp.zeros((8, d), F32).at[:batch].set(jax.nn.silu(c)).at[batch].set(jax.nn.silu(c_ctx))
    groups = batch + 1

    mods = [(matmul(cvec, ada_w, layer=l, tn=512) + ada_b[l])[:groups].reshape(groups, 1, 6 * d)
            for l in range(depth)]
    h = norm_modulate(xs, norm_mix_pre[0], mods[0], 0, 1, rows=m_all, rows_per_group=n_lat)

    for l in range(depth):
        ctx_out = l < depth - 1
        m_out = m_all if ctx_out else m_lat
        mod = mods[l]

        o_dt = o_diff - ssd_heads
        pa = matmul(h, w_in[l, :, :o_mla], tm=1024, tn=512)
        pb = matmul(h, jnp.concatenate([w_in[l, :, o_mla:o_ssd], w_in[l, :, o_dt:o_diff]], axis=1), tm=1024, tn=384)
        pc = matmul(h, w_in[l, :, o_ssd:o_dt])
        pd = matmul(h, w_in[l, :, o_diff:])

        prep = rwkv_prep(pa, rows, l, rwkv_shift, rwkv_w0, rwkv_w_up, rwkv_a0, rwkv_a_up, rwkv_g_up,
                         rwkv_k_k, rwkv_k_a, rwkv_r_k.reshape(depth, bw), seg_ones)
        y_rwkv = rwkv_scan(prep[:7], prep[7], rows)
        ya = rwkv_readout(y_rwkv, prep[9], prep[8], seg_ones, rwkv_ln_g[l], rwkv_ln_b[l], rows=m_out)

        q_m = mla_q_prep(pb, rows, mla_q_norm[l], mla_w_uq[l], cos2, sin2)
        k_m, vv = mla_kv_prep(pb, rows, mla_kv_norm[l], mla_w_ukv[l], cos2, sin2)
        ctx_blk = m_lat // n_ctx
        segs_lat = [(n_lat, 0), (n_ctx, ctx_blk)]
        mla_scale = (MLA_NOPE + MLA_ROPE) ** -0.5
        yb = attention(q_m, k_m, vv, heads=MLA_HEADS, batch=batch, q_rows=n_lat, q_first=0, segments=segs_lat,
                       scale=mla_scale, tq=512, row_parts=2)
        if ctx_out:
            yb_c = attention(q_m, k_m, vv, heads=MLA_HEADS, batch=batch, q_rows=n_ctx, q_first=ctx_blk,
                             segments=[(n_ctx, ctx_blk)], scale=mla_scale)
            yb = jnp.concatenate([yb, yb_c], axis=0)

        assert (2 * gn) % bw == 0
        xbc = ssd_prep(pc, rows, ssd_conv_w[l], ssd_conv_b[l], width=bw, first_block=1, n_blocks=1 + 2 * gn // bw)
        dt_raw = pb[:, mla_in:mla_in + ssd_heads]
        dts = jnp.stack([jax.nn.softplus(dt_raw + ssd_dt_bias[l, dr]) for dr in range(2)])
        las = dts * (-jnp.exp(ssd_a_log[l]))[:, None, :]
        y_ssd = ssd_scan(xbc, las, dts, rows, heads=ssd_heads)
        yc = ssd_readout(y_ssd, xbc, pc, ssd_d[l], ssd_norm[l], rows=m_out)

        q_d, k_d, v_d = diff_prep(pd, rows, cos2, sin2)
        lam_init = 0.8 - 0.6 * math.exp(-0.3 * l)
        lam = (jnp.exp(jnp.sum(diff_lq1[l] * diff_lk1[l])) - jnp.exp(jnp.sum(diff_lq2[l] * diff_lk2[l])) + lam_init)
        diff_kw = dict(heads=DIFF_HEADS, batch=batch, scale=DIFF_HEAD ** -0.5, diff_lam=lam,
                       diff_gain=diff_subln[l], out_scale=1.0 - lam_init)
        yd = attention(q_d, k_d, v_d, q_rows=n_lat, q_first=0, segments=segs_lat, tq=1024, row_parts=2, **diff_kw)
        if ctx_out:
            yd_c = attention(q_d, k_d, v_d, q_rows=n_ctx, q_first=ctx_blk, segments=[(n_ctx, ctx_blk)], **diff_kw)
            yd = jnp.concatenate([yd, yd_c], axis=0)

        gates = matmul(h, w_gate, layer=l, rows=m_out, act="sigmoid", out_dtype=BF16, tm=1024, tn=512, out_slab=512)
        merged = merge_branches(gates, [ya, yb, yc, yd], w_branch, l, rows=m_out)
        mix = matmul(merged, w_out, layer=l)
        xs, hf = norm_residual(xs, mix, norm_mix_post[l], mod, 2, rows=m_out, rows_per_group=n_lat,
                               next_norm=(norm_ffn_pre[l], mod, 3, 4))
        f1 = matmul(hf, w_ff1, layer=l, act="relu2", out_dtype=BF16, out_slab=d)
        f2 = matmul(f1, w_ff2, layer=l, tn=512)
        if ctx_out:
            xs, h = norm_residual(xs, f2, norm_ffn_post[l], mod, 5, rows=m_out, rows_per_group=n_lat,
                                  next_norm=(norm_mix_pre[l + 1], mods[l + 1], 0, 1))
        else:
            xs = norm_residual(xs, f2, norm_ffn_post[l], mod, 5, rows=m_out, rows_per_group=n_lat)

    return xs[:m_lat].reshape(batch, n_lat, d)
```

```python
import functools
import math

import jax
import jax.numpy as jnp
from jax import lax
from jax.experimental import pallas as pl
from jax.experimental.pallas import tpu as pltpu

F32 = jnp.float32
BF16 = jnp.bfloat16
HIGHEST = lax.Precision.HIGHEST

NORM_EPS = 1e-6
ROPE_BASE = 10000.0
GRID_W = 64
N_BRANCH = 4
LANES = 128
SUBLANES = 8

RWKV_HEAD = 64
RWKV_DECAY_RANK = 64
RWKV_AAA_RANK = 64
RWKV_GATE_RANK = 128
RWKV_LN_EPS = 64e-5
RWKV_CHUNK = 64

MLA_HEADS = 8
MLA_NOPE = 128
MLA_ROPE = 64
MLA_Q_RANK = 768
MLA_KV_RANK = 256

SSD_HEAD = 64
SSD_GROUPS = 4
SSD_STATE = 128
SSD_CHUNK = 128

DIFF_HEADS = 8
DIFF_HEAD = 64

PREP_ROWS = 256

VMEM_LIMIT_BYTES = 56 * 1024 * 1024


def _params(*sem):
    return pltpu.CompilerParams(dimension_semantics=sem, vmem_limit_bytes=VMEM_LIMIT_BYTES)


def _dot(a, b):
    return jnp.dot(a, b, preferred_element_type=F32)


def _dot_nt(a, b):
    return lax.dot_general(a, b, (((1,), (1,)), ((), ())), preferred_element_type=F32)


def _dot_tn(a, b):
    return lax.dot_general(a, b, (((0,), (0,)), ((), ())), preferred_element_type=F32)


def _dot_hi(a, b):
    return jnp.dot(a, b, preferred_element_type=F32, precision=HIGHEST)


def _split_bf16(x, terms):
    parts = []
    for _ in range(terms):
        p = x.astype(BF16)
        parts.append(p)
        x = x - p.astype(F32)
    return parts


def _dot_split_lhs(x, e, terms=2):
    return sum(_dot(p, e) for p in _split_bf16(x, terms))


def _dot_split_rhs(e, x, terms=2):
    return sum(_dot(e, p) for p in _split_bf16(x, terms))


def _dot3(a, b):
    a_hi, a_lo = _split_bf16(a, 2)
    b_hi, b_lo = _split_bf16(b, 2)
    return _dot(a_hi, b_hi) + (_dot(a_lo, b_hi) + _dot(a_hi, b_lo))


def _full(shape):
    nd = len(shape)
    return pl.BlockSpec(shape, lambda *_: (0,) * nd)


def _apply_act(y, act):
    if act is None:
        return y
    if act == "sigmoid":
        return 0.5 * jnp.tanh((0.5 * y).astype(BF16)) + 0.5
    if act == "relu2":
        return jnp.square(jnp.maximum(y, 0.0))
    raise ValueError(act)


def _mm_kernel(x_ref, w_ref, o_ref, wbf_ref, *acc, nk, act):
    k = pl.program_id(1)
    i = pl.program_id(2)

    @pl.when(i == 0)
    def _():
        wbf_ref[...] = w_ref[...].astype(BF16)

    y = _dot(x_ref[...].astype(BF16), wbf_ref[...])
    if nk == 1:
        o_ref[...] = _apply_act(y, act).astype(o_ref.dtype)
        return
    (acc_ref,) = acc
    tm = x_ref.shape[0]
    rows = pl.ds(pl.multiple_of(i * tm, tm), tm)

    @pl.when(k == 0)
    def _():
        acc_ref[rows, :] = y

    @pl.when(jnp.logical_and(k > 0, k < nk - 1))
    def _():
        acc_ref[rows, :] += y

    @pl.when(k == nk - 1)
    def _():
        o_ref[...] = _apply_act(acc_ref[rows, :] + y, act).astype(o_ref.dtype)


def matmul(x, w, *, layer=None, rows=None, tm=512, tn=1024, tk=4096, act=None, out_dtype=F32):
    m = x.shape[0] if rows is None else rows
    kdim = x.shape[1]
    n = w.shape[-1]
    assert w.shape[-2] == kdim
    tm = math.gcd(tm, m)
    tk = min(tk, kdim)
    tn = min(tn, n)
    assert m % tm == 0 and kdim % tk == 0
    ni, nk, nj = m // tm, kdim // tk, pl.cdiv(n, tn)
    if w.ndim == 3:
        w_spec = pl.BlockSpec((None, tk, tn), lambda j, k, i: (layer, k, j))
    else:
        w_spec = pl.BlockSpec((tk, tn), lambda j, k, i: (k, j))
    if nk == 1:
        o_map = lambda j, k, i: (i, j)
    else:
        o_map = lambda j, k, i: (jnp.where(k == nk - 1, i, 0), j)
    scratch = [pltpu.VMEM((tk, tn), BF16)]
    if nk > 1:
        scratch.append(pltpu.VMEM((m, tn), F32))
    return pl.pallas_call(
        functools.partial(_mm_kernel, nk=nk, act=act),
        grid=(nj, nk, ni),
        in_specs=[pl.BlockSpec((tm, tk), lambda j, k, i: (i, k)), w_spec],
        out_specs=pl.BlockSpec((tm, tn), o_map),
        out_shape=jax.ShapeDtypeStruct((m, n), out_dtype),
        scratch_shapes=scratch,
        compiler_params=_params("arbitrary", "arbitrary", "arbitrary"),
        name="matmul_ws",
    )(x, w)


def _merge_kernel(g0, g1, g2, g3, y0, y1, y2, y3, w_ref, o_ref, wbf_ref):
    @pl.when(pl.program_id(1) == 0)
    def _():
        wbf_ref[...] = w_ref[...].astype(BF16)

    acc = None
    for b, (g, y) in enumerate(((g0, y0), (g1, y1), (g2, y2), (g3, y3))):
        t = g[...].astype(F32) * _dot(y[...].astype(BF16), wbf_ref[b])
        acc = t if acc is None else acc + t
    o_ref[...] = acc.astype(o_ref.dtype)


def merge_branches(gates, ys, w_branch, layer, *, rows, tm=1024, tn=512):
    d = w_branch.shape[-1]
    bw = w_branch.shape[-2]
    tm = math.gcd(tm, rows)
    assert rows % tm == 0 and d % tn == 0
    nj = d // tn
    g_specs = [pl.BlockSpec((tm, tn), functools.partial(lambda j, i, b: (i, b * nj + j), b=b))
               for b in range(N_BRANCH)]
    y_specs = [pl.BlockSpec((tm, bw), lambda j, i: (i, 0)) for _ in range(N_BRANCH)]
    return pl.pallas_call(
        _merge_kernel,
        grid=(nj, rows // tm),
        in_specs=g_specs + y_specs + [pl.BlockSpec((None, N_BRANCH, bw, tn), lambda j, i: (layer, 0, 0, j))],
        out_specs=pl.BlockSpec((tm, tn), lambda j, i: (i, j)),
        out_shape=jax.ShapeDtypeStruct((rows, d), BF16),
        scratch_shapes=[pltpu.VMEM((N_BRANCH, bw, tn), BF16)],
        compiler_params=_params("arbitrary", "arbitrary"),
        name="merge_branches",
    )(gates, gates, gates, gates, *ys, w_branch)


def _rms(x, eps=NORM_EPS):
    return x * lax.rsqrt(jnp.mean(x * x, axis=-1, keepdims=True) + eps)


def _norm_mod_kernel(x_ref, g_ref, shift_ref, scale_ref, o_ref):
    x = x_ref[...].astype(F32)
    y = _rms(x) * g_ref[...]
    o_ref[...] = (y * (1.0 + scale_ref[...]) + shift_ref[...]).astype(o_ref.dtype)


def _group_of_rows(tm, rows_per_group, n_groups):
    return lambda i: jnp.minimum((i * tm) // rows_per_group, n_groups - 1)


def norm_modulate(x, g, mod, shift_idx, scale_idx, *, rows, rows_per_group, tm=PREP_ROWS):
    d = x.shape[1]
    tm = min(tm, rows_per_group)
    assert rows % tm == 0 and rows_per_group % tm == 0
    grp = _group_of_rows(tm, rows_per_group, mod.shape[0])
    return pl.pallas_call(
        _norm_mod_kernel,
        grid=(rows // tm,),
        in_specs=[pl.BlockSpec((tm, d), lambda i: (i, 0)),
                  pl.BlockSpec((1, d), lambda i: (0, 0)),
                  pl.BlockSpec((None, 1, d), lambda i: (grp(i), 0, shift_idx)),
                  pl.BlockSpec((None, 1, d), lambda i: (grp(i), 0, scale_idx))],
        out_specs=pl.BlockSpec((tm, d), lambda i: (i, 0)),
        out_shape=jax.ShapeDtypeStruct((rows, d), BF16),
        compiler_params=_params("arbitrary"),
        name="norm_modulate",
    )(x, g.reshape(1, d), mod, mod)


def _norm_res_kernel(x_ref, y_ref, g_ref, m_ref, *rest):
    y = _rms(y_ref[...].astype(F32)) * g_ref[...]
    x = x_ref[...] + m_ref[...] * y
    if len(rest) == 1:
        rest[0][...] = x
        return
    g2_ref, shift_ref, scale_ref, o_ref, h_ref = rest
    o_ref[...] = x
    h_ref[...] = (_rms(x) * g2_ref[...] * (1.0 + scale_ref[...]) + shift_ref[...]).astype(h_ref.dtype)


def norm_residual(x, y, g, mod, gate_idx, *, rows, rows_per_group, tm=PREP_ROWS, next_norm=None):
    d = x.shape[1]
    tm = min(tm, rows_per_group)
    assert rows % tm == 0 and rows_per_group % tm == 0
    grp = _group_of_rows(tm, rows_per_group, mod.shape[0])
    tile = pl.BlockSpec((tm, d), lambda i: (i, 0))
    vec = pl.BlockSpec((1, d), lambda i: (0, 0))
    mod_spec = lambda idx: pl.BlockSpec((None, 1, d), lambda i: (grp(i), 0, idx))
    in_specs = [tile, tile, vec, mod_spec(gate_idx)]
    args = [x, y, g.reshape(1, d), mod]
    out_specs, out_shape = tile, jax.ShapeDtypeStruct((rows, d), F32)
    if next_norm is not None:
        g2, mod2, shift_idx, scale_idx = next_norm
        in_specs += [vec, mod_spec(shift_idx), mod_spec(scale_idx)]
        args += [g2.reshape(1, d), mod2, mod2]
        out_specs, out_shape = [tile, tile], [out_shape, jax.ShapeDtypeStruct((rows, d), BF16)]
    return pl.pallas_call(
        _norm_res_kernel,
        grid=(rows // tm,),
        in_specs=in_specs,
        out_specs=out_specs,
        out_shape=out_shape,
        compiler_params=_params("arbitrary"),
        name="norm_residual",
    )(*args)


class _Rows:
    def __init__(self, batch, n_lat, n_ctx):
        self.batch, self.n_lat, self.n_ctx = batch, n_lat, n_ctx
        self.m_lat = batch * n_lat
        self.m = batch * (n_lat + n_ctx)

    def split(self, t):
        c = t.shape[-1]
        return (t[:self.m_lat].reshape(self.batch, self.n_lat, c),
                t[self.m_lat:].reshape(self.batch, self.n_ctx, c))

    def join(self, lat, ctx):
        c = lat.shape[-1]
        return jnp.concatenate([lat.reshape(self.m_lat, c), ctx.reshape(self.batch * self.n_ctx, c)], axis=0)

    def seq_edges(self, i, tm):
        r0 = i * tm
        in_lat = r0 < self.m_lat
        pos = jnp.where(in_lat, r0 % self.n_lat, (r0 - self.m_lat) % self.n_ctx)
        length = jnp.where(in_lat, self.n_lat, self.n_ctx)
        return pos == 0, pos + tm == length

    def chunk_block(self, chunk):
        ncc, nlc = self.n_ctx // chunk, self.n_lat // chunk
        ctx0 = self.m_lat // chunk

        def block(d, b, c):
            in_ctx = c < ncc
            cc = jnp.where(in_ctx, c, c - ncc)
            n_seg = jnp.where(in_ctx, ncc, nlc)
            cc = jnp.where(d == 1, n_seg - 1 - cc, cc)
            return jnp.where(in_ctx, ctx0 + b * ncc, b * nlc) + cc
        return block, ncc + nlc


def _conv3_tile(x, prev_row, next_row, w_ref, first, last):
    tm = x.shape[0]
    rid = lax.broadcasted_iota(jnp.int32, x.shape, 0)
    prev_row = jnp.where(first, 0.0, prev_row)
    next_row = jnp.where(last, 0.0, next_row)
    prev = jnp.where(rid == 0, prev_row, pltpu.roll(x, 1, 0))
    nxt = jnp.where(rid == tm - 1, next_row, pltpu.roll(x, tm - 1, 0))
    return prev * w_ref[0:1, :] + x * w_ref[1:2, :] + nxt * w_ref[2:3, :]


def _halo_specs(tm, width, col_block, n_rows):
    per = tm // SUBLANES
    last = n_rows // SUBLANES - 1
    return [pl.BlockSpec((SUBLANES, width), lambda i: (jnp.maximum(i * per - 1, 0), col_block)),
            pl.BlockSpec((SUBLANES, width), lambda i: (jnp.minimum((i + 1) * per, last), col_block))]


def _chunk_cumsum_matrix(tm, chunk, reverse):
    r = lax.broadcasted_iota(jnp.int32, (tm, tm), 0)
    c = lax.broadcasted_iota(jnp.int32, (tm, tm), 1)
    same = (r // chunk) == (c // chunk)
    tri = (r <= c) if reverse else (r >= c)
    return jnp.logical_and(same, tri).astype(BF16)


assert RWKV_CHUNK * math.exp(-0.5) < 80.0

def _rwkv_prep_kernel(rkv_ref, hp_ref, hn_ref, low_ref, shift_ref, w0_ref, a0_ref, wup_ref, aup_ref, gup_ref,
                      kk_ref, ka_ref, rk_ref, e_ref,
                      at_ref, rt_ref, bt_ref, kt_ref, bc_ref, kc_ref, gend_ref, v_ref, gate_ref, bonus_ref,
                      *, rows, chunk):
    tm = rkv_ref.shape[0]
    bw = v_ref.shape[1]
    first, last = rows.seq_edges(pl.program_id(0), tm)
    rkv = _conv3_tile(rkv_ref[...], hp_ref[SUBLANES - 1:SUBLANES, :], hn_ref[0:1, :], shift_ref, first, last)
    r, k, v = rkv[:, :bw], rkv[:, bw:2 * bw], rkv[:, 2 * bw:]
    e = e_ref[...]
    kk = k * kk_ref[...]
    kk = kk * lax.rsqrt(_dot_split_lhs(kk * kk, e) + 1e-12)
    low = low_ref[...]
    wd = low[:, :RWKV_DECAY_RANK]
    ad = low[:, RWKV_DECAY_RANK:RWKV_DECAY_RANK + RWKV_AAA_RANK]
    gd = low[:, RWKV_DECAY_RANK + RWKV_AAA_RANK:]
    w_pre = _dot3(jnp.tanh(wd), wup_ref[...])
    a_pre = _dot3(ad, aup_ref[...])
    gate_ref[...] = _dot3(jax.nn.sigmoid(gd), gup_ref[...])
    v_ref[...] = v.astype(BF16)
    kt_sum = None
    for d in range(2):
        w_log = -jax.nn.softplus(-(w0_ref[d:d + 1, :] + w_pre[:, d * bw:(d + 1) * bw])) - 0.5
        lw = -jnp.exp(w_log)
        a = jax.nn.sigmoid(a0_ref[d:d + 1, :] + a_pre[:, d * bw:(d + 1) * bw])
        kt = k * (1.0 + (a - 1.0) * ka_ref[...])
        kt_sum = kt if kt_sum is None else kt_sum + kt
        b = kk * a
        lam = _dot_split_rhs(_chunk_cumsum_matrix(tm, chunk, d == 1), lw)
        g_inv = jnp.exp(-lam)
        at_ref[d] = (-kk * jnp.exp(lam - lw)).astype(BF16)
        rt_ref[d] = (r * jnp.exp(lam)).astype(BF16)
        bt_ref[d] = (b * g_inv).astype(BF16)
        kt_ref[d] = (kt * g_inv).astype(BF16)
        for q in range(tm // chunk):
            end = q * chunk if d == 1 else (q + 1) * chunk - 1
            lam_end = lam[end:end + 1, :]
            sl = slice(q * chunk, (q + 1) * chunk)
            to_end = jnp.exp(lam_end - lam[sl])
            bc_ref[d, sl, :] = (b[sl] * to_end).astype(BF16)
            kc_ref[d, sl, :] = (kt[sl] * to_end).astype(BF16)
            gend_ref[d, q] = jnp.exp(lam_end)
    bonus_ref[...] = _dot_split_lhs(r * kt_sum * rk_ref[...], e, terms=1) * v


def rwkv_prep(pa, rows, layer, shift, w0, w_up, a0, a_up, g_up, k_k, k_a, r_k, seg_ones, *, tm=PREP_ROWS,
              chunk=RWKV_CHUNK):
    m = pa.shape[0]
    bw = k_k.shape[-1]
    low_w = RWKV_DECAY_RANK + RWKV_AAA_RANK + RWKV_GATE_RANK
    tm = min(tm, rows.n_ctx)
    assert m % tm == 0 and rows.n_ctx % tm == 0 and rows.n_lat % tm == 0 and tm % chunk == 0
    assert (3 * bw) % low_w == 0
    wup = jnp.concatenate([w_up[layer, 0], w_up[layer, 1]], axis=1)
    aup = jnp.concatenate([a_up[layer, 0], a_up[layer, 1]], axis=1)
    row = lambda t: t.reshape(1, bw)
    big = jax.ShapeDtypeStruct((2, m, bw), BF16)
    big_spec = pl.BlockSpec((2, tm, bw), lambda i: (0, i, 0))
    one_spec = pl.BlockSpec((tm, bw), lambda i: (i, 0))
    cpt = tm // chunk
    return pl.pallas_call(
        functools.partial(_rwkv_prep_kernel, rows=rows, chunk=chunk),
        grid=(m // tm,),
        in_specs=[pl.BlockSpec((tm, 3 * bw), lambda i: (i, 0))] + _halo_specs(tm, 3 * bw, 0, m)
        + [pl.BlockSpec((tm, low_w), lambda i: (i, 3 * bw // low_w)),
           _full((3, 3 * bw)), _full((2, bw)), _full((2, bw)), _full(wup.shape), _full(aup.shape),
           pl.BlockSpec((None,) + g_up.shape[1:], lambda i: (layer, 0, 0)),
           _full((1, bw)), _full((1, bw)), _full((1, bw)), _full(seg_ones.shape)],
        out_specs=[big_spec] * 6 + [pl.BlockSpec((2, cpt, 1, bw), lambda i: (0, i, 0, 0)),
                                    one_spec, one_spec, one_spec],
        out_shape=[big] * 6 + [jax.ShapeDtypeStruct((2, m // chunk, 1, bw), F32),
                               jax.ShapeDtypeStruct((m, bw), BF16), jax.ShapeDtypeStruct((m, bw), F32),
                               jax.ShapeDtypeStruct((m, bw), F32)],
        compiler_params=_params("arbitrary"),
        name="rwkv_prep",
    )(pa, pa, pa, pa, shift[layer], w0[layer], a0[layer], wup, aup, g_up, row(k_k[layer]), row(k_a[layer]),
      row(r_k[layer]), seg_ones)


def _pair_select(lo, z):
    half = z.shape[0] // 2
    return jnp.where(lo, z[:half], z[half:])


def _rwkv_scan_chunks(refs, y_ref, s_ref, *, chunk, reverse, order_of_chunks):
    at_ref, rt_ref, bt_ref, kt_ref, bc_ref, kc_ref, gend_ref, v_ref = refs
    c2 = 2 * chunk
    n_pairs = s_ref.shape[0]
    pw = 2 * RWKV_HEAD
    n_levels = chunk.bit_length() - 1
    order = lax.broadcasted_iota(jnp.int32, (chunk, chunk), 0) - lax.broadcasted_iota(jnp.int32, (chunk, chunk), 1)
    if reverse:
        order = -order
    strict = order > 0
    incl = order >= 0
    lo = lax.broadcasted_iota(jnp.int32, (chunk, pw), 1) < RWKV_HEAD
    lo2 = lax.broadcasted_iota(jnp.int32, (c2, pw), 1) < RWKV_HEAD
    own = ((lax.broadcasted_iota(jnp.int32, (pw, pw), 0) < RWKV_HEAD)
           == (lax.broadcasted_iota(jnp.int32, (pw, pw), 1) < RWKV_HEAD))
    pairs = range(n_pairs)
    sl = [slice(j * pw, (j + 1) * pw) for j in pairs]

    def tri(mask, z):
        return jnp.where(mask, z, 0.0)

    parts = {}
    for q in order_of_chunks:
        rs = slice(q * chunk, (q + 1) * chunk)
        a = [at_ref[rs, s] for s in sl]
        r = [rt_ref[rs, s] for s in sl]
        v = [v_ref[rs, s] for s in sl]
        ar = [jnp.concatenate([a[j], r[j]], axis=0) for j in pairs]
        ar_st = [jnp.concatenate([jnp.where(lo2, ar[j], 0), jnp.where(lo2, 0, ar[j])], axis=0) for j in pairs]
        p_b = [_dot_nt(ar_st[j], bt_ref[rs, sl[j]]) for j in pairs]
        p_k = [_dot_nt(ar_st[j], kt_ref[rs, sl[j]]) for j in pairs]
        lp = [[tri(strict, p_b[j][e * c2:e * c2 + chunk]) for e in range(2)] for j in pairs]
        l_ak = [jnp.concatenate([tri(strict, p_k[j][e * c2:e * c2 + chunk]) for e in range(2)], axis=0).astype(BF16)
                for j in pairs]
        m_rb = [jnp.concatenate([tri(incl, p_b[j][e * c2 + chunk:(e + 1) * c2]) for e in range(2)],
                                axis=0).astype(BF16) for j in pairs]
        m_rk = [jnp.concatenate([tri(incl, p_k[j][e * c2 + chunk:(e + 1) * c2]) for e in range(2)],
                                axis=0).astype(BF16) for j in pairs]
        x_w = [a[j].astype(F32) for j in pairs]
        x_u = [_pair_select(lo, _dot(l_ak[j], v[j])) for j in pairs]
        for lvl in range(n_levels):
            lpb = [[lp[j][e].astype(BF16) for e in range(2)] for j in pairs]
            lst = [jnp.concatenate(lpb[j], axis=0) for j in pairs]
            z = [_dot(lst[j], jnp.concatenate([x_w[j], x_u[j]], axis=1).astype(BF16)) for j in pairs]
            x_w = [x_w[j] + _pair_select(lo, z[j][:, :pw]) for j in pairs]
            x_u = [x_u[j] + _pair_select(lo, z[j][:, pw:]) for j in pairs]
            if lvl + 1 < n_levels:
                lp = [[_dot(lpb[j][e], lpb[j][e]) for e in range(2)] for j in pairs]
        parts[q] = (rs, r, v, m_rb, m_rk, [x.astype(BF16) for x in x_w], x_u)
    s = [s_ref[j] for j in pairs]
    for q in order_of_chunks:
        rs, r, v, m_rb, m_rk, x_wb, x_u = parts[q]
        sb = [s[j].astype(BF16) for j in pairs]
        u = [_dot_nt(x_wb[j], sb[j]) + x_u[j] for j in pairs]
        ub = [u[j].astype(BF16) for j in pairs]
        y = [_dot_nt(r[j], sb[j]) + _pair_select(lo, _dot(m_rb[j], ub[j])) + _pair_select(lo, _dot(m_rk[j], v[j]))
             for j in pairs]
        s = [s[j] * gend_ref[q, :, sl[j]]
             + jnp.where(own, _dot_tn(ub[j], bc_ref[rs, sl[j]]) + _dot_tn(v[j], kc_ref[rs, sl[j]]), 0.0)
             for j in pairs]
        for j in pairs:
            y_ref[rs, sl[j]] = y[j]
    for j in pairs:
        s_ref[j] = s[j]


def _rwkv_scan_kernel(*refs, chunk, per_step):
    y_ref, s_ref = refs[-2:]

    @pl.when(pl.program_id(2) == 0)
    def _():
        s_ref[...] = jnp.zeros_like(s_ref)

    @pl.when(pl.program_id(0) == 0)
    def _():
        _rwkv_scan_chunks(refs[:-2], y_ref, s_ref, chunk=chunk, reverse=False,
                          order_of_chunks=tuple(range(per_step)))

    @pl.when(pl.program_id(0) == 1)
    def _():
        _rwkv_scan_chunks(refs[:-2], y_ref, s_ref, chunk=chunk, reverse=True,
                          order_of_chunks=tuple(reversed(range(per_step))))


def rwkv_scan(ops, v, rows, *, chunk=RWKV_CHUNK, chunks_per_step=4):
    at, rt, bt, kt, bc, kc, gend = ops
    _, m, bw = at.shape
    per = math.gcd(chunks_per_step, math.gcd(rows.n_ctx // chunk, rows.n_lat // chunk))
    rows_per_step = per * chunk
    block, nc = rows.chunk_block(rows_per_step)
    big = pl.BlockSpec((None, rows_per_step, bw), lambda d, b, c: (d, block(d, b, c), 0))
    return pl.pallas_call(
        functools.partial(_rwkv_scan_kernel, chunk=chunk, per_step=per),
        grid=(2, rows.batch, nc),
        in_specs=[big] * 6 + [pl.BlockSpec((None, per, 1, bw), lambda d, b, c: (d, block(d, b, c), 0, 0)),
                              pl.BlockSpec((rows_per_step, bw), lambda d, b, c: (block(d, b, c), 0))],
        out_specs=big,
        out_shape=jax.ShapeDtypeStruct((2, m, bw), F32),
        scratch_shapes=[pltpu.VMEM((bw // (2 * RWKV_HEAD), 2 * RWKV_HEAD, 2 * RWKV_HEAD), F32)],
        compiler_params=_params("arbitrary", "arbitrary", "arbitrary"),
        name="rwkv_scan",
    )(at, rt, bt, kt, bc, kc, gend, v)


def _rwkv_readout_kernel(y_ref, bonus_ref, gate_ref, e_ref, g_ref, b_ref, o_ref):
    y = y_ref[0] + y_ref[1]
    e = e_ref[...]
    mu = _dot_split_lhs(y, e) * (1.0 / RWKV_HEAD)
    yc = y - mu
    var = _dot_split_lhs(yc * yc, e) * (1.0 / RWKV_HEAD)
    yn = yc * lax.rsqrt(var + RWKV_LN_EPS) * g_ref[...] + b_ref[...]
    o_ref[...] = ((yn + bonus_ref[...]) * gate_ref[...]).astype(o_ref.dtype)


def rwkv_readout(y, bonus, gate, seg_ones, ln_g, ln_b, *, rows, tm=PREP_ROWS):
    bw = y.shape[-1]
    tm = math.gcd(tm, rows)
    one = pl.BlockSpec((tm, bw), lambda i: (i, 0))
    return pl.pallas_call(
        _rwkv_readout_kernel,
        grid=(rows // tm,),
        in_specs=[pl.BlockSpec((2, tm, bw), lambda i: (0, i, 0)), one, one, _full(seg_ones.shape),
                  _full((1, bw)), _full((1, bw))],
        out_specs=one,
        out_shape=jax.ShapeDtypeStruct((rows, bw), BF16),
        compiler_params=_params("arbitrary"),
        name="rwkv_readout",
    )(y, bonus, gate, seg_ones, ln_g.reshape(1, bw), ln_b.reshape(1, bw))


def _ssd_prep_kernel(x_ref, hp_ref, hn_ref, w_ref, b_ref, o_ref, *, rows):
    first, last = rows.seq_edges(pl.program_id(0), x_ref.shape[0])
    y = _conv3_tile(x_ref[...], hp_ref[SUBLANES - 1:SUBLANES, :], hn_ref[0:1, :], w_ref, first, last) + b_ref[...]
    o_ref[...] = y * jax.nn.sigmoid(y)


def ssd_prep(pc, rows, conv_w, conv_b, *, width, first_block, n_blocks, tm=PREP_ROWS):
    m = pc.shape[0]
    tm = min(tm, rows.n_ctx)
    assert m % tm == 0 and rows.n_ctx % tm == 0 and rows.n_lat % tm == 0
    per = tm // SUBLANES
    last = m // SUBLANES - 1
    return pl.pallas_call(
        functools.partial(_ssd_prep_kernel, rows=rows),
        grid=(m // tm, n_blocks),
        in_specs=[pl.BlockSpec((tm, width), lambda i, j: (i, first_block + j)),
                  pl.BlockSpec((SUBLANES, width), lambda i, j: (jnp.maximum(i * per - 1, 0), first_block + j)),
                  pl.BlockSpec((SUBLANES, width), lambda i, j: (jnp.minimum((i + 1) * per, last), first_block + j)),
                  pl.BlockSpec((3, width), lambda i, j: (0, j)),
                  pl.BlockSpec((1, width), lambda i, j: (0, j))],
        out_specs=pl.BlockSpec((tm, width), lambda i, j: (i, j)),
        out_shape=jax.ShapeDtypeStruct((m, n_blocks * width), F32),
        compiler_params=_params("arbitrary", "arbitrary"),
        name="ssd_prep",
    )(pc, pc, pc, conv_w, conv_b.reshape(1, n_blocks * width))


def _ssd_scan_kernel(x_ref, b_ref, c_ref, la_ref, lat_ref, dt_ref, y_ref, s_ref, *, chunk, heads, groups):
    @pl.when(pl.program_id(2) == 0)
    def _():
        s_ref[...] = jnp.zeros_like(s_ref)

    rev = pl.program_id(0) == 1
    pw = 2 * SSD_HEAD
    order = lax.broadcasted_iota(jnp.int32, (chunk, chunk), 0) - lax.broadcasted_iota(jnp.int32, (chunk, chunk), 1)
    order = jnp.where(rev, -order, order)
    upto = order >= 0
    lo = lax.broadcasted_iota(jnp.int32, (chunk, pw), 1) < SSD_HEAD
    acs_c_all = _dot_hi(upto.astype(F32), la_ref[...])
    acs_r_all = _dot_hi(lat_ref[...], (order <= 0).astype(F32))
    tot_c_all = jnp.where(rev, acs_c_all[0:1, :], acs_c_all[chunk - 1:chunk, :])
    dt_all = dt_ref[...]
    hpg = heads // groups
    for gi in range(groups):
        cg = c_ref[:, gi * SSD_STATE:(gi + 1) * SSD_STATE].astype(BF16)
        bg = b_ref[:, gi * SSD_STATE:(gi + 1) * SSD_STATE].astype(BF16)
        cb = _dot_nt(cg, bg)
        for qi in range(hpg // 2):
            h0 = gi * hpg + 2 * qi
            j = h0 // 2
            cols = slice(j * pw, (j + 1) * pw)
            w_st, acs_c, tot = [], [], []
            for e in range(2):
                h = h0 + e
                acs_c.append(acs_c_all[:, h:h + 1])
                tot.append(tot_c_all[:, h:h + 1])
                decay = jnp.exp(jnp.where(upto, acs_c[e] - acs_r_all[h:h + 1, :], -1e30))
                w_st.append((cb * decay).astype(BF16))
            xdt = x_ref[:, cols] * jnp.where(lo, dt_all[:, h0:h0 + 1], dt_all[:, h0 + 1:h0 + 2])
            st = s_ref[j]
            y = _pair_select(lo, _dot(jnp.concatenate(w_st, axis=0), xdt.astype(BF16)))
            y = y + _dot(cg, st.astype(BF16)) * jnp.where(lo, jnp.exp(acs_c[0]), jnp.exp(acs_c[1]))
            y_ref[:, cols] = y
            to_end = jnp.where(lo, jnp.exp(tot[0] - acs_c[0]), jnp.exp(tot[1] - acs_c[1]))
            s_ref[j] = (st * jnp.where(lo[0:1, :], jnp.exp(tot[0]), jnp.exp(tot[1]))
                        + _dot_tn(bg, (xdt * to_end).astype(BF16)))


def ssd_scan(xbc, la, dt, rows, *, heads, chunk=SSD_CHUNK):
    m = xbc.shape[0]
    bw = heads * SSD_HEAD
    gn = SSD_GROUPS * SSD_STATE
    assert bw % gn == 0
    block, nc = rows.chunk_block(chunk)
    lat = jnp.swapaxes(la, 1, 2)
    thin = pl.BlockSpec((None, chunk, heads), lambda d, b, c: (d, block(d, b, c), 0))
    return pl.pallas_call(
        functools.partial(_ssd_scan_kernel, chunk=chunk, heads=heads, groups=SSD_GROUPS),
        grid=(2, rows.batch, nc),
        in_specs=[pl.BlockSpec((chunk, bw), lambda d, b, c: (block(d, b, c), 0)),
                  pl.BlockSpec((chunk, gn), lambda d, b, c: (block(d, b, c), bw // gn)),
                  pl.BlockSpec((chunk, gn), lambda d, b, c: (block(d, b, c), bw // gn + 1)),
                  thin,
                  pl.BlockSpec((None, heads, chunk), lambda d, b, c: (d, 0, block(d, b, c))),
                  thin],
        out_specs=pl.BlockSpec((None, chunk, bw), lambda d, b, c: (d, block(d, b, c), 0)),
        out_shape=jax.ShapeDtypeStruct((2, m, bw), F32),
        scratch_shapes=[pltpu.VMEM((heads // 2, SSD_STATE, 2 * SSD_HEAD), F32)],
        compiler_params=_params("arbitrary", "arbitrary", "arbitrary"),
        name="ssd_scan",
    )(xbc, xbc, xbc, la, lat, dt)


def _ssd_readout_kernel(y_ref, x_ref, z_ref, d_ref, g_ref, o_ref):
    y = y_ref[0] + y_ref[1] + d_ref[...] * x_ref[...]
    z = z_ref[...]
    o_ref[...] = (_rms(y * (z * jax.nn.sigmoid(z))) * g_ref[...]).astype(o_ref.dtype)


def ssd_readout(y, xbc, pc, d_skip, norm_g, *, rows, tm=PREP_ROWS):
    bw = y.shape[-1]
    tm = math.gcd(tm, rows)
    one = pl.BlockSpec((tm, bw), lambda i: (i, 0))
    return pl.pallas_call(
        _ssd_readout_kernel,
        grid=(rows // tm,),
        in_specs=[pl.BlockSpec((2, tm, bw), lambda i: (0, i, 0)), one, one, _full((1, bw)), _full((1, bw))],
        out_specs=one,
        out_shape=jax.ShapeDtypeStruct((rows, bw), BF16),
        compiler_params=_params("arbitrary"),
        name="ssd_readout",
    )(y, xbc, pc, jnp.repeat(d_skip, SSD_HEAD).reshape(1, bw), norm_g.reshape(1, bw))


def _softmax_pv(scores, values):
    m = None
    for s in scores:
        ms = jnp.max(s, axis=-1, keepdims=True)
        m = ms if m is None else jnp.maximum(m, ms)
    l, o = None, None
    for s, v in zip(scores, values):
        p = jnp.exp(s - m)
        ls = jnp.sum(p, axis=-1, keepdims=True)
        os_ = _dot(p.astype(BF16), v)
        l = ls if l is None else l + ls
        o = os_ if o is None else o + os_
    return o / l


def _attn_kernel(lam_ref, q_ref, g_ref, *refs, n_seg, scale, diff, out_scale, row_parts):
    seg_refs = refs[:2 * n_seg]
    o_ref = refs[2 * n_seg]
    values = [seg_refs[2 * si + 1][...] for si in range(n_seg)]
    rows_per = q_ref.shape[0] // row_parts
    for r in range(row_parts):
        rs = slice(r * rows_per, (r + 1) * rows_per)
        q = q_ref[rs, :].astype(F32) * scale
        if not diff:
            qb = q.astype(BF16)
            scores = [_dot_nt(qb, seg_refs[2 * si][...]) for si in range(n_seg)]
            o_ref[rs, :] = _softmax_pv(scores, values).astype(o_ref.dtype)
            continue
        first = lax.broadcasted_iota(jnp.int32, q.shape, 1) < DIFF_HEAD
        q1 = jnp.where(first, q, 0.0).astype(BF16)
        q2 = jnp.where(first, 0.0, q).astype(BF16)
        o1 = _softmax_pv([_dot_nt(q1, seg_refs[2 * si][...]) for si in range(n_seg)], values)
        o2 = _softmax_pv([_dot_nt(q2, seg_refs[2 * si][...]) for si in range(n_seg)], values)
        o = o1 - lam_ref[0] * o2
        o_ref[rs, :] = (_rms(o) * g_ref[...] * out_scale).astype(o_ref.dtype)


def attention(q, k, v, *, heads, batch, q_rows, q_first, segments, scale, tq=256, row_parts=1,
              diff_lam=None, diff_gain=None, out_scale=1.0):
    dqk = q.shape[1] // heads
    dv = v.shape[1] // heads
    tq = min(tq, q_rows)
    nq = q_rows // tq
    if tq % (row_parts * 2 * SUBLANES):
        row_parts = 1
    diff = diff_lam is not None
    lam = (diff_lam if diff else jnp.zeros((), F32)).reshape(1).astype(F32)
    gain = (diff_gain if diff else jnp.ones((dv,), F32)).reshape(1, dv).astype(F32)
    seg_specs = []
    for seg_rows, first in segments:
        for w in (dqk, dv):
            seg_specs.append(pl.BlockSpec((seg_rows, w), functools.partial(lambda b, h, i, f: (f + b, h), f=first)))
    return pl.pallas_call(
        functools.partial(_attn_kernel, n_seg=len(segments), scale=scale, diff=diff, out_scale=out_scale,
                          row_parts=row_parts),
        grid=(batch, heads, nq),
        in_specs=[pl.BlockSpec(memory_space=pltpu.SMEM),
                  pl.BlockSpec((tq, dqk), lambda b, h, i: ((q_first + b) * nq + i, h)),
                  pl.BlockSpec((1, dv), lambda b, h, i: (0, 0))] + seg_specs,
        out_specs=pl.BlockSpec((tq, dv), lambda b, h, i: (b * nq + i, h)),
        out_shape=jax.ShapeDtypeStruct((batch * q_rows, heads * dv), BF16),
        compiler_params=_params("arbitrary", "arbitrary", "arbitrary"),
        name="diff_attention" if diff else "mla_attention",
    )(lam, q, gain, *([k, v] * len(segments)))


ROPE_GROUP = 64
ROPE_QUARTER = ROPE_GROUP // 4


def _axial_rope_tables(n_tok):
    t = jnp.arange(n_tok)
    row = (t // GRID_W).astype(F32)
    col = (t % GRID_W).astype(F32)
    axis_dim = ROPE_GROUP // 2
    inv = 1.0 / (ROPE_BASE ** (jnp.arange(0, axis_dim, 2, dtype=F32) / axis_dim))
    ar = row[:, None] * inv[None]
    ac = col[:, None] * inv[None]
    ang = jnp.concatenate([ar, ar, ac, ac], axis=-1)
    sign = jnp.where((jnp.arange(ROPE_GROUP) % (2 * ROPE_QUARTER)) < ROPE_QUARTER, -1.0, 1.0)
    return jnp.cos(ang), jnp.sin(ang) * sign


def _rope_lanes(x, cos, sin_signed):
    lane = lax.broadcasted_iota(jnp.int32, x.shape, 1)
    takes_next = (lane % (2 * ROPE_QUARTER)) < ROPE_QUARTER
    rot = jnp.where(takes_next, pltpu.roll(x, LANES - ROPE_QUARTER, 1), pltpu.roll(x, ROPE_QUARTER, 1))
    return x * cos + rot * sin_signed


def _rope_table_spec(rows, tm):
    per_seq = rows.n_lat // tm
    return pl.BlockSpec((tm, LANES), lambda i: (jnp.where(i * tm < rows.m_lat, i % per_seq, 0), 0))


def _tile_tables(i, tm, rows, cos_ref, sin_ref):
    is_lat = i * tm < rows.m_lat
    return jnp.where(is_lat, cos_ref[...], 1.0), jnp.where(is_lat, sin_ref[...], 0.0)


def _diff_prep_kernel(q_ref, k_ref, v_ref, cos_ref, sin_ref, qo_ref, ko_ref, vo_ref, *, rows):
    tm = q_ref.shape[0]
    cos, sin = _tile_tables(pl.program_id(0), tm, rows, cos_ref, sin_ref)
    for src, dst in ((q_ref, qo_ref), (k_ref, ko_ref)):
        for cb in range(src.shape[1] // LANES):
            cols = slice(cb * LANES, (cb + 1) * LANES)
            dst[:, cols] = _rope_lanes(src[:, cols], cos, sin).astype(dst.dtype)
    vo_ref[...] = v_ref[...].astype(vo_ref.dtype)


def diff_prep(pd, rows, cos2, sin2, *, tm=PREP_ROWS):
    m = pd.shape[0]
    bw = pd.shape[1] // 3
    tm = min(tm, rows.n_ctx)
    assert m % tm == 0 and rows.n_ctx % tm == 0 and rows.n_lat % tm == 0
    out = jax.ShapeDtypeStruct((m, bw), BF16)
    one = pl.BlockSpec((tm, bw), lambda i: (i, 0))
    tab = _rope_table_spec(rows, tm)
    return pl.pallas_call(
        functools.partial(_diff_prep_kernel, rows=rows),
        grid=(m // tm,),
        in_specs=[pl.BlockSpec((tm, bw), functools.partial(lambda i, j: (i, j), j=j)) for j in range(3)] + [tab, tab],
        out_specs=[one] * 3,
        out_shape=[out] * 3,
        compiler_params=_params("arbitrary"),
        name="diff_prep",
    )(pd, pd, pd, cos2, sin2)


def _mla_q_kernel(cq_ref, g_ref, w_ref, cos_ref, sin_ref, o_ref, *, rows):
    tm = cq_ref.shape[0]
    cos, sin = _tile_tables(pl.program_id(0), tm, rows, cos_ref, sin_ref)
    q = _dot((_rms(cq_ref[...]) * g_ref[...]).astype(BF16), w_ref[...])
    for h in range(MLA_HEADS):
        nope = slice(2 * h * LANES, (2 * h + 1) * LANES)
        rope = slice((2 * h + 1) * LANES, (2 * h + 2) * LANES)
        o_ref[:, nope] = q[:, nope].astype(o_ref.dtype)
        o_ref[:, rope] = _rope_lanes(q[:, rope], cos, sin).astype(o_ref.dtype)


def mla_q_prep(pb, rows, q_norm, w_uq, cos2, sin2, *, tm=PREP_ROWS):
    m = pb.shape[0]
    tm = min(tm, rows.n_ctx)
    w = w_uq.reshape(MLA_Q_RANK, MLA_HEADS, MLA_NOPE + MLA_ROPE)
    w = jnp.pad(w, ((0, 0), (0, 0), (0, 2 * LANES - MLA_NOPE - MLA_ROPE))).reshape(MLA_Q_RANK, -1).astype(BF16)
    tab = _rope_table_spec(rows, tm)
    return pl.pallas_call(
        functools.partial(_mla_q_kernel, rows=rows),
        grid=(m // tm,),
        in_specs=[pl.BlockSpec((tm, MLA_Q_RANK), lambda i: (i, 0)), _full((1, MLA_Q_RANK)), _full(w.shape), tab, tab],
        out_specs=pl.BlockSpec((tm, w.shape[1]), lambda i: (i, 0)),
        out_shape=jax.ShapeDtypeStruct((m, w.shape[1]), BF16),
        compiler_params=_params("arbitrary"),
        name="mla_q_prep",
    )(pb, q_norm.reshape(1, MLA_Q_RANK), w, cos2, sin2)


def _mla_kv_kernel(ckv_ref, kr_ref, g_ref, w_ref, cos_ref, sin_ref, k_ref, v_ref, *, rows):
    tm = ckv_ref.shape[0]
    cos, sin = _tile_tables(pl.program_id(0), tm, rows, cos_ref, sin_ref)
    kv = _dot((_rms(ckv_ref[...]) * g_ref[...]).astype(BF16), w_ref[...])
    lane = lax.broadcasted_iota(jnp.int32, (tm, LANES), 1)
    kr = jnp.where(lane < MLA_ROPE, _rope_lanes(kr_ref[...], cos, sin), 0.0).astype(k_ref.dtype)
    nope_w = MLA_HEADS * MLA_NOPE
    for h in range(MLA_HEADS):
        k_ref[:, 2 * h * LANES:(2 * h + 1) * LANES] = kv[:, h * MLA_NOPE:(h + 1) * MLA_NOPE].astype(k_ref.dtype)
        k_ref[:, (2 * h + 1) * LANES:(2 * h + 2) * LANES] = kr
    v_ref[...] = kv[:, nope_w:].astype(v_ref.dtype)


def mla_kv_prep(pb, rows, kv_norm, w_ukv, cos2, sin2, *, tm=PREP_ROWS):
    m = pb.shape[0]
    tm = min(tm, rows.n_ctx)
    assert MLA_NOPE == LANES and MLA_Q_RANK % MLA_KV_RANK == 0 and (MLA_Q_RANK + MLA_KV_RANK) % LANES == 0
    w = w_ukv.reshape(MLA_KV_RANK, MLA_HEADS, 2 * MLA_NOPE)
    w = jnp.concatenate([w[:, :, :MLA_NOPE].reshape(MLA_KV_RANK, -1), w[:, :, MLA_NOPE:].reshape(MLA_KV_RANK, -1)],
                        axis=1).astype(BF16)
    nope_w = MLA_HEADS * MLA_NOPE
    tab = _rope_table_spec(rows, tm)
    return pl.pallas_call(
        functools.partial(_mla_kv_kernel, rows=rows),
        grid=(m // tm,),
        in_specs=[pl.BlockSpec((tm, MLA_KV_RANK), lambda i: (i, MLA_Q_RANK // MLA_KV_RANK)),
                  pl.BlockSpec((tm, LANES), lambda i: (i, (MLA_Q_RANK + MLA_KV_RANK) // LANES)),
                  _full((1, MLA_KV_RANK)), _full(w.shape), tab, tab],
        out_specs=[pl.BlockSpec((tm, 2 * nope_w), lambda i: (i, 0)), pl.BlockSpec((tm, nope_w), lambda i: (i, 0))],
        out_shape=[jax.ShapeDtypeStruct((m, 2 * nope_w), BF16), jax.ShapeDtypeStruct((m, nope_w), BF16)],
        compiler_params=_params("arbitrary"),
        name="mla_kv_prep",
    )(pb, pb, kv_norm.reshape(1, MLA_KV_RANK), w, cos2, sin2)


def kernel(x, c, ctx, c_ctx, ada_w, ada_b, norm_mix_pre, norm_mix_post, norm_ffn_pre, norm_ffn_post, w_in, rwkv_shift, rwkv_w0, rwkv_w_up, rwkv_a0, rwkv_a_up, rwkv_g_up, rwkv_k_k, rwkv_k_a, rwkv_r_k, rwkv_ln_g, rwkv_ln_b, mla_q_norm, mla_w_uq, mla_kv_norm, mla_w_ukv, ssd_conv_w, ssd_conv_b, ssd_dt_bias, ssd_a_log, ssd_d, ssd_norm, diff_lq1, diff_lk1, diff_lq2, diff_lk2, diff_subln, w_branch, w_gate, w_out, w_ff1, w_ff2):
    batch, n_lat, d = x.shape
    n_ctx = ctx.shape[1]
    depth = ada_w.shape[0]
    bw = d // N_BRANCH
    rows = _Rows(batch, n_lat, n_ctx)
    m_lat, m_all = rows.m_lat, rows.m
    ssd_heads = bw // SSD_HEAD
    gn = SSD_GROUPS * SSD_STATE
    rwkv_in = 3 * bw + RWKV_DECAY_RANK + RWKV_AAA_RANK + RWKV_GATE_RANK
    mla_in = MLA_Q_RANK + MLA_KV_RANK + MLA_ROPE
    ssd_in = 2 * bw + 2 * gn + ssd_heads
    diff_in = 3 * bw
    o_mla, o_ssd, o_diff = rwkv_in, rwkv_in + mla_in, rwkv_in + mla_in + ssd_in
    assert w_in.shape[-1] == o_diff + diff_in
    assert n_lat % SSD_CHUNK == 0 and n_ctx % SSD_CHUNK == 0

    assert MLA_ROPE == ROPE_GROUP and DIFF_HEAD == ROPE_GROUP
    cos2, sin2 = (jnp.tile(t, (1, LANES // ROPE_GROUP)) for t in _axial_rope_tables(n_lat))
    hid = jnp.arange(bw) // RWKV_HEAD
    seg_ones = (hid[:, None] == hid[None, :]).astype(BF16)

    xs = jnp.concatenate([x.reshape(m_lat, d), ctx.reshape(batch * n_ctx, d)], axis=0)
    cvec = jnp.zeros((8, d), F32).at[:batch].set(jax.nn.silu(c)).at[batch].set(jax.nn.silu(c_ctx))
    groups = batch + 1

    mods = [(matmul(cvec, ada_w, layer=l, tn=512) + ada_b[l])[:groups].reshape(groups, 1, 6 * d)
            for l in range(depth)]
    h = norm_modulate(xs, norm_mix_pre[0], mods[0], 0, 1, rows=m_all, rows_per_group=n_lat)

    for l in range(depth):
        ctx_out = l < depth - 1
        m_out = m_all if ctx_out else m_lat
        mod = mods[l]

        o_dt = o_diff - ssd_heads
        pa = matmul(h, w_in[l, :, :o_mla], tm=1024, tn=512)
        pb = matmul(h, jnp.concatenate([w_in[l, :, o_mla:o_ssd], w_in[l, :, o_dt:o_diff]], axis=1), tm=1024, tn=384)
        pc = matmul(h, w_in[l, :, o_ssd:o_dt])
        pd = matmul(h, w_in[l, :, o_diff:])

        prep = rwkv_prep(pa, rows, l, rwkv_shift, rwkv_w0, rwkv_w_up, rwkv_a0, rwkv_a_up, rwkv_g_up,
                         rwkv_k_k, rwkv_k_a, rwkv_r_k.reshape(depth, bw), seg_ones)
        y_rwkv = rwkv_scan(prep[:7], prep[7], rows)
        ya = rwkv_readout(y_rwkv, prep[9], prep[8], seg_ones, rwkv_ln_g[l], rwkv_ln_b[l], rows=m_out)

        q_m = mla_q_prep(pb, rows, mla_q_norm[l], mla_w_uq[l], cos2, sin2)
        k_m, vv = mla_kv_prep(pb, rows, mla_kv_norm[l], mla_w_ukv[l], cos2, sin2)
        ctx_blk = m_lat // n_ctx
        segs_lat = [(n_lat, 0), (n_ctx, ctx_blk)]
        mla_scale = (MLA_NOPE + MLA_ROPE) ** -0.5
        yb = attention(q_m, k_m, vv, heads=MLA_HEADS, batch=batch, q_rows=n_lat, q_first=0, segments=segs_lat,
                       scale=mla_scale, tq=512, row_parts=2)
        if ctx_out:
            yb_c = attention(q_m, k_m, vv, heads=MLA_HEADS, batch=batch, q_rows=n_ctx, q_first=ctx_blk,
                             segments=[(n_ctx, ctx_blk)], scale=mla_scale)
            yb = jnp.concatenate([yb, yb_c], axis=0)

        assert (2 * gn) % bw == 0
        xbc = ssd_prep(pc, rows, ssd_conv_w[l], ssd_conv_b[l], width=bw, first_block=1, n_blocks=1 + 2 * gn // bw)
        dt_raw = pb[:, mla_in:mla_in + ssd_heads]
        dts = jnp.stack([jax.nn.softplus(dt_raw + ssd_dt_bias[l, dr]) for dr in range(2)])
        las = dts * (-jnp.exp(ssd_a_log[l]))[:, None, :]
        y_ssd = ssd_scan(xbc, las, dts, rows, heads=ssd_heads)
        yc = ssd_readout(y_ssd, xbc, pc, ssd_d[l], ssd_norm[l], rows=m_out)

        q_d, k_d, v_d = diff_prep(pd, rows, cos2, sin2)
        lam_init = 0.8 - 0.6 * math.exp(-0.3 * l)
        lam = (jnp.exp(jnp.sum(diff_lq1[l] * diff_lk1[l])) - jnp.exp(jnp.sum(diff_lq2[l] * diff_lk2[l])) + lam_init)
        diff_kw = dict(heads=DIFF_HEADS, batch=batch, scale=DIFF_HEAD ** -0.5, diff_lam=lam,
                       diff_gain=diff_subln[l], out_scale=1.0 - lam_init)
        yd = attention(q_d, k_d, v_d, q_rows=n_lat, q_first=0, segments=segs_lat, tq=1024, row_parts=2, **diff_kw)
        if ctx_out:
            yd_c = attention(q_d, k_d, v_d, q_rows=n_ctx, q_first=ctx_blk, segments=[(n_ctx, ctx_blk)], **diff_kw)
            yd = jnp.concatenate([yd, yd_c], axis=0)

        gates = matmul(h, w_gate, layer=l, rows=m_out, act="sigmoid", out_dtype=BF16)
        merged = merge_branches(gates, [ya, yb, yc, yd], w_branch, l, rows=m_out)
        mix = matmul(merged, w_out, layer=l)
        xs, hf = norm_residual(xs, mix, norm_mix_post[l], mod, 2, rows=m_out, rows_per_group=n_lat,
                               next_norm=(norm_ffn_pre[l], mod, 3, 4))
        f1 = matmul(hf, w_ff1, layer=l, act="relu2", out_dtype=BF16)
        f2 = matmul(f1, w_ff2, layer=l, tn=512)
        if ctx_out:
            xs, h = norm_residual(xs, f2, norm_ffn_post[l], mod, 5, rows=m_out, rows_per_group=n_lat,
                                  next_norm=(norm_mix_pre[l + 1], mods[l + 1], 0, 1))
        else:
            xs = norm_residual(xs, f2, norm_ffn_post[l], mod, 5, rows=m_out, rows_per_group=n_lat)

    return xs[:m_lat].reshape(batch, n_lat, d)
```

```python
import functools
import math

import jax
import jax.numpy as jnp
from jax import lax
from jax.experimental import pallas as pl
from jax.experimental.pallas import tpu as pltpu

F32 = jnp.float32
BF16 = jnp.bfloat16
HIGHEST = lax.Precision.HIGHEST

NORM_EPS = 1e-6
ROPE_BASE = 10000.0
GRID_W = 64
N_BRANCH = 4
LANES = 128
SUBLANES = 8

RWKV_HEAD = 64
RWKV_DECAY_RANK = 64
RWKV_AAA_RANK = 64
RWKV_GATE_RANK = 128
RWKV_LN_EPS = 64e-5
RWKV_CHUNK = 64

MLA_HEADS = 8
MLA_NOPE = 128
MLA_ROPE = 64
MLA_Q_RANK = 768
MLA_KV_RANK = 256

SSD_HEAD = 64
SSD_GROUPS = 4
SSD_STATE = 128
SSD_CHUNK = 128

DIFF_HEADS = 8
DIFF_HEAD = 64

PREP_ROWS = 256

VMEM_LIMIT_BYTES = 56 * 1024 * 1024


def _params(*sem):
    return pltpu.CompilerParams(dimension_semantics=sem, vmem_limit_bytes=VMEM_LIMIT_BYTES)


def _dot(a, b):
    return jnp.dot(a, b, preferred_element_type=F32)


def _dot_nt(a, b):
    return lax.dot_general(a, b, (((1,), (1,)), ((), ())), preferred_element_type=F32)


def _dot_tn(a, b):
    return lax.dot_general(a, b, (((0,), (0,)), ((), ())), preferred_element_type=F32)


def _dot_hi(a, b):
    return jnp.dot(a, b, preferred_element_type=F32, precision=HIGHEST)


def _split_bf16(x, terms):
    parts = []
    for _ in range(terms):
        p = x.astype(BF16)
        parts.append(p)
        x = x - p.astype(F32)
    return parts


def _dot_split_lhs(x, e, terms=2):
    return sum(_dot(p, e) for p in _split_bf16(x, terms))


def _dot_split_rhs(e, x, terms=2):
    return sum(_dot(e, p) for p in _split_bf16(x, terms))


def _dot3(a, b):
    a_hi, a_lo = _split_bf16(a, 2)
    b_hi, b_lo = _split_bf16(b, 2)
    return _dot(a_hi, b_hi) + (_dot(a_lo, b_hi) + _dot(a_hi, b_lo))


def _full(shape):
    nd = len(shape)
    return pl.BlockSpec(shape, lambda *_: (0,) * nd)


def _apply_act(y, act):
    if act is None:
        return y
    if act == "sigmoid":
        return 0.5 * jnp.tanh((0.5 * y).astype(BF16)) + 0.5
    if act == "relu2":
        return jnp.square(jnp.maximum(y, 0.0))
    raise ValueError(act)


def _mm_kernel(x_ref, w_ref, o_ref, wbf_ref, *acc, nk, act):
    k = pl.program_id(1)
    i = pl.program_id(2)

    @pl.when(i == 0)
    def _():
        wbf_ref[...] = w_ref[...].astype(BF16)

    y = _dot(x_ref[...].astype(BF16), wbf_ref[...])
    if nk == 1:
        o_ref[...] = _apply_act(y, act).astype(o_ref.dtype)
        return
    (acc_ref,) = acc
    tm = x_ref.shape[0]
    rows = pl.ds(pl.multiple_of(i * tm, tm), tm)

    @pl.when(k == 0)
    def _():
        acc_ref[rows, :] = y

    @pl.when(jnp.logical_and(k > 0, k < nk - 1))
    def _():
        acc_ref[rows, :] += y

    @pl.when(k == nk - 1)
    def _():
        o_ref[...] = _apply_act(acc_ref[rows, :] + y, act).astype(o_ref.dtype)


def matmul(x, w, *, layer=None, rows=None, tm=512, tn=1024, tk=4096, act=None, out_dtype=F32):
    m = x.shape[0] if rows is None else rows
    kdim = x.shape[1]
    n = w.shape[-1]
    assert w.shape[-2] == kdim
    tm = math.gcd(tm, m)
    tk = min(tk, kdim)
    tn = min(tn, n)
    assert m % tm == 0 and kdim % tk == 0
    ni, nk, nj = m // tm, kdim // tk, pl.cdiv(n, tn)
    if w.ndim == 3:
        w_spec = pl.BlockSpec((None, tk, tn), lambda j, k, i: (layer, k, j))
    else:
        w_spec = pl.BlockSpec((tk, tn), lambda j, k, i: (k, j))
    if nk == 1:
        o_map = lambda j, k, i: (i, j)
    else:
        o_map = lambda j, k, i: (jnp.where(k == nk - 1, i, 0), j)
    scratch = [pltpu.VMEM((tk, tn), BF16)]
    if nk > 1:
        scratch.append(pltpu.VMEM((m, tn), F32))
    return pl.pallas_call(
        functools.partial(_mm_kernel, nk=nk, act=act),
        grid=(nj, nk, ni),
        in_specs=[pl.BlockSpec((tm, tk), lambda j, k, i: (i, k)), w_spec],
        out_specs=pl.BlockSpec((tm, tn), o_map),
        out_shape=jax.ShapeDtypeStruct((m, n), out_dtype),
        scratch_shapes=scratch,
        compiler_params=_params("arbitrary", "arbitrary", "arbitrary"),
        name="matmul_ws",
    )(x, w)


def _merge_kernel(g0, g1, g2, g3, y0, y1, y2, y3, w_ref, o_ref, wbf_ref):
    @pl.when(pl.program_id(1) == 0)
    def _():
        wbf_ref[...] = w_ref[...].astype(BF16)

    acc = None
    for b, (g, y) in enumerate(((g0, y0), (g1, y1), (g2, y2), (g3, y3))):
        t = g[...].astype(F32) * _dot(y[...].astype(BF16), wbf_ref[b])
        acc = t if acc is None else acc + t
    o_ref[...] = acc.astype(o_ref.dtype)


def merge_branches(gates, ys, w_branch, layer, *, rows, tm=1024, tn=512):
    d = w_branch.shape[-1]
    bw = w_branch.shape[-2]
    tm = math.gcd(tm, rows)
    assert rows % tm == 0 and d % tn == 0
    nj = d // tn
    g_specs = [pl.BlockSpec((tm, tn), functools.partial(lambda j, i, b: (i, b * nj + j), b=b))
               for b in range(N_BRANCH)]
    y_specs = [pl.BlockSpec((tm, bw), lambda j, i: (i, 0)) for _ in range(N_BRANCH)]
    return pl.pallas_call(
        _merge_kernel,
        grid=(nj, rows // tm),
        in_specs=g_specs + y_specs + [pl.BlockSpec((None, N_BRANCH, bw, tn), lambda j, i: (layer, 0, 0, j))],
        out_specs=pl.BlockSpec((tm, tn), lambda j, i: (i, j)),
        out_shape=jax.ShapeDtypeStruct((rows, d), BF16),
        scratch_shapes=[pltpu.VMEM((N_BRANCH, bw, tn), BF16)],
        compiler_params=_params("arbitrary", "arbitrary"),
        name="merge_branches",
    )(gates, gates, gates, gates, *ys, w_branch)


def _rms(x, eps=NORM_EPS):
    return x * lax.rsqrt(jnp.mean(x * x, axis=-1, keepdims=True) + eps)


def _norm_mod_kernel(x_ref, g_ref, shift_ref, scale_ref, o_ref):
    x = x_ref[...].astype(F32)
    y = _rms(x) * g_ref[...]
    o_ref[...] = (y * (1.0 + scale_ref[...]) + shift_ref[...]).astype(o_ref.dtype)


def _group_of_rows(tm, rows_per_group, n_groups):
    return lambda i: jnp.minimum((i * tm) // rows_per_group, n_groups - 1)


def norm_modulate(x, g, mod, shift_idx, scale_idx, *, rows, rows_per_group, tm=PREP_ROWS):
    d = x.shape[1]
    tm = min(tm, rows_per_group)
    assert rows % tm == 0 and rows_per_group % tm == 0
    grp = _group_of_rows(tm, rows_per_group, mod.shape[0])
    return pl.pallas_call(
        _norm_mod_kernel,
        grid=(rows // tm,),
        in_specs=[pl.BlockSpec((tm, d), lambda i: (i, 0)),
                  pl.BlockSpec((1, d), lambda i: (0, 0)),
                  pl.BlockSpec((None, 1, d), lambda i: (grp(i), 0, shift_idx)),
                  pl.BlockSpec((None, 1, d), lambda i: (grp(i), 0, scale_idx))],
        out_specs=pl.BlockSpec((tm, d), lambda i: (i, 0)),
        out_shape=jax.ShapeDtypeStruct((rows, d), BF16),
        compiler_params=_params("arbitrary"),
        name="norm_modulate",
    )(x, g.reshape(1, d), mod, mod)


def _norm_res_kernel(x_ref, y_ref, g_ref, m_ref, *rest):
    y = _rms(y_ref[...].astype(F32)) * g_ref[...]
    x = x_ref[...] + m_ref[...] * y
    if len(rest) == 1:
        rest[0][...] = x
        return
    g2_ref, shift_ref, scale_ref, o_ref, h_ref = rest
    o_ref[...] = x
    h_ref[...] = (_rms(x) * g2_ref[...] * (1.0 + scale_ref[...]) + shift_ref[...]).astype(h_ref.dtype)


def norm_residual(x, y, g, mod, gate_idx, *, rows, rows_per_group, tm=PREP_ROWS, next_norm=None):
    d = x.shape[1]
    tm = min(tm, rows_per_group)
    assert rows % tm == 0 and rows_per_group % tm == 0
    grp = _group_of_rows(tm, rows_per_group, mod.shape[0])
    tile = pl.BlockSpec((tm, d), lambda i: (i, 0))
    vec = pl.BlockSpec((1, d), lambda i: (0, 0))
    mod_spec = lambda idx: pl.BlockSpec((None, 1, d), lambda i: (grp(i), 0, idx))
    in_specs = [tile, tile, vec, mod_spec(gate_idx)]
    args = [x, y, g.reshape(1, d), mod]
    out_specs, out_shape = tile, jax.ShapeDtypeStruct((rows, d), F32)
    if next_norm is not None:
        g2, mod2, shift_idx, scale_idx = next_norm
        in_specs += [vec, mod_spec(shift_idx), mod_spec(scale_idx)]
        args += [g2.reshape(1, d), mod2, mod2]
        out_specs, out_shape = [tile, tile], [out_shape, jax.ShapeDtypeStruct((rows, d), BF16)]
    return pl.pallas_call(
        _norm_res_kernel,
        grid=(rows // tm,),
        in_specs=in_specs,
        out_specs=out_specs,
        out_shape=out_shape,
        compiler_params=_params("arbitrary"),
        name="norm_residual",
    )(*args)


class _Rows:
    def __init__(self, batch, n_lat, n_ctx):
        self.batch, self.n_lat, self.n_ctx = batch, n_lat, n_ctx
        self.m_lat = batch * n_lat
        self.m = batch * (n_lat + n_ctx)

    def split(self, t):
        c = t.shape[-1]
        return (t[:self.m_lat].reshape(self.batch, self.n_lat, c),
                t[self.m_lat:].reshape(self.batch, self.n_ctx, c))

    def join(self, lat, ctx):
        c = lat.shape[-1]
        return jnp.concatenate([lat.reshape(self.m_lat, c), ctx.reshape(self.batch * self.n_ctx, c)], axis=0)

    def seq_edges(self, i, tm):
        r0 = i * tm
        in_lat = r0 < self.m_lat
        pos = jnp.where(in_lat, r0 % self.n_lat, (r0 - self.m_lat) % self.n_ctx)
        length = jnp.where(in_lat, self.n_lat, self.n_ctx)
        return pos == 0, pos + tm == length

    def chunk_block(self, chunk):
        ncc, nlc = self.n_ctx // chunk, self.n_lat // chunk
        ctx0 = self.m_lat // chunk

        def block(d, b, c):
            in_ctx = c < ncc
            cc = jnp.where(in_ctx, c, c - ncc)
            n_seg = jnp.where(in_ctx, ncc, nlc)
            cc = jnp.where(d == 1, n_seg - 1 - cc, cc)
            return jnp.where(in_ctx, ctx0 + b * ncc, b * nlc) + cc
        return block, ncc + nlc


def _conv3_tile(x, prev_row, next_row, w_ref, first, last):
    tm = x.shape[0]
    rid = lax.broadcasted_iota(jnp.int32, x.shape, 0)
    prev_row = jnp.where(first, 0.0, prev_row)
    next_row = jnp.where(last, 0.0, next_row)
    prev = jnp.where(rid == 0, prev_row, pltpu.roll(x, 1, 0))
    nxt = jnp.where(rid == tm - 1, next_row, pltpu.roll(x, tm - 1, 0))
    return prev * w_ref[0:1, :] + x * w_ref[1:2, :] + nxt * w_ref[2:3, :]


def _halo_specs(tm, width, col_block, n_rows):
    per = tm // SUBLANES
    last = n_rows // SUBLANES - 1
    return [pl.BlockSpec((SUBLANES, width), lambda i: (jnp.maximum(i * per - 1, 0), col_block)),
            pl.BlockSpec((SUBLANES, width), lambda i: (jnp.minimum((i + 1) * per, last), col_block))]


def _chunk_cumsum_matrix(tm, chunk, reverse):
    r = lax.broadcasted_iota(jnp.int32, (tm, tm), 0)
    c = lax.broadcasted_iota(jnp.int32, (tm, tm), 1)
    same = (r // chunk) == (c // chunk)
    tri = (r <= c) if reverse else (r >= c)
    return jnp.logical_and(same, tri).astype(BF16)


assert RWKV_CHUNK * math.exp(-0.5) < 80.0

def _rwkv_prep_kernel(rkv_ref, hp_ref, hn_ref, low_ref, shift_ref, w0_ref, a0_ref, wup_ref, aup_ref, gup_ref,
                      kk_ref, ka_ref, rk_ref, e_ref,
                      at_ref, rt_ref, bt_ref, kt_ref, bc_ref, kc_ref, gend_ref, v_ref, gate_ref, bonus_ref,
                      *, rows, chunk):
    tm = rkv_ref.shape[0]
    bw = v_ref.shape[1]
    first, last = rows.seq_edges(pl.program_id(0), tm)
    rkv = _conv3_tile(rkv_ref[...], hp_ref[SUBLANES - 1:SUBLANES, :], hn_ref[0:1, :], shift_ref, first, last)
    r, k, v = rkv[:, :bw], rkv[:, bw:2 * bw], rkv[:, 2 * bw:]
    e = e_ref[...]
    kk = k * kk_ref[...]
    kk = kk * lax.rsqrt(_dot_split_lhs(kk * kk, e) + 1e-12)
    low = low_ref[...]
    wd = low[:, :RWKV_DECAY_RANK]
    ad = low[:, RWKV_DECAY_RANK:RWKV_DECAY_RANK + RWKV_AAA_RANK]
    gd = low[:, RWKV_DECAY_RANK + RWKV_AAA_RANK:]
    w_pre = _dot3(jnp.tanh(wd), wup_ref[...])
    a_pre = _dot3(ad, aup_ref[...])
    gate_ref[...] = _dot3(jax.nn.sigmoid(gd), gup_ref[...])
    v_ref[...] = v.astype(BF16)
    kt_sum = None
    for d in range(2):
        w_log = -jax.nn.softplus(-(w0_ref[d:d + 1, :] + w_pre[:, d * bw:(d + 1) * bw])) - 0.5
        lw = -jnp.exp(w_log)
        a = jax.nn.sigmoid(a0_ref[d:d + 1, :] + a_pre[:, d * bw:(d + 1) * bw])
        kt = k * (1.0 + (a - 1.0) * ka_ref[...])
        kt_sum = kt if kt_sum is None else kt_sum + kt
        b = kk * a
        lam = _dot_split_rhs(_chunk_cumsum_matrix(tm, chunk, d == 1), lw)
        g_inv = jnp.exp(-lam)
        at_ref[d] = (-kk * jnp.exp(lam - lw)).astype(BF16)
        rt_ref[d] = (r * jnp.exp(lam)).astype(BF16)
        bt_ref[d] = (b * g_inv).astype(BF16)
        kt_ref[d] = (kt * g_inv).astype(BF16)
        for q in range(tm // chunk):
            end = q * chunk if d == 1 else (q + 1) * chunk - 1
            lam_end = lam[end:end + 1, :]
            sl = slice(q * chunk, (q + 1) * chunk)
            to_end = jnp.exp(lam_end - lam[sl])
            bc_ref[d, sl, :] = (b[sl] * to_end).astype(BF16)
            kc_ref[d, sl, :] = (kt[sl] * to_end).astype(BF16)
            gend_ref[d, q] = jnp.exp(lam_end)
    bonus_ref[...] = _dot_split_lhs(r * kt_sum * rk_ref[...], e, terms=1) * v


def rwkv_prep(pa, rows, layer, shift, w0, w_up, a0, a_up, g_up, k_k, k_a, r_k, seg_ones, *, tm=PREP_ROWS,
              chunk=RWKV_CHUNK):
    m = pa.shape[0]
    bw = k_k.shape[-1]
    low_w = RWKV_DECAY_RANK + RWKV_AAA_RANK + RWKV_GATE_RANK
    tm = min(tm, rows.n_ctx)
    assert m % tm == 0 and rows.n_ctx % tm == 0 and rows.n_lat % tm == 0 and tm % chunk == 0
    assert (3 * bw) % low_w == 0
    wup = jnp.concatenate([w_up[layer, 0], w_up[layer, 1]], axis=1)
    aup = jnp.concatenate([a_up[layer, 0], a_up[layer, 1]], axis=1)
    row = lambda t: t.reshape(1, bw)
    big = jax.ShapeDtypeStruct((2, m, bw), BF16)
    big_spec = pl.BlockSpec((2, tm, bw), lambda i: (0, i, 0))
    one_spec = pl.BlockSpec((tm, bw), lambda i: (i, 0))
    cpt = tm // chunk
    return pl.pallas_call(
        functools.partial(_rwkv_prep_kernel, rows=rows, chunk=chunk),
        grid=(m // tm,),
        in_specs=[pl.BlockSpec((tm, 3 * bw), lambda i: (i, 0))] + _halo_specs(tm, 3 * bw, 0, m)
        + [pl.BlockSpec((tm, low_w), lambda i: (i, 3 * bw // low_w)),
           _full((3, 3 * bw)), _full((2, bw)), _full((2, bw)), _full(wup.shape), _full(aup.shape),
           pl.BlockSpec((None,) + g_up.shape[1:], lambda i: (layer, 0, 0)),
           _full((1, bw)), _full((1, bw)), _full((1, bw)), _full(seg_ones.shape)],
        out_specs=[big_spec] * 6 + [pl.BlockSpec((2, cpt, 1, bw), lambda i: (0, i, 0, 0)),
                                    one_spec, one_spec, one_spec],
        out_shape=[big] * 6 + [jax.ShapeDtypeStruct((2, m // chunk, 1, bw), F32),
                               jax.ShapeDtypeStruct((m, bw), BF16), jax.ShapeDtypeStruct((m, bw), F32),
                               jax.ShapeDtypeStruct((m, bw), F32)],
        compiler_params=_params("arbitrary"),
        name="rwkv_prep",
    )(pa, pa, pa, pa, shift[layer], w0[layer], a0[layer], wup, aup, g_up, row(k_k[layer]), row(k_a[layer]),
      row(r_k[layer]), seg_ones)


def _pair_select(lo, z):
    half = z.shape[0] // 2
    return jnp.where(lo, z[:half], z[half:])


def _rwkv_scan_chunks(refs, y_ref, s_ref, *, chunk, reverse, order_of_chunks):
    at_ref, rt_ref, bt_ref, kt_ref, bc_ref, kc_ref, gend_ref, v_ref = refs
    c2 = 2 * chunk
    n_pairs = s_ref.shape[0]
    pw = 2 * RWKV_HEAD
    n_levels = chunk.bit_length() - 1
    order = lax.broadcasted_iota(jnp.int32, (chunk, chunk), 0) - lax.broadcasted_iota(jnp.int32, (chunk, chunk), 1)
    if reverse:
        order = -order
    strict = order > 0
    incl = order >= 0
    lo = lax.broadcasted_iota(jnp.int32, (chunk, pw), 1) < RWKV_HEAD
    lo2 = lax.broadcasted_iota(jnp.int32, (c2, pw), 1) < RWKV_HEAD
    own = ((lax.broadcasted_iota(jnp.int32, (pw, pw), 0) < RWKV_HEAD)
           == (lax.broadcasted_iota(jnp.int32, (pw, pw), 1) < RWKV_HEAD))
    pairs = range(n_pairs)
    sl = [slice(j * pw, (j + 1) * pw) for j in pairs]

    def tri(mask, z):
        return jnp.where(mask, z, 0.0)

    parts = {}
    for q in order_of_chunks:
        rs = slice(q * chunk, (q + 1) * chunk)
        a = [at_ref[rs, s] for s in sl]
        r = [rt_ref[rs, s] for s in sl]
        v = [v_ref[rs, s] for s in sl]
        ar = [jnp.concatenate([a[j], r[j]], axis=0) for j in pairs]
        ar_st = [jnp.concatenate([jnp.where(lo2, ar[j], 0), jnp.where(lo2, 0, ar[j])], axis=0) for j in pairs]
        p_b = [_dot_nt(ar_st[j], bt_ref[rs, sl[j]]) for j in pairs]
        p_k = [_dot_nt(ar_st[j], kt_ref[rs, sl[j]]) for j in pairs]
        lp = [[tri(strict, p_b[j][e * c2:e * c2 + chunk]) for e in range(2)] for j in pairs]
        l_ak = [jnp.concatenate([tri(strict, p_k[j][e * c2:e * c2 + chunk]) for e in range(2)], axis=0).astype(BF16)
                for j in pairs]
        m_rb = [jnp.concatenate([tri(incl, p_b[j][e * c2 + chunk:(e + 1) * c2]) for e in range(2)],
                                axis=0).astype(BF16) for j in pairs]
        m_rk = [jnp.concatenate([tri(incl, p_k[j][e * c2 + chunk:(e + 1) * c2]) for e in range(2)],
                                axis=0).astype(BF16) for j in pairs]
        x_w = [a[j].astype(F32) for j in pairs]
        x_u = [_pair_select(lo, _dot(l_ak[j], v[j])) for j in pairs]
        for lvl in range(n_levels):
            lpb = [[lp[j][e].astype(BF16) for e in range(2)] for j in pairs]
            lst = [jnp.concatenate(lpb[j], axis=0) for j in pairs]
            z = [_dot(lst[j], jnp.concatenate([x_w[j], x_u[j]], axis=1).astype(BF16)) for j in pairs]
            x_w = [x_w[j] + _pair_select(lo, z[j][:, :pw]) for j in pairs]
            x_u = [x_u[j] + _pair_select(lo, z[j][:, pw:]) for j in pairs]
            if lvl + 1 < n_levels:
                lp = [[_dot(lpb[j][e], lpb[j][e]) for e in range(2)] for j in pairs]
        parts[q] = (rs, r, v, m_rb, m_rk, [x.astype(BF16) for x in x_w], x_u)
    s = [s_ref[j] for j in pairs]
    for q in order_of_chunks:
        rs, r, v, m_rb, m_rk, x_wb, x_u = parts[q]
        sb = [s[j].astype(BF16) for j in pairs]
        u = [_dot_nt(x_wb[j], sb[j]) + x_u[j] for j in pairs]
        ub = [u[j].astype(BF16) for j in pairs]
        y = [_dot_nt(r[j], sb[j]) + _pair_select(lo, _dot(m_rb[j], ub[j])) + _pair_select(lo, _dot(m_rk[j], v[j]))
             for j in pairs]
        s = [s[j] * gend_ref[q, :, sl[j]]
             + jnp.where(own, _dot_tn(ub[j], bc_ref[rs, sl[j]]) + _dot_tn(v[j], kc_ref[rs, sl[j]]), 0.0)
             for j in pairs]
        for j in pairs:
            y_ref[rs, sl[j]] = y[j]
    for j in pairs:
        s_ref[j] = s[j]


def _rwkv_scan_kernel(*refs, chunk, per_step):
    y_ref, s_ref = refs[-2:]

    @pl.when(pl.program_id(2) == 0)
    def _():
        s_ref[...] = jnp.zeros_like(s_ref)

    @pl.when(pl.program_id(0) == 0)
    def _():
        _rwkv_scan_chunks(refs[:-2], y_ref, s_ref, chunk=chunk, reverse=False,
                          order_of_chunks=tuple(range(per_step)))

    @pl.when(pl.program_id(0) == 1)
    def _():
        _rwkv_scan_chunks(refs[:-2], y_ref, s_ref, chunk=chunk, reverse=True,
                          order_of_chunks=tuple(reversed(range(per_step))))


def rwkv_scan(ops, v, rows, *, chunk=RWKV_CHUNK, chunks_per_step=4):
    at, rt, bt, kt, bc, kc, gend = ops
    _, m, bw = at.shape
    per = math.gcd(chunks_per_step, math.gcd(rows.n_ctx // chunk, rows.n_lat // chunk))
    rows_per_step = per * chunk
    block, nc = rows.chunk_block(rows_per_step)
    big = pl.BlockSpec((None, rows_per_step, bw), lambda d, b, c: (d, block(d, b, c), 0))
    return pl.pallas_call(
        functools.partial(_rwkv_scan_kernel, chunk=chunk, per_step=per),
        grid=(2, rows.batch, nc),
        in_specs=[big] * 6 + [pl.BlockSpec((None, per, 1, bw), lambda d, b, c: (d, block(d, b, c), 0, 0)),
                              pl.BlockSpec((rows_per_step, bw), lambda d, b, c: (block(d, b, c), 0))],
        out_specs=big,
        out_shape=jax.ShapeDtypeStruct((2, m, bw), F32),
        scratch_shapes=[pltpu.VMEM((bw // (2 * RWKV_HEAD), 2 * RWKV_HEAD, 2 * RWKV_HEAD), F32)],
        compiler_params=_params("arbitrary", "arbitrary", "arbitrary"),
        name="rwkv_scan",
    )(at, rt, bt, kt, bc, kc, gend, v)


def _rwkv_readout_kernel(y_ref, bonus_ref, gate_ref, e_ref, g_ref, b_ref, o_ref):
    y = y_ref[0] + y_ref[1]
    e = e_ref[...]
    mu = _dot_split_lhs(y, e) * (1.0 / RWKV_HEAD)
    yc = y - mu
    var = _dot_split_lhs(yc * yc, e) * (1.0 / RWKV_HEAD)
    yn = yc * lax.rsqrt(var + RWKV_LN_EPS) * g_ref[...] + b_ref[...]
    o_ref[...] = ((yn + bonus_ref[...]) * gate_ref[...]).astype(o_ref.dtype)


def rwkv_readout(y, bonus, gate, seg_ones, ln_g, ln_b, *, rows, tm=PREP_ROWS):
    bw = y.shape[-1]
    tm = math.gcd(tm, rows)
    one = pl.BlockSpec((tm, bw), lambda i: (i, 0))
    return pl.pallas_call(
        _rwkv_readout_kernel,
        grid=(rows // tm,),
        in_specs=[pl.BlockSpec((2, tm, bw), lambda i: (0, i, 0)), one, one, _full(seg_ones.shape),
                  _full((1, bw)), _full((1, bw))],
        out_specs=one,
        out_shape=jax.ShapeDtypeStruct((rows, bw), BF16),
        compiler_params=_params("arbitrary"),
        name="rwkv_readout",
    )(y, bonus, gate, seg_ones, ln_g.reshape(1, bw), ln_b.reshape(1, bw))


def _ssd_prep_kernel(x_ref, hp_ref, hn_ref, w_ref, b_ref, o_ref, *, rows):
    first, last = rows.seq_edges(pl.program_id(0), x_ref.shape[0])
    y = _conv3_tile(x_ref[...], hp_ref[SUBLANES - 1:SUBLANES, :], hn_ref[0:1, :], w_ref, first, last) + b_ref[...]
    o_ref[...] = y * jax.nn.sigmoid(y)


def ssd_prep(pc, rows, conv_w, conv_b, *, width, first_block, n_blocks, tm=PREP_ROWS):
    m = pc.shape[0]
    tm = min(tm, rows.n_ctx)
    assert m % tm == 0 and rows.n_ctx % tm == 0 and rows.n_lat % tm == 0
    per = tm // SUBLANES
    last = m // SUBLANES - 1
    return pl.pallas_call(
        functools.partial(_ssd_prep_kernel, rows=rows),
        grid=(m // tm, n_blocks),
        in_specs=[pl.BlockSpec((tm, width), lambda i, j: (i, first_block + j)),
                  pl.BlockSpec((SUBLANES, width), lambda i, j: (jnp.maximum(i * per - 1, 0), first_block + j)),
                  pl.BlockSpec((SUBLANES, width), lambda i, j: (jnp.minimum((i + 1) * per, last), first_block + j)),
                  pl.BlockSpec((3, width), lambda i, j: (0, j)),
                  pl.BlockSpec((1, width), lambda i, j: (0, j))],
        out_specs=pl.BlockSpec((tm, width), lambda i, j: (i, j)),
        out_shape=jax.ShapeDtypeStruct((m, n_blocks * width), F32),
        compiler_params=_params("arbitrary", "arbitrary"),
        name="ssd_prep",
    )(pc, pc, pc, conv_w, conv_b.reshape(1, n_blocks * width))


def _ssd_scan_chunks(x_ref, b_ref, c_ref, la_ref, lat_ref, dt_ref, y_ref, s_ref, *, chunk, heads, groups, reverse,
                     order_of_chunks):
    pw = 2 * SSD_HEAD
    order = lax.broadcasted_iota(jnp.int32, (chunk, chunk), 0) - lax.broadcasted_iota(jnp.int32, (chunk, chunk), 1)
    if reverse:
        order = -order
    upto = order >= 0
    lo = lax.broadcasted_iota(jnp.int32, (chunk, pw), 1) < SSD_HEAD
    hpg = heads // groups
    st = [s_ref[j] for j in range(heads // 2)]
    for q in order_of_chunks:
        rs = slice(q * chunk, (q + 1) * chunk)
        acs_c_all = _dot_hi(upto.astype(F32), la_ref[rs, :])
        acs_r_all = _dot_hi(lat_ref[:, rs], (order <= 0).astype(F32))
        tot_c_all = acs_c_all[0:1, :] if reverse else acs_c_all[chunk - 1:chunk, :]
        dt_all = dt_ref[rs, :]
        for gi in range(groups):
            cg = c_ref[rs, gi * SSD_STATE:(gi + 1) * SSD_STATE].astype(BF16)
            bg = b_ref[rs, gi * SSD_STATE:(gi + 1) * SSD_STATE].astype(BF16)
            cb = _dot_nt(cg, bg)
            for qi in range(hpg // 2):
                h0 = gi * hpg + 2 * qi
                j = h0 // 2
                cols = slice(j * pw, (j + 1) * pw)
                w_st, acs_c, tot = [], [], []
                for e in range(2):
                    h = h0 + e
                    acs_c.append(acs_c_all[:, h:h + 1])
                    tot.append(tot_c_all[:, h:h + 1])
                    decay = jnp.exp(jnp.where(upto, acs_c[e] - acs_r_all[h:h + 1, :], -1e30))
                    w_st.append((cb * decay).astype(BF16))
                xdt = x_ref[rs, cols] * jnp.where(lo, dt_all[:, h0:h0 + 1], dt_all[:, h0 + 1:h0 + 2])
                y = _pair_select(lo, _dot(jnp.concatenate(w_st, axis=0), xdt.astype(BF16)))
                y = y + _dot(cg, st[j].astype(BF16)) * jnp.where(lo, jnp.exp(acs_c[0]), jnp.exp(acs_c[1]))
                y_ref[rs, cols] = y
                to_end = jnp.where(lo, jnp.exp(tot[0] - acs_c[0]), jnp.exp(tot[1] - acs_c[1]))
                st[j] = (st[j] * jnp.where(lo[0:1, :], jnp.exp(tot[0]), jnp.exp(tot[1]))
                         + _dot_tn(bg, (xdt * to_end).astype(BF16)))
    for j in range(heads // 2):
        s_ref[j] = st[j]


def _ssd_scan_kernel(*refs, chunk, heads, groups, per_step):
    s_ref = refs[-1]

    @pl.when(pl.program_id(2) == 0)
    def _():
        s_ref[...] = jnp.zeros_like(s_ref)

    @pl.when(pl.program_id(0) == 0)
    def _():
        _ssd_scan_chunks(*refs, chunk=chunk, heads=heads, groups=groups, reverse=False,
                         order_of_chunks=tuple(range(per_step)))

    @pl.when(pl.program_id(0) == 1)
    def _():
        _ssd_scan_chunks(*refs, chunk=chunk, heads=heads, groups=groups, reverse=True,
                         order_of_chunks=tuple(reversed(range(per_step))))


def ssd_scan(xbc, la, dt, rows, *, heads, chunk=SSD_CHUNK, chunks_per_step=2):
    m = xbc.shape[0]
    bw = heads * SSD_HEAD
    gn = SSD_GROUPS * SSD_STATE
    assert bw % gn == 0
    per = math.gcd(chunks_per_step, math.gcd(rows.n_ctx // chunk, rows.n_lat // chunk))
    rows_per_step = per * chunk
    block, nc = rows.chunk_block(rows_per_step)
    lat = jnp.swapaxes(la, 1, 2)
    thin = pl.BlockSpec((None, rows_per_step, heads), lambda d, b, c: (d, block(d, b, c), 0))
    return pl.pallas_call(
        functools.partial(_ssd_scan_kernel, chunk=chunk, heads=heads, groups=SSD_GROUPS, per_step=per),
        grid=(2, rows.batch, nc),
        in_specs=[pl.BlockSpec((rows_per_step, bw), lambda d, b, c: (block(d, b, c), 0)),
                  pl.BlockSpec((rows_per_step, gn), lambda d, b, c: (block(d, b, c), bw // gn)),
                  pl.BlockSpec((rows_per_step, gn), lambda d, b, c: (block(d, b, c), bw // gn + 1)),
                  thin,
                  pl.BlockSpec((None, heads, rows_per_step), lambda d, b, c: (d, 0, block(d, b, c))),
                  thin],
        out_specs=pl.BlockSpec((None, rows_per_step, bw), lambda d, b, c: (d, block(d, b, c), 0)),
        out_shape=jax.ShapeDtypeStruct((2, m, bw), F32),
        scratch_shapes=[pltpu.VMEM((heads // 2, SSD_STATE, 2 * SSD_HEAD), F32)],
        compiler_params=_params("arbitrary", "arbitrary", "arbitrary"),
        name="ssd_scan",
    )(xbc, xbc, xbc, la, lat, dt)


def _ssd_readout_kernel(y_ref, x_ref, z_ref, d_ref, g_ref, o_ref):
    y = y_ref[0] + y_ref[1] + d_ref[...] * x_ref[...]
    z = z_ref[...]
    o_ref[...] = (_rms(y * (z * jax.nn.sigmoid(z))) * g_ref[...]).astype(o_ref.dtype)


def ssd_readout(y, xbc, pc, d_skip, norm_g, *, rows, tm=PREP_ROWS):
    bw = y.shape[-1]
    tm = math.gcd(tm, rows)
    one = pl.BlockSpec((tm, bw), lambda i: (i, 0))
    return pl.pallas_call(
        _ssd_readout_kernel,
        grid=(rows // tm,),
        in_specs=[pl.BlockSpec((2, tm, bw), lambda i: (0, i, 0)), one, one, _full((1, bw)), _full((1, bw))],
        out_specs=one,
        out_shape=jax.ShapeDtypeStruct((rows, bw), BF16),
        compiler_params=_params("arbitrary"),
        name="ssd_readout",
    )(y, xbc, pc, jnp.repeat(d_skip, SSD_HEAD).reshape(1, bw), norm_g.reshape(1, bw))


def _softmax_pv(scores, values):
    m = None
    for s in scores:
        ms = jnp.max(s, axis=-1, keepdims=True)
        m = ms if m is None else jnp.maximum(m, ms)
    l, o = None, None
    for s, v in zip(scores, values):
        p = jnp.exp(s - m)
        ls = jnp.sum(p, axis=-1, keepdims=True)
        os_ = _dot(p.astype(BF16), v)
        l = ls if l is None else l + ls
        o = os_ if o is None else o + os_
    return o / l


def _attn_kernel(lam_ref, q_ref, g_ref, *refs, n_seg, scale, diff, out_scale, row_parts):
    seg_refs = refs[:2 * n_seg]
    o_ref = refs[2 * n_seg]
    values = [seg_refs[2 * si + 1][...] for si in range(n_seg)]
    rows_per = q_ref.shape[0] // row_parts
    for r in range(row_parts):
        rs = slice(r * rows_per, (r + 1) * rows_per)
        q = q_ref[rs, :].astype(F32) * scale
        if not diff:
            qb = q.astype(BF16)
            scores = [_dot_nt(qb, seg_refs[2 * si][...]) for si in range(n_seg)]
            o_ref[rs, :] = _softmax_pv(scores, values).astype(o_ref.dtype)
            continue
        first = lax.broadcasted_iota(jnp.int32, q.shape, 1) < DIFF_HEAD
        q1 = jnp.where(first, q, 0.0).astype(BF16)
        q2 = jnp.where(first, 0.0, q).astype(BF16)
        o1 = _softmax_pv([_dot_nt(q1, seg_refs[2 * si][...]) for si in range(n_seg)], values)
        o2 = _softmax_pv([_dot_nt(q2, seg_refs[2 * si][...]) for si in range(n_seg)], values)
        o = o1 - lam_ref[0] * o2
        o_ref[rs, :] = (_rms(o) * g_ref[...] * out_scale).astype(o_ref.dtype)


def attention(q, k, v, *, heads, batch, q_rows, q_first, segments, scale, tq=256, row_parts=1,
              diff_lam=None, diff_gain=None, out_scale=1.0):
    dqk = q.shape[1] // heads
    dv = v.shape[1] // heads
    tq = min(tq, q_rows)
    nq = q_rows // tq
    if tq % (row_parts * 2 * SUBLANES):
        row_parts = 1
    diff = diff_lam is not None
    lam = (diff_lam if diff else jnp.zeros((), F32)).reshape(1).astype(F32)
    gain = (diff_gain if diff else jnp.ones((dv,), F32)).reshape(1, dv).astype(F32)
    seg_specs = []
    for seg_rows, first in segments:
        for w in (dqk, dv):
            seg_specs.append(pl.BlockSpec((seg_rows, w), functools.partial(lambda b, h, i, f: (f + b, h), f=first)))
    return pl.pallas_call(
        functools.partial(_attn_kernel, n_seg=len(segments), scale=scale, diff=diff, out_scale=out_scale,
                          row_parts=row_parts),
        grid=(batch, heads, nq),
        in_specs=[pl.BlockSpec(memory_space=pltpu.SMEM),
                  pl.BlockSpec((tq, dqk), lambda b, h, i: ((q_first + b) * nq + i, h)),
                  pl.BlockSpec((1, dv), lambda b, h, i: (0, 0))] + seg_specs,
        out_specs=pl.BlockSpec((tq, dv), lambda b, h, i: (b * nq + i, h)),
        out_shape=jax.ShapeDtypeStruct((batch * q_rows, heads * dv), BF16),
        compiler_params=_params("arbitrary", "arbitrary", "arbitrary"),
        name="diff_attention" if diff else "mla_attention",
    )(lam, q, gain, *([k, v] * len(segments)))


ROPE_GROUP = 64
ROPE_QUARTER = ROPE_GROUP // 4


def _axial_rope_tables(n_tok):
    t = jnp.arange(n_tok)
    row = (t // GRID_W).astype(F32)
    col = (t % GRID_W).astype(F32)
    axis_dim = ROPE_GROUP // 2
    inv = 1.0 / (ROPE_BASE ** (jnp.arange(0, axis_dim, 2, dtype=F32) / axis_dim))
    ar = row[:, None] * inv[None]
    ac = col[:, None] * inv[None]
    ang = jnp.concatenate([ar, ar, ac, ac], axis=-1)
    sign = jnp.where((jnp.arange(ROPE_GROUP) % (2 * ROPE_QUARTER)) < ROPE_QUARTER, -1.0, 1.0)
    return jnp.cos(ang), jnp.sin(ang) * sign


def _rope_lanes(x, cos, sin_signed):
    lane = lax.broadcasted_iota(jnp.int32, x.shape, 1)
    takes_next = (lane % (2 * ROPE_QUARTER)) < ROPE_QUARTER
    rot = jnp.where(takes_next, pltpu.roll(x, LANES - ROPE_QUARTER, 1), pltpu.roll(x, ROPE_QUARTER, 1))
    return x * cos + rot * sin_signed


def _rope_table_spec(rows, tm):
    per_seq = rows.n_lat // tm
    return pl.BlockSpec((tm, LANES), lambda i: (jnp.where(i * tm < rows.m_lat, i % per_seq, 0), 0))


def _tile_tables(i, tm, rows, cos_ref, sin_ref):
    is_lat = i * tm < rows.m_lat
    return jnp.where(is_lat, cos_ref[...], 1.0), jnp.where(is_lat, sin_ref[...], 0.0)


def _diff_prep_kernel(q_ref, k_ref, v_ref, cos_ref, sin_ref, qo_ref, ko_ref, vo_ref, *, rows):
    tm = q_ref.shape[0]
    cos, sin = _tile_tables(pl.program_id(0), tm, rows, cos_ref, sin_ref)
    for src, dst in ((q_ref, qo_ref), (k_ref, ko_ref)):
        for cb in range(src.shape[1] // LANES):
            cols = slice(cb * LANES, (cb + 1) * LANES)
            dst[:, cols] = _rope_lanes(src[:, cols], cos, sin).astype(dst.dtype)
    vo_ref[...] = v_ref[...].astype(vo_ref.dtype)


def diff_prep(pd, rows, cos2, sin2, *, tm=PREP_ROWS):
    m = pd.shape[0]
    bw = pd.shape[1] // 3
    tm = min(tm, rows.n_ctx)
    assert m % tm == 0 and rows.n_ctx % tm == 0 and rows.n_lat % tm == 0
    out = jax.ShapeDtypeStruct((m, bw), BF16)
    one = pl.BlockSpec((tm, bw), lambda i: (i, 0))
    tab = _rope_table_spec(rows, tm)
    return pl.pallas_call(
        functools.partial(_diff_prep_kernel, rows=rows),
        grid=(m // tm,),
        in_specs=[pl.BlockSpec((tm, bw), functools.partial(lambda i, j: (i, j), j=j)) for j in range(3)] + [tab, tab],
        out_specs=[one] * 3,
        out_shape=[out] * 3,
        compiler_params=_params("arbitrary"),
        name="diff_prep",
    )(pd, pd, pd, cos2, sin2)


def _mla_q_kernel(cq_ref, g_ref, w_ref, cos_ref, sin_ref, o_ref, *, rows):
    tm = cq_ref.shape[0]
    cos, sin = _tile_tables(pl.program_id(0), tm, rows, cos_ref, sin_ref)
    q = _dot((_rms(cq_ref[...]) * g_ref[...]).astype(BF16), w_ref[...])
    for h in range(MLA_HEADS):
        nope = slice(2 * h * LANES, (2 * h + 1) * LANES)
        rope = slice((2 * h + 1) * LANES, (2 * h + 2) * LANES)
        o_ref[:, nope] = q[:, nope].astype(o_ref.dtype)
        o_ref[:, rope] = _rope_lanes(q[:, rope], cos, sin).astype(o_ref.dtype)


def mla_q_prep(pb, rows, q_norm, w_uq, cos2, sin2, *, tm=PREP_ROWS):
    m = pb.shape[0]
    tm = min(tm, rows.n_ctx)
    w = w_uq.reshape(MLA_Q_RANK, MLA_HEADS, MLA_NOPE + MLA_ROPE)
    w = jnp.pad(w, ((0, 0), (0, 0), (0, 2 * LANES - MLA_NOPE - MLA_ROPE))).reshape(MLA_Q_RANK, -1).astype(BF16)
    tab = _rope_table_spec(rows, tm)
    return pl.pallas_call(
        functools.partial(_mla_q_kernel, rows=rows),
        grid=(m // tm,),
        in_specs=[pl.BlockSpec((tm, MLA_Q_RANK), lambda i: (i, 0)), _full((1, MLA_Q_RANK)), _full(w.shape), tab, tab],
        out_specs=pl.BlockSpec((tm, w.shape[1]), lambda i: (i, 0)),
        out_shape=jax.ShapeDtypeStruct((m, w.shape[1]), BF16),
        compiler_params=_params("arbitrary"),
        name="mla_q_prep",
    )(pb, q_norm.reshape(1, MLA_Q_RANK), w, cos2, sin2)


def _mla_kv_kernel(ckv_ref, kr_ref, g_ref, w_ref, cos_ref, sin_ref, k_ref, v_ref, *, rows):
    tm = ckv_ref.shape[0]
    cos, sin = _tile_tables(pl.program_id(0), tm, rows, cos_ref, sin_ref)
    kv = _dot((_rms(ckv_ref[...]) * g_ref[...]).astype(BF16), w_ref[...])
    lane = lax.broadcasted_iota(jnp.int32, (tm, LANES), 1)
    kr = jnp.where(lane < MLA_ROPE, _rope_lanes(kr_ref[...], cos, sin), 0.0).astype(k_ref.dtype)
    nope_w = MLA_HEADS * MLA_NOPE
    for h in range(MLA_HEADS):
        k_ref[:, 2 * h * LANES:(2 * h + 1) * LANES] = kv[:, h * MLA_NOPE:(h + 1) * MLA_NOPE].astype(k_ref.dtype)
        k_ref[:, (2 * h + 1) * LANES:(2 * h + 2) * LANES] = kr
    v_ref[...] = kv[:, nope_w:].astype(v_ref.dtype)


def mla_kv_prep(pb, rows, kv_norm, w_ukv, cos2, sin2, *, tm=PREP_ROWS):
    m = pb.shape[0]
    tm = min(tm, rows.n_ctx)
    assert MLA_NOPE == LANES and MLA_Q_RANK % MLA_KV_RANK == 0 and (MLA_Q_RANK + MLA_KV_RANK) % LANES == 0
    w = w_ukv.reshape(MLA_KV_RANK, MLA_HEADS, 2 * MLA_NOPE)
    w = jnp.concatenate([w[:, :, :MLA_NOPE].reshape(MLA_KV_RANK, -1), w[:, :, MLA_NOPE:].reshape(MLA_KV_RANK, -1)],
                        axis=1).astype(BF16)
    nope_w = MLA_HEADS * MLA_NOPE
    tab = _rope_table_spec(rows, tm)
    return pl.pallas_call(
        functools.partial(_mla_kv_kernel, rows=rows),
        grid=(m // tm,),
        in_specs=[pl.BlockSpec((tm, MLA_KV_RANK), lambda i: (i, MLA_Q_RANK // MLA_KV_RANK)),
                  pl.BlockSpec((tm, LANES), lambda i: (i, (MLA_Q_RANK + MLA_KV_RANK) // LANES)),
                  _full((1, MLA_KV_RANK)), _full(w.shape), tab, tab],
        out_specs=[pl.BlockSpec((tm, 2 * nope_w), lambda i: (i, 0)), pl.BlockSpec((tm, nope_w), lambda i: (i, 0))],
        out_shape=[jax.ShapeDtypeStruct((m, 2 * nope_w), BF16), jax.ShapeDtypeStruct((m, nope_w), BF16)],
        compiler_params=_params("arbitrary"),
        name="mla_kv_prep",
    )(pb, pb, kv_norm.reshape(1, MLA_KV_RANK), w, cos2, sin2)


def kernel(x, c, ctx, c_ctx, ada_w, ada_b, norm_mix_pre, norm_mix_post, norm_ffn_pre, norm_ffn_post, w_in, rwkv_shift, rwkv_w0, rwkv_w_up, rwkv_a0, rwkv_a_up, rwkv_g_up, rwkv_k_k, rwkv_k_a, rwkv_r_k, rwkv_ln_g, rwkv_ln_b, mla_q_norm, mla_w_uq, mla_kv_norm, mla_w_ukv, ssd_conv_w, ssd_conv_b, ssd_dt_bias, ssd_a_log, ssd_d, ssd_norm, diff_lq1, diff_lk1, diff_lq2, diff_lk2, diff_subln, w_branch, w_gate, w_out, w_ff1, w_ff2):
    batch, n_lat, d = x.shape
    n_ctx = ctx.shape[1]
    depth = ada_w.shape[0]
    bw = d // N_BRANCH
    rows = _Rows(batch, n_lat, n_ctx)
    m_lat, m_all = rows.m_lat, rows.m
    ssd_heads = bw // SSD_HEAD
    gn = SSD_GROUPS * SSD_STATE
    rwkv_in = 3 * bw + RWKV_DECAY_RANK + RWKV_AAA_RANK + RWKV_GATE_RANK
    mla_in = MLA_Q_RANK + MLA_KV_RANK + MLA_ROPE
    ssd_in = 2 * bw + 2 * gn + ssd_heads
    diff_in = 3 * bw
    o_mla, o_ssd, o_diff = rwkv_in, rwkv_in + mla_in, rwkv_in + mla_in + ssd_in
    assert w_in.shape[-1] == o_diff + diff_in
    assert n_lat % SSD_CHUNK == 0 and n_ctx % SSD_CHUNK == 0

    assert MLA_ROPE == ROPE_GROUP and DIFF_HEAD == ROPE_GROUP
    cos2, sin2 = (jnp.tile(t, (1, LANES // ROPE_GROUP)) for t in _axial_rope_tables(n_lat))
    hid = jnp.arange(bw) // RWKV_HEAD
    seg_ones = (hid[:, None] == hid[None, :]).astype(BF16)

    xs = jnp.concatenate([x.reshape(m_lat, d), ctx.reshape(batch * n_ctx, d)], axis=0)
    cvec = jnp.zeros((8, d), F32).at[:batch].set(jax.nn.silu(c)).at[batch].set(jax.nn.silu(c_ctx))
    groups = batch + 1

    mods = [(matmul(cvec, ada_w, layer=l, tn=512) + ada_b[l])[:groups].reshape(groups, 1, 6 * d)
            for l in range(depth)]
    h = norm_modulate(xs, norm_mix_pre[0], mods[0], 0, 1, rows=m_all, rows_per_group=n_lat)

    for l in range(depth):
        ctx_out = l < depth - 1
        m_out = m_all if ctx_out else m_lat
        mod = mods[l]

        o_dt = o_diff - ssd_heads
        pa = matmul(h, w_in[l, :, :o_mla], tm=1024, tn=512)
        pb = matmul(h, jnp.concatenate([w_in[l, :, o_mla:o_ssd], w_in[l, :, o_dt:o_diff]], axis=1), tm=1024, tn=384)
        pc = matmul(h, w_in[l, :, o_ssd:o_dt])
        pd = matmul(h, w_in[l, :, o_diff:])

        prep = rwkv_prep(pa, rows, l, rwkv_shift, rwkv_w0, rwkv_w_up, rwkv_a0, rwkv_a_up, rwkv_g_up,
                         rwkv_k_k, rwkv_k_a, rwkv_r_k.reshape(depth, bw), seg_ones)
        y_rwkv = rwkv_scan(prep[:7], prep[7], rows)
        ya = rwkv_readout(y_rwkv, prep[9], prep[8], seg_ones, rwkv_ln_g[l], rwkv_ln_b[l], rows=m_out)

        q_m = mla_q_prep(pb, rows, mla_q_norm[l], mla_w_uq[l], cos2, sin2)
        k_m, vv = mla_kv_prep(pb, rows, mla_kv_norm[l], mla_w_ukv[l], cos2, sin2)
        ctx_blk = m_lat // n_ctx
        segs_lat = [(n_lat, 0), (n_ctx, ctx_blk)]
        mla_scale = (MLA_NOPE + MLA_ROPE) ** -0.5
        yb = attention(q_m, k_m, vv, heads=MLA_HEADS, batch=batch, q_rows=n_lat, q_first=0, segments=segs_lat,
                       scale=mla_scale, tq=512, row_parts=2)
        if ctx_out:
            yb_c = attention(q_m, k_m, vv, heads=MLA_HEADS, batch=batch, q_rows=n_ctx, q_first=ctx_blk,
                             segments=[(n_ctx, ctx_blk)], scale=mla_scale)
            yb = jnp.concatenate([yb, yb_c], axis=0)

        assert (2 * gn) % bw == 0
        xbc = ssd_prep(pc, rows, ssd_conv_w[l], ssd_conv_b[l], width=bw, first_block=1, n_blocks=1 + 2 * gn // bw)
        dt_raw = pb[:, mla_in:mla_in + ssd_heads]
        dts = jnp.stack([jax.nn.softplus(dt_raw + ssd_dt_bias[l, dr]) for dr in range(2)])
        las = dts * (-jnp.exp(ssd_a_log[l]))[:, None, :]
        y_ssd = ssd_scan(xbc, las, dts, rows, heads=ssd_heads)
        yc = ssd_readout(y_ssd, xbc, pc, ssd_d[l], ssd_norm[l], rows=m_out)

        q_d, k_d, v_d = diff_prep(pd, rows, cos2, sin2)
        lam_init = 0.8 - 0.6 * math.exp(-0.3 * l)
        lam = (jnp.exp(jnp.sum(diff_lq1[l] * diff_lk1[l])) - jnp.exp(jnp.sum(diff_lq2[l] * diff_lk2[l])) + lam_init)
        diff_kw = dict(heads=DIFF_HEADS, batch=batch, scale=DIFF_HEAD ** -0.5, diff_lam=lam,
                       diff_gain=diff_subln[l], out_scale=1.0 - lam_init)
        yd = attention(q_d, k_d, v_d, q_rows=n_lat, q_first=0, segments=segs_lat, tq=1024, row_parts=2, **diff_kw)
        if ctx_out:
            yd_c = attention(q_d, k_d, v_d, q_rows=n_ctx, q_first=ctx_blk, segments=[(n_ctx, ctx_blk)], **diff_kw)
            yd = jnp.concatenate([yd, yd_c], axis=0)

        gates = matmul(h, w_gate, layer=l, rows=m_out, act="sigmoid", out_dtype=BF16)
        merged = merge_branches(gates, [ya, yb, yc, yd], w_branch, l, rows=m_out)
        mix = matmul(merged, w_out, layer=l)
        xs, hf = norm_residual(xs, mix, norm_mix_post[l], mod, 2, rows=m_out, rows_per_group=n_lat,
                               next_norm=(norm_ffn_pre[l], mod, 3, 4))
        f1 = matmul(hf, w_ff1, layer=l, act="relu2", out_dtype=BF16)
        f2 = matmul(f1, w_ff2, layer=l, tn=512)
        if ctx_out:
            xs, h = norm_residual(xs, f2, norm_ffn_post[l], mod, 5, rows=m_out, rows_per_group=n_lat,
                                  next_norm=(norm_mix_pre[l + 1], mods[l + 1], 0, 1))
        else:
            xs = norm_residual(xs, f2, norm_ffn_post[l], mod, 5, rows=m_out, rows_per_group=n_lat)

    return xs[:m_lat].reshape(batch, n_lat, d)
```
